```python
import math
import jax
import jax.numpy as jnp
from jax import lax
import numpy as np

D_MODEL = 1024
BATCH = 32
SEQ = 256
DEPTH = 2
DEC_BATCH = 8
DEC_SEQ = 1024
PAST_LEN = 256

GRID_W = 64
HEAD_DIM = 64
ROPE_BASE = 10000.0
EPS = 1e-6
GN_EPS = 1e-5
RET_HEADS = 8
RET_DK = 64
RET_DV = 128
RET_CHUNK = 128
GQA_HEADS = 8
GQA_KV_HEADS = 2
Q_BLOCK = 128
NA_HEADS = 16
NA_KH = 8
NA_KW = 16
NA_WIDTH = NA_HEADS * HEAD_DIM
D_FF = 2816
CONV_WIDTH = 3
N_EVEN = (DEPTH + 1) // 2
N_ODD = DEPTH // 2
EVEN_SPLITS = (RET_HEADS * RET_DK, RET_HEADS * RET_DK, RET_HEADS * RET_DV, RET_HEADS * RET_DV,
               GQA_HEADS * HEAD_DIM, GQA_KV_HEADS * HEAD_DIM, GQA_KV_HEADS * HEAD_DIM)
EVEN_IN = 3840
EVEN_OUT_IN = 1536

kernel_name = 'hybrid_diffusion_retention_gqa_natten_step'

F32 = jnp.float32


def rms_norm(x, g):
    x32 = x.astype(F32)
    y = x32 * lax.rsqrt(jnp.mean(x32 * x32, axis=-1, keepdims=True) + EPS)
    return (y * g.astype(F32)).astype(x.dtype)


def to_heads(x, n, d):
    b, t, _ = x.shape
    return x.reshape(b, t, n, d).transpose(0, 2, 1, 3)


def from_heads(x):
    b, h, t, d = x.shape
    return x.transpose(0, 2, 1, 3).reshape(b, t, h * d)


def axial_angles(n, head_dim):
    t = jnp.arange(n)
    row = (t // GRID_W).astype(F32)
    col = (t % GRID_W).astype(F32)
    half = head_dim // 2
    inv = ROPE_BASE ** (-jnp.arange(0, half, 2, dtype=F32) / half)
    return row[:, None] * inv, col[:, None] * inv


def _rotate(x, ang):
    x1, x2 = jnp.split(x, 2, axis=-1)
    cos = jnp.cos(ang).astype(x.dtype)
    sin = jnp.sin(ang).astype(x.dtype)
    return jnp.concatenate([x1 * cos - x2 * sin, x2 * cos + x1 * sin], axis=-1)


def apply_axial_rope(x, ang):
    ang_r, ang_c = ang
    half = x.shape[-1] // 2
    return jnp.concatenate([_rotate(x[..., :half], ang_r), _rotate(x[..., half:], ang_c)], axis=-1)


def retention_chunkwise(q, k, v, decay_logit, s0):
    b, h, n, _ = q.shape
    dv = v.shape[-1]
    c = RET_CHUNK
    nc = n // c
    log_g = jax.nn.log_sigmoid(decay_logit.astype(F32))
    pos = jnp.arange(c, dtype=F32)
    diff = pos[:, None] - pos[None, :]
    inner_decay = jnp.where(diff >= 0, jnp.exp(log_g[:, None, None] * jnp.maximum(diff, 0.0)),
                            0.0).astype(q.dtype)
    q_decay = jnp.exp(log_g[:, None] * (pos + 1.0))[..., None].astype(q.dtype)
    k_decay = jnp.exp(log_g[:, None] * (c - 1.0 - pos))[..., None].astype(q.dtype)
    chunk_decay = jnp.exp(log_g * c)[:, None, None].astype(q.dtype)

    def chunks(x):
        return x.reshape(b, h, nc, c, x.shape[-1]).transpose(2, 0, 1, 3, 4)

    def step(s, inp):
        qc, kc, vc = inp
        scores = jnp.einsum('bhid,bhjd->bhij', qc, kc) * inner_decay
        o = (jnp.einsum('bhij,bhje->bhie', scores, vc)
             + jnp.einsum('bhid,bhde->bhie', qc * q_decay, s))
        s = s * chunk_decay + jnp.einsum('bhjd,bhje->bhde', kc * k_decay, vc)
        return s, o

    s_fin, o = lax.scan(step, s0.astype(q.dtype), (chunks(q), chunks(k), chunks(v)))
    return o.transpose(1, 2, 0, 3, 4).reshape(b, h, n, dv), s_fin


def retention_group_norm(o, gain):
    o32 = o.astype(F32)
    mu = jnp.mean(o32, axis=-1, keepdims=True)
    var = jnp.mean(jnp.square(o32 - mu), axis=-1, keepdims=True)
    y = (o32 - mu) * lax.rsqrt(var + GN_EPS)
    return (from_heads(y) * gain.astype(F32)).astype(o.dtype)


def attention_blocked(q, k, v):
    b, h, n, d = q.shape
    kv = k.shape[1]
    g = h // kv
    nb = n // Q_BLOCK
    qb = q.reshape(b, kv, g, nb, Q_BLOCK, d).transpose(3, 0, 1, 2, 4, 5)
    scale = d ** -0.5

    def block(qi):
        s = jnp.einsum('bkgqd,bkmd->bkgqm', qi, k).astype(F32) * scale
        p = jax.nn.softmax(s, axis=-1).astype(v.dtype)
        return jnp.einsum('bkgqm,bkmd->bkgqd', p, v)

    o = lax.map(block, qb)
    return o.transpose(1, 2, 3, 0, 4, 5).reshape(b, h, n, d)


def na_tables(rows):
    kh = min(NA_KH, rows)
    kw = NA_KW
    r = np.arange(rows)
    cidx = np.arange(GRID_W)
    rs = np.clip(r - kh // 2, 0, rows - kh)
    cs = np.clip(cidx - kw // 2, 0, GRID_W - kw)
    key_r = rs[:, None] + np.arange(kh)
    key_c = cs[:, None] + np.arange(kw)
    idx = (key_r[:, None, :, None] * GRID_W + key_c[None, :, None, :]).reshape(rows, GRID_W, kh * kw)
    rel_r = key_r - r[:, None] + NA_KH - 1
    rel_c = key_c - cidx[:, None] + NA_KW - 1
    return (jnp.asarray(idx, jnp.int32), jnp.asarray(rel_r, jnp.int32), jnp.asarray(rel_c, jnp.int32))


def neighbourhood_attention(q, k, v, ctx_k, ctx_v, rpb):
    b, h, n, d = q.shape
    rows = n // GRID_W
    idx, rel_r, rel_c = na_tables(rows)
    n_win = idx.shape[-1]
    scale = d ** -0.5
    q_rows = q.reshape(b, h, rows, GRID_W, d).transpose(2, 0, 1, 3, 4)

    def row_block(inp):
        qi, idx_i, rel_r_i = inp
        kg = jnp.take(k, idx_i, axis=2)
        vg = jnp.take(v, idx_i, axis=2)
        bias = rpb[:, rel_r_i[None, :, None], rel_c[:, None, :]].reshape(h, GRID_W, n_win)
        s_win = jnp.einsum('bhwd,bhwkd->bhwk', qi, kg).astype(F32) * scale + bias.astype(F32)
        s_ctx = jnp.einsum('bhwd,bhmd->bhwm', qi, ctx_k).astype(F32) * scale
        p = jax.nn.softmax(jnp.concatenate([s_win, s_ctx], axis=-1), axis=-1).astype(v.dtype)
        return (jnp.einsum('bhwk,bhwkd->bhwd', p[..., :n_win], vg)
                + jnp.einsum('bhwm,bhmd->bhwd', p[..., n_win:], ctx_v))

    o = lax.map(row_block, (q_rows, idx, rel_r))
    return o.transpose(1, 2, 0, 3, 4).reshape(b, h, n, d)


def even_mixer(h, w_in, w_out, decay_f, decay_b, gn, q_gain, k_gain, s0_f, s0_b, ctx_k, ctx_v, ang):
    split_idx = [int(i) for i in np.cumsum(EVEN_SPLITS)[:-1]]
    qr, kr, vr, gr, qa, ka, va = jnp.split(h @ w_in, split_idx, axis=-1)
    qr = to_heads(qr, RET_HEADS, RET_DK)
    kr = to_heads(kr, RET_HEADS, RET_DK) * (RET_DK ** -0.5)
    vr = to_heads(vr, RET_HEADS, RET_DV)
    qa = rms_norm(to_heads(qa, GQA_HEADS, HEAD_DIM), q_gain)
    ka = rms_norm(to_heads(ka, GQA_KV_HEADS, HEAD_DIM), k_gain)
    va = to_heads(va, GQA_KV_HEADS, HEAD_DIM)
    if ang is None:
        keys, vals = ka, va
    else:
        qr = apply_axial_rope(qr, ang)
        kr = apply_axial_rope(kr, ang)
        qa = apply_axial_rope(qa, ang)
        keys = jnp.concatenate([apply_axial_rope(ka, ang), ctx_k], axis=2)
        vals = jnp.concatenate([va, ctx_v], axis=2)
    o_f, s_f = retention_chunkwise(qr, kr, vr, decay_f, s0_f)
    o_b, s_b = retention_chunkwise(qr[:, :, ::-1], kr[:, :, ::-1], vr[:, :, ::-1], decay_b, s0_b)
    ret = retention_group_norm(o_f + o_b[:, :, ::-1], gn) * jax.nn.silu(gr)
    att = from_heads(attention_blocked(qa, keys, vals))
    out = jnp.concatenate([ret, att], axis=-1) @ w_out
    return out, s_f, s_b, ka, va


def odd_mixer(h, w_in, w_out, rpb, ctx_k, ctx_v):
    q, k, v = jnp.split(h @ w_in, 3, axis=-1)
    q = to_heads(q, NA_HEADS, HEAD_DIM)
    k = to_heads(k, NA_HEADS, HEAD_DIM)
    v = to_heads(v, NA_HEADS, HEAD_DIM)
    if ctx_k is None:
        o = attention_blocked(q, k, v)
    else:
        o = neighbourhood_attention(q, k, v, ctx_k, ctx_v, rpb)
    return from_heads(o) @ w_out, k, v


def conv_ffn(h, w_up, conv_w, conv_b, w_down):
    u = h @ w_up
    up = jnp.pad(u, ((0, 0), (1, 1), (0, 0)))
    u = up[:, :-2] * conv_w[0] + up[:, 1:-1] * conv_w[1] + up[:, 2:] * conv_w[2] + conv_b
    a, g = jnp.split(u, 2, axis=-1)
    return (jax.nn.silu(a) * g) @ w_down


def ada_params(cond, w, b):
    m = jax.nn.silu(cond) @ w + b
    return jnp.split(m[:, None, :], 6, axis=-1)


def setup_inputs(seed: int = 0) -> dict:
    key = jax.random.key(seed)
    ks = iter(jax.random.split(key, 40))

    def nrm(shape, scale):
        return jax.random.normal(next(ks), shape, F32) * scale

    def gain(shape):
        return 1.0 + nrm(shape, 0.05)

    gam = 1.0 - 2.0 ** (-5.0 - jnp.arange(RET_HEADS, dtype=F32))
    decay_logit = jnp.log(gam) - jnp.log(1.0 - gam)
    conv_base = jnp.array([0.25, 0.5, 0.25], F32)[:, None]
    return {
        'x_prompt': nrm((BATCH, SEQ, D_MODEL), 1.0),
        'x_sample': nrm((DEC_BATCH, DEC_SEQ, D_MODEL), 1.0),
        'state_ret_fwd': nrm((DEC_BATCH, N_EVEN, RET_HEADS, RET_DK, RET_DV), 0.5),
        'state_ret_bwd': nrm((DEC_BATCH, N_EVEN, RET_HEADS, RET_DK, RET_DV), 0.5),
        'cache_gqa_k': nrm((DEC_BATCH, N_EVEN, GQA_KV_HEADS, PAST_LEN, HEAD_DIM), 1.0),
        'cache_gqa_v': nrm((DEC_BATCH, N_EVEN, GQA_KV_HEADS, PAST_LEN, HEAD_DIM), 1.0),
        'cache_na_k': nrm((DEC_BATCH, N_ODD, NA_HEADS, PAST_LEN, HEAD_DIM), 1.0),
        'cache_na_v': nrm((DEC_BATCH, N_ODD, NA_HEADS, PAST_LEN, HEAD_DIM), 1.0),
        'c': nrm((DEC_BATCH, D_MODEL), 1.0),
        'c_ctx': nrm((D_MODEL,), 1.0),
        'ada_w': nrm((DEPTH, D_MODEL, 6 * D_MODEL), 0.5 * D_MODEL ** -0.5),
        'ada_b': nrm((DEPTH, 6 * D_MODEL), 0.01),
        'norm_mix': gain((DEPTH, D_MODEL)),
        'norm_ffn': gain((DEPTH, D_MODEL)),
        'norm_final': gain((D_MODEL,)),
        'even_w_in': nrm((N_EVEN, D_MODEL, EVEN_IN), D_MODEL ** -0.5),
        'even_w_out': nrm((N_EVEN, EVEN_OUT_IN, D_MODEL), EVEN_OUT_IN ** -0.5),
        'ret_decay_fwd': decay_logit[None, :] + nrm((N_EVEN, RET_HEADS), 0.05),
        'ret_decay_bwd': decay_logit[None, :] + nrm((N_EVEN, RET_HEADS), 0.05),
        'ret_gn': gain((N_EVEN, RET_HEADS * RET_DV)),
        'gqa_q_norm': gain((N_EVEN, HEAD_DIM)),
        'gqa_k_norm': gain((N_EVEN, HEAD_DIM)),
        'odd_w_in': nrm((N_ODD, D_MODEL, 3 * NA_WIDTH), D_MODEL ** -0.5),
        'odd_w_out': nrm((N_ODD, NA_WIDTH, D_MODEL), NA_WIDTH ** -0.5),
        'na_rpb': nrm((N_ODD, NA_HEADS, 2 * NA_KH - 1, 2 * NA_KW - 1), 0.5),
        'ffn_w_up': nrm((DEPTH, D_MODEL, 2 * D_FF), D_MODEL ** -0.5),
        'ffn_conv_w': conv_base[None] + nrm((DEPTH, CONV_WIDTH, 2 * D_FF), 0.3),
        'ffn_conv_b': nrm((DEPTH, 2 * D_FF), 0.01),
        'ffn_w_down': nrm((DEPTH, D_FF, D_MODEL), D_FF ** -0.5),
    }


def reference(x_prompt, x_sample, state_ret_fwd, state_ret_bwd, cache_gqa_k, cache_gqa_v,
              cache_na_k, cache_na_v, c, c_ctx, ada_w, ada_b, norm_mix, norm_ffn, norm_final,
              even_w_in, even_w_out, ret_decay_fwd, ret_decay_bwd, ret_gn, gqa_q_norm, gqa_k_norm,
              odd_w_in, odd_w_out, na_rpb, ffn_w_up, ffn_conv_w, ffn_conv_b, ffn_w_down):
    xc = x_prompt
    xs = x_sample
    ang = axial_angles(x_sample.shape[1], HEAD_DIM)
    sf_list, sb_list, gk_list, gv_list, nk_list, nv_list = [], [], [], [], [], []
    for l in range(DEPTH):
        mc = ada_params(c_ctx[None, :], ada_w[l], ada_b[l])
        ms = ada_params(c, ada_w[l], ada_b[l])
        hc = rms_norm(xc, norm_mix[l]) * (1.0 + mc[1]) + mc[0]
        hs = rms_norm(xs, norm_mix[l]) * (1.0 + ms[1]) + ms[0]
        if l % 2 == 0:
            e = l // 2
            zeros = jnp.zeros((xc.shape[0], RET_HEADS, RET_DK, RET_DV), xc.dtype)
            oc, s_f, s_b, kc, vc = even_mixer(hc, even_w_in[e], even_w_out[e], ret_decay_fwd[e],
                                              ret_decay_bwd[e], ret_gn[e], gqa_q_norm[e], gqa_k_norm[e],
                                              zeros, zeros, None, None, None)
            os_, _, _, _, _ = even_mixer(hs, even_w_in[e], even_w_out[e], ret_decay_fwd[e],
                                         ret_decay_bwd[e], ret_gn[e], gqa_q_norm[e], gqa_k_norm[e],
                                         state_ret_fwd[:, e], state_ret_bwd[:, e],
                                         cache_gqa_k[:, e], cache_gqa_v[:, e], ang)
            sf_list.append(s_f)
            sb_list.append(s_b)
            gk_list.append(kc)
            gv_list.append(vc)
        else:
            o = l // 2
            oc, kc, vc = odd_mixer(hc, odd_w_in[o], odd_w_out[o], na_rpb[o], None, None)
            os_, _, _ = odd_mixer(hs, odd_w_in[o], odd_w_out[o], na_rpb[o], cache_na_k[:, o], cache_na_v[:, o])
            nk_list.append(kc)
            nv_list.append(vc)
        xc = xc + mc[2] * oc
        xs = xs + ms[2] * os_
        xc = xc + mc[5] * conv_ffn(rms_norm(xc, norm_ffn[l]) * (1.0 + mc[4]) + mc[3],
                                   ffn_w_up[l], ffn_conv_w[l], ffn_conv_b[l], ffn_w_down[l])
        xs = xs + ms[5] * conv_ffn(rms_norm(xs, norm_ffn[l]) * (1.0 + ms[4]) + ms[3],
                                   ffn_w_up[l], ffn_conv_w[l], ffn_conv_b[l], ffn_w_down[l])
    y_prompt = rms_norm(xc, norm_final)
    y_sample = rms_norm(xs, norm_final)
    new_state_ret_fwd = jnp.stack(sf_list, axis=1)
    new_state_ret_bwd = jnp.stack(sb_list, axis=1)
    new_cache_gqa_k = jnp.stack(gk_list, axis=1)
    new_cache_gqa_v = jnp.stack(gv_list, axis=1)
    new_cache_na_k = jnp.stack(nk_list, axis=1)
    new_cache_na_v = jnp.stack(nv_list, axis=1)
    return (y_prompt, y_sample, new_state_ret_fwd, new_state_ret_bwd, new_cache_gqa_k, new_cache_gqa_v, new_cache_na_k, new_cache_na_v)
```

```python
import functools

import numpy as np
import jax
import jax.numpy as jnp
from jax import lax
from jax.experimental import pallas as pl
from jax.experimental.pallas import tpu as pltpu

F32 = jnp.float32
BF16 = jnp.bfloat16

D_MODEL = 1024
GRID_W = 64
HEAD_DIM = 64
ROPE_BASE = 10000.0
EPS = 1e-6
GN_EPS = 1e-5
RET_HEADS = 8
RET_DK = 64
RET_DV = 128
RET_CHUNK = 128
GQA_HEADS = 8
GQA_KV_HEADS = 2
NA_HEADS = 16
NA_KH = 8
NA_KW = 16
D_FF = 2816
LANES = 128
MXU_DIM = 256
FF_CHUNK = MXU_DIM
N_FF_CHUNKS = D_FF // FF_CHUNK
NEG = -1e30
VMEM_LIMIT = 56 * 1024 * 1024


def _nn(a, b):
    return jnp.dot(a, b, preferred_element_type=F32)


def _nt(a, b):
    return lax.dot_general(a, b, (((1,), (1,)), ((), ())), preferred_element_type=F32)


def _tn(a, b):
    return lax.dot_general(a, b, (((0,), (0,)), ((), ())), preferred_element_type=F32)


def _sigmoid(x):
    return 1.0 / (1.0 + jnp.exp(-x))


def _params(n_axes):
    return pltpu.CompilerParams(dimension_semantics=("arbitrary",) * n_axes, vmem_limit_bytes=VMEM_LIMIT)


def _modulated_norm(x, g, scale, shift):
    ms = jnp.mean(x * x, axis=-1, keepdims=True)
    return (x * lax.rsqrt(ms + EPS) * g) * (1.0 + scale) + shift


def _lane_lo(rows):
    return lax.broadcasted_iota(jnp.int32, (rows, LANES), 1) < HEAD_DIM


def _head_rms_norm(xb, gain, lo):
    sq = xb * xb
    s_lo = jnp.sum(jnp.where(lo, sq, 0.0), axis=-1, keepdims=True)
    s_hi = jnp.sum(jnp.where(lo, 0.0, sq), axis=-1, keepdims=True)
    r = jnp.where(lo, lax.rsqrt(s_lo * (1.0 / HEAD_DIM) + EPS), lax.rsqrt(s_hi * (1.0 / HEAD_DIM) + EPS))
    return xb * r * gain


def _rope(xb, cos, sin_signed, first16):
    partner = jnp.where(first16, pltpu.roll(xb, LANES - 16, 1), pltpu.roll(xb, 16, 1))
    return xb * cos + partner * sin_signed


def _softmax_rows(scores):
    m = functools.reduce(jnp.maximum, [jnp.max(s, axis=-1, keepdims=True) for s in scores])
    es = [jnp.exp(s - m) for s in scores]
    l = functools.reduce(jnp.add, [jnp.sum(e, axis=-1, keepdims=True) for e in es])
    inv = 1.0 / l
    return [e * inv for e in es]


def _ada_kernel(c_ref, w_ref, b_ref, o_ref):
    c = c_ref[...]
    a = (c * _sigmoid(c)).astype(BF16)
    o_ref[...] = _nn(a, w_ref[...].astype(BF16)) + b_ref[...]


def _ada_params(cond, ada_w, ada_b):
    depth = ada_w.shape[0]
    rows = cond.shape[0]
    tn = 1024
    return pl.pallas_call(
        _ada_kernel,
        out_shape=jax.ShapeDtypeStruct((depth, rows, 6 * D_MODEL), F32),
        grid=(depth, 6 * D_MODEL // tn),
        in_specs=[
            pl.BlockSpec((rows, D_MODEL), lambda l, j: (0, 0)),
            pl.BlockSpec((None, D_MODEL, tn), lambda l, j: (l, 0, j)),
            pl.BlockSpec((None, 1, tn), lambda l, j: (l, 0, j)),
        ],
        out_specs=pl.BlockSpec((None, rows, tn), lambda l, j: (l, 0, j)),
        compiler_params=_params(2),
        name="ada_params",
    )(cond, ada_w, ada_b.reshape(depth, 1, 6 * D_MODEL))


def _mod_spec(layer, which, bidx):
    return pl.BlockSpec((None, None, None, 1, D_MODEL), lambda i: (layer, bidx(i), which, 0, 0))


def _batch_index_fn(latent, rows_per_tile, seq_len):
    if not latent:
        return lambda i: 0
    return lambda i: 1 + (i * rows_per_tile) // seq_len


def _in_even_kernel(*refs, latent, tm):
    if latent:
        (x_ref, shift_ref, scale_ref, g_ref, w_ref, qg_ref, kg_ref, cos_ref, sin_ref,
         qr_ref, kr_ref, vr_ref, gr_ref, qa_ref, kd_ref, vd_ref) = refs
    else:
        (x_ref, shift_ref, scale_ref, g_ref, w_ref, qg_ref, kg_ref,
         qr_ref, kr_ref, vr_ref, gr_ref, qa_ref, kd_ref, vd_ref, ck_ref, cv_ref) = refs
    hb = _modulated_norm(x_ref[...], g_ref[...], scale_ref[...], shift_ref[...]).astype(BF16)
    lane = lax.broadcasted_iota(jnp.int32, (tm, LANES), 1)
    lo = lane < HEAD_DIM
    if latent:
        cos = cos_ref[...]
        sin = sin_ref[...]
        first16 = (lane % 32) < 16
        rope = lambda v: _rope(v, cos, sin, first16)
    else:
        rope = lambda v: v
    dk_scale = RET_DK ** -0.5
    q_scale = HEAD_DIM ** -0.5

    r = _nn(hb, w_ref[:, 0:512])
    for b in range(4):
        qr_ref[:, b * LANES:(b + 1) * LANES] = rope(r[:, b * LANES:(b + 1) * LANES])
    r = _nn(hb, w_ref[:, 512:1024]) * dk_scale
    for b in range(4):
        kr_ref[:, b * LANES:(b + 1) * LANES] = rope(r[:, b * LANES:(b + 1) * LANES])
    for c in range(2):
        vr_ref[:, c * 512:(c + 1) * 512] = _nn(hb, w_ref[:, 1024 + c * 512:1536 + c * 512]).astype(BF16)
    for c in range(2):
        r = _nn(hb, w_ref[:, 2048 + c * 512:2560 + c * 512])
        gr_ref[:, c * 512:(c + 1) * 512] = r * _sigmoid(r)
    r = _nn(hb, w_ref[:, 3072:3584])
    qg = qg_ref[...]
    for b in range(4):
        blk = rope(_head_rms_norm(r[:, b * LANES:(b + 1) * LANES], qg, lo)) * q_scale
        qa_ref[:, b * LANES:(b + 1) * LANES] = blk.astype(BF16)
    r = _nn(hb, w_ref[:, 3584:3840])
    kn = _head_rms_norm(r[:, 0:LANES], kg_ref[...], lo)
    vn = r[:, LANES:2 * LANES]
    if not latent:
        for bb in range(tm // ck_ref.shape[3]):
            rows = slice(bb * ck_ref.shape[3], (bb + 1) * ck_ref.shape[3])
            for kv in range(GQA_KV_HEADS):
                ck_ref[bb, 0, kv] = kn[rows, kv * HEAD_DIM:(kv + 1) * HEAD_DIM]
                cv_ref[bb, 0, kv] = vn[rows, kv * HEAD_DIM:(kv + 1) * HEAD_DIM]
    kn = rope(kn)
    for src, dst in ((kn, kd_ref), (vn, vd_ref)):
        sw = pltpu.roll(src, HEAD_DIM, 1)
        dst[:, 0:LANES] = jnp.where(lo, src, sw).astype(BF16)
        dst[:, LANES:2 * LANES] = jnp.where(lo, sw, src).astype(BF16)


def _in_proj_even(x2d, mods, layer, norm_g, w_bf, q_gain2, k_gain2, rope_tabs, *, n_batch, seq_len, latent):
    n = x2d.shape[0]
    tm = 512
    bidx = _batch_index_fn(latent, tm, seq_len)
    row = lambda i: (i, 0)
    const = lambda i: (0, 0)
    in_specs = [
        pl.BlockSpec((tm, D_MODEL), row),
        _mod_spec(layer, 0, bidx),
        _mod_spec(layer, 1, bidx),
        pl.BlockSpec((1, D_MODEL), const),
        pl.BlockSpec(w_bf.shape, const),
        pl.BlockSpec((1, LANES), const),
        pl.BlockSpec((1, LANES), const),
    ]
    args = [x2d, mods, mods, norm_g, w_bf, q_gain2, k_gain2]
    if latent:
        tiles_per_seq = seq_len // tm
        in_specs += [pl.BlockSpec((tm, LANES), lambda i: (i % tiles_per_seq, 0))] * 2
        args += list(rope_tabs)
    out_shape = [
        jax.ShapeDtypeStruct((n, 512), F32),
        jax.ShapeDtypeStruct((n, 512), F32),
        jax.ShapeDtypeStruct((n, 1024), BF16),
        jax.ShapeDtypeStruct((n, 1024), F32),
        jax.ShapeDtypeStruct((n, 512), BF16),
        jax.ShapeDtypeStruct((n, 256), BF16),
        jax.ShapeDtypeStruct((n, 256), BF16),
    ]
    out_specs = [pl.BlockSpec((tm, s.shape[1]), row) for s in out_shape]
    if not latent:
        cache = jax.ShapeDtypeStruct((n_batch, 1, GQA_KV_HEADS, seq_len, HEAD_DIM), F32)
        out_shape += [cache, cache]
        out_specs += [pl.BlockSpec((tm // seq_len, 1, GQA_KV_HEADS, seq_len, HEAD_DIM),
                                   lambda i: (i, 0, 0, 0, 0))] * 2
    return pl.pallas_call(
        functools.partial(_in_even_kernel, latent=latent, tm=tm),
        out_shape=out_shape,
        grid=(n // tm,),
        in_specs=in_specs,
        out_specs=out_specs,
        compiler_params=_params(1),
        name="in_proj_even_latent" if latent else "in_proj_even_ctx",
    )(*args)


def _retention_kernel(*refs, n, has_state, write_state):
    lg_ref, q_ref, k_ref, v_ref, gr_ref, gn_ref = refs[:6]
    refs = refs[6:]
    if has_state:
        s0f_ref, s0b_ref = refs[:2]
        refs = refs[2:]
    o_ref = refs[0]
    if write_state:
        sf_ref, sb_ref = refs[1:3]
    c = RET_CHUNK
    nc = n // c
    p = pl.program_id(1)
    row = lax.broadcasted_iota(jnp.int32, (c, c), 0)
    col = lax.broadcasted_iota(jnp.int32, (c, c), 1)
    diff = (row - col).astype(F32)
    pos = row.astype(F32)
    lane_lo = col < RET_DK
    zeros_half = jnp.zeros((RET_DK, RET_DV), F32)
    for hh in range(2):
        lgf = lg_ref[0, 2 * p + hh]
        lgb = lg_ref[1, 2 * p + hh]
        decay = (jnp.where(diff >= 0, jnp.exp(lgf * jnp.maximum(diff, 0.0)), 0.0)
                 + jnp.where(diff <= 0, jnp.exp(lgb * jnp.maximum(-diff, 0.0)), 0.0))
        qd_f = jnp.exp(lgf * (pos + 1.0))
        kd_f = jnp.exp(lgf * (c - 1.0 - pos))
        cd_f = jnp.exp(lgf * jnp.full((c, RET_DV), float(c), F32))
        qd_b = jnp.exp(lgb * (c - pos))
        kd_b = jnp.exp(lgb * pos)
        cd_b = jnp.exp(lgb * jnp.full((c, RET_DV), float(c), F32))
        hmask = lane_lo if hh == 0 else jnp.logical_not(lane_lo)
        vcols = slice(hh * RET_DV, (hh + 1) * RET_DV)

        def k_chunk(i):
            return jnp.where(hmask, k_ref[i * c:(i + 1) * c, :], 0.0)

        def v_chunk(i):
            return v_ref[i * c:(i + 1) * c, vcols]

        def place(s0):
            return jnp.concatenate([s0, zeros_half] if hh == 0 else [zeros_half, s0], axis=0)

        if has_state:
            s_f = place(s0f_ref[0, 0, hh])
            s_b = place(s0b_ref[0, 0, hh])
        else:
            s_f = jnp.zeros((c, RET_DV), F32)
            s_b = jnp.zeros((c, RET_DV), F32)
        before_f = []
        for i in range(nc):
            before_f.append(s_f)
            s_f = s_f * cd_f + _tn((k_chunk(i) * kd_f).astype(BF16), v_chunk(i))
        before_b = [None] * nc
        for i in reversed(range(nc)):
            before_b[i] = s_b
            s_b = s_b * cd_b + _tn((k_chunk(i) * kd_b).astype(BF16), v_chunk(i))
        if write_state:
            sf_ref[0, 0, hh] = s_f[hh * RET_DK:(hh + 1) * RET_DK]
            sb_ref[0, 0, hh] = s_b[hh * RET_DK:(hh + 1) * RET_DK]
        gn = gn_ref[:, vcols]
        for i in range(nc):
            qc = q_ref[i * c:(i + 1) * c, :]
            scores = _nt(qc.astype(BF16), k_chunk(i).astype(BF16)) * decay
            o = (_nn(scores.astype(BF16), v_chunk(i))
                 + _nn((qc * qd_f).astype(BF16), before_f[i].astype(BF16))
                 + _nn((qc * qd_b).astype(BF16), before_b[i].astype(BF16)))
            mu = jnp.mean(o, axis=-1, keepdims=True)
            d = o - mu
            var = jnp.mean(d * d, axis=-1, keepdims=True)
            y = d * lax.rsqrt(var + GN_EPS) * gn * gr_ref[i * c:(i + 1) * c, vcols]
            o_ref[i * c:(i + 1) * c, vcols] = y.astype(BF16)


def _retention(log_g, qr, kr, vr, gr, gn, state_f, state_b, *, n_batch, seq_len, write_state):
    n = qr.shape[0]
    pairs = RET_HEADS // 2
    has_state = state_f is not None
    tok = lambda b, p: (b, p)
    in_specs = [
        pl.BlockSpec(memory_space=pltpu.SMEM),
        pl.BlockSpec((seq_len, LANES), tok),
        pl.BlockSpec((seq_len, LANES), tok),
        pl.BlockSpec((seq_len, 2 * RET_DV), tok),
        pl.BlockSpec((seq_len, 2 * RET_DV), tok),
        pl.BlockSpec((1, 2 * RET_DV), lambda b, p: (0, p)),
    ]
    args = [log_g, qr, kr, vr, gr, gn]
    state_spec = pl.BlockSpec((1, 1, 2, RET_DK, RET_DV), lambda b, p: (b, 0, p, 0, 0))
    if has_state:
        in_specs += [state_spec, state_spec]
        args += [state_f, state_b]
    out_shape = [jax.ShapeDtypeStruct((n, RET_HEADS * RET_DV), BF16)]
    out_specs = [pl.BlockSpec((seq_len, 2 * RET_DV), tok)]
    if write_state:
        st = jax.ShapeDtypeStruct((n_batch, 1, RET_HEADS, RET_DK, RET_DV), F32)
        out_shape += [st, st]
        out_specs += [state_spec, state_spec]
    return pl.pallas_call(
        functools.partial(_retention_kernel, n=seq_len, has_state=has_state, write_state=write_state),
        out_shape=out_shape,
        grid=(n_batch, pairs),
        in_specs=in_specs,
        out_specs=out_specs,
        compiler_params=_params(2),
        name="retention_latent" if has_state else "retention_ctx",
    )(*args)


def _gqa_kernel(*refs, n_src, tq):
    q_ref = refs[0]
    k_refs = refs[1:1 + 2 * n_src:2]
    v_refs = refs[2:2 + 2 * n_src:2]
    o_ref = refs[1 + 2 * n_src]
    for g in range(GQA_KV_HEADS):
        base = g * 2 * LANES
        q = jnp.concatenate([q_ref[:, base:base + LANES], q_ref[:, base + LANES:base + 2 * LANES]], axis=0)
        kcols = slice(g * LANES, (g + 1) * LANES)
        outs = []
        for half in range(2):
            scores = []
            for k_ref in k_refs:
                kd = k_ref[:, kcols]
                lo = _lane_lo(kd.shape[0])
                keep = lo if half == 0 else jnp.logical_not(lo)
                scores.append(_nt(q, jnp.where(keep, kd, jnp.zeros_like(kd))))
            probs = _softmax_rows(scores)
            o = None
            for pr, v_ref in zip(probs, v_refs):
                t = _nn(pr.astype(BF16), v_ref[:, kcols])
                o = t if o is None else o + t
            outs.append(o)
        o = jnp.where(_lane_lo(2 * tq), outs[0], outs[1]).astype(BF16)
        o_ref[:, base:base + LANES] = o[:tq]
        o_ref[:, base + LANES:base + 2 * LANES] = o[tq:]


def _gqa_attention(qa, kd, vd, ctx_kd, ctx_vd, *, n_batch, seq_len, tq):
    n = qa.shape[0]
    tiles = seq_len // tq
    n_src = 1 if ctx_kd is None else 2
    qmap = lambda b, t: (b * tiles + t, 0)
    kmap = lambda b, t: (b, 0)
    in_specs = [pl.BlockSpec((tq, GQA_HEADS * HEAD_DIM), qmap),
                pl.BlockSpec((seq_len, 2 * LANES), kmap),
                pl.BlockSpec((seq_len, 2 * LANES), kmap)]
    args = [qa, kd, vd]
    if n_src == 2:
        past = ctx_kd.shape[1]
        cmap = lambda b, t: (b, 0, 0)
        in_specs += [pl.BlockSpec((None, past, 2 * LANES), cmap)] * 2
        args += [ctx_kd, ctx_vd]
    return pl.pallas_call(
        functools.partial(_gqa_kernel, n_src=n_src, tq=tq),
        out_shape=jax.ShapeDtypeStruct((n, GQA_HEADS * HEAD_DIM), BF16),
        grid=(n_batch, tiles),
        in_specs=in_specs,
        out_specs=pl.BlockSpec((tq, GQA_HEADS * HEAD_DIM), qmap),
        compiler_params=_params(2),
        name="gqa_latent" if n_src == 2 else "gqa_ctx",
    )(*args)


def _out_proj_kernel(*refs, n_mix):
    x_ref, gate_ref = refs[:2]
    m_refs = refs[2:2 + n_mix]
    w_refs = refs[2 + n_mix:2 + 2 * n_mix]
    o_ref = refs[2 + 2 * n_mix]
    acc = None
    for m_ref, w_ref in zip(m_refs, w_refs):
        t = _nn(m_ref[...], w_ref[...])
        acc = t if acc is None else acc + t
    o_ref[...] = x_ref[...] + gate_ref[...] * acc


def _out_proj(x2d, mods, layer, mixes, weights, *, seq_len, latent):
    n = x2d.shape[0]
    tm = 512
    bidx = _batch_index_fn(latent, tm, seq_len)
    row = lambda i: (i, 0)
    const = lambda i: (0, 0)
    in_specs = [pl.BlockSpec((tm, D_MODEL), row), _mod_spec(layer, 2, bidx)]
    in_specs += [pl.BlockSpec((tm, m.shape[1]), row) for m in mixes]
    in_specs += [pl.BlockSpec(w.shape, const) for w in weights]
    return pl.pallas_call(
        functools.partial(_out_proj_kernel, n_mix=len(mixes)),
        out_shape=jax.ShapeDtypeStruct((n, D_MODEL), F32),
        grid=(n // tm,),
        in_specs=in_specs,
        out_specs=pl.BlockSpec((tm, D_MODEL), row),
        compiler_params=_params(1),
        name="out_proj_latent" if latent else "out_proj_ctx",
    )(x2d, mods, *mixes, *weights)


def _ffn_kernel(*refs, tm, seq_len, final):
    x_ref, shift_ref, scale_ref, gate_ref, g_ref, wup_ref, cw_ref, cb_ref, wd_ref = refs[:9]
    refs = refs[9:]
    if final:
        gfin_ref = refs[0]
        refs = refs[1:]
    o_ref, hb_ref, acc_ref = refs
    hb_ref[...] = _modulated_norm(x_ref[...], g_ref[...], scale_ref[...], shift_ref[...]).astype(BF16)
    acc_ref[...] = jnp.zeros_like(acc_ref)
    pos = lax.broadcasted_iota(jnp.int32, (tm, 2 * FF_CHUNK), 0) % seq_len
    first = pos == 0
    last = pos == seq_len - 1

    def chunk(j, carry):
        u = _nn(hb_ref[...], wup_ref[j])
        prev = jnp.where(first, 0.0, pltpu.roll(u, 1, 0))
        nxt = jnp.where(last, 0.0, pltpu.roll(u, tm - 1, 0))
        cw = cw_ref[j]
        cv = prev * cw[0:1] + u * cw[1:2] + nxt * cw[2:3] + cb_ref[j]
        a = cv[:, :FF_CHUNK]
        act = (a * _sigmoid(a) * cv[:, FF_CHUNK:]).astype(BF16)
        acc_ref[...] += _nn(act, wd_ref[j])
        return carry

    lax.fori_loop(0, N_FF_CHUNKS, chunk, 0)
    y = x_ref[...] + gate_ref[...] * acc_ref[...]
    if final:
        ms = jnp.mean(y * y, axis=-1, keepdims=True)
        y = y * lax.rsqrt(ms + EPS) * gfin_ref[...]
    o_ref[...] = y


def _conv_ffn(x2d, mods, layer, norm_g, wup_c, cw_c, cb_c, wd_c, final_g, *, seq_len, latent):
    n = x2d.shape[0]
    tm = 1024
    bidx = _batch_index_fn(latent, tm, seq_len)
    row = lambda i: (i, 0)
    const2 = lambda i: (0, 0)
    const3 = lambda i: (0, 0, 0)
    resident = lambda a: pl.BlockSpec(a.shape, const3, pipeline_mode=pl.Buffered(1))
    in_specs = [pl.BlockSpec((tm, D_MODEL), row), _mod_spec(layer, 3, bidx), _mod_spec(layer, 4, bidx),
                _mod_spec(layer, 5, bidx), pl.BlockSpec((1, D_MODEL), const2),
                resident(wup_c), resident(cw_c), resident(cb_c), resident(wd_c)]
    args = [x2d, mods, mods, mods, norm_g, wup_c, cw_c, cb_c, wd_c]
    final = final_g is not None
    if final:
        in_specs.append(pl.BlockSpec((1, D_MODEL), const2))
        args.append(final_g)
    return pl.pallas_call(
        functools.partial(_ffn_kernel, tm=tm, seq_len=seq_len, final=final),
        out_shape=jax.ShapeDtypeStruct((n, D_MODEL), F32),
        grid=(n // tm,),
        in_specs=in_specs,
        out_specs=pl.BlockSpec((tm, D_MODEL), row),
        scratch_shapes=[pltpu.VMEM((tm, D_MODEL), BF16), pltpu.VMEM((tm, D_MODEL), F32)],
        compiler_params=_params(1),
        name="conv_ffn_latent" if latent else "conv_ffn_ctx",
    )(*args)


def _in_odd_kernel(*refs, write_cache, tm):
    x_ref, shift_ref, scale_ref, g_ref, w_ref, q_ref, k_ref, v_ref = refs[:8]
    hb = _modulated_norm(x_ref[...], g_ref[...], scale_ref[...], shift_ref[...]).astype(BF16)
    width = NA_HEADS * HEAD_DIM
    q_scale = HEAD_DIM ** -0.5
    for c in range(2):
        cols = slice(c * 512, (c + 1) * 512)
        q_ref[:, cols] = (_nn(hb, w_ref[:, c * 512:(c + 1) * 512]) * q_scale).astype(BF16)
    for which, dst in ((1, k_ref), (2, v_ref)):
        for c in range(2):
            r = _nn(hb, w_ref[:, which * width + c * 512:which * width + (c + 1) * 512])
            dst[:, c * 512:(c + 1) * 512] = r.astype(BF16)
            if write_cache:
                cache_ref = refs[8 + which - 1]
                seq = cache_ref.shape[3]
                for bb in range(tm // seq):
                    for hd in range(512 // HEAD_DIM):
                        cache_ref[bb, 0, c * (512 // HEAD_DIM) + hd] = (
                            r[bb * seq:(bb + 1) * seq, hd * HEAD_DIM:(hd + 1) * HEAD_DIM])


def _in_proj_odd(x2d, mods, layer, norm_g, w_bf, *, n_batch, seq_len, latent):
    n = x2d.shape[0]
    tm = 512
    width = NA_HEADS * HEAD_DIM
    bidx = _batch_index_fn(latent, tm, seq_len)
    row = lambda i: (i, 0)
    const = lambda i: (0, 0)
    in_specs = [pl.BlockSpec((tm, D_MODEL), row), _mod_spec(layer, 0, bidx), _mod_spec(layer, 1, bidx),
                pl.BlockSpec((1, D_MODEL), const), pl.BlockSpec(w_bf.shape, const)]
    out_shape = [jax.ShapeDtypeStruct((n, width), BF16)] * 3
    out_specs = [pl.BlockSpec((tm, width), row)] * 3
    write_cache = not latent
    if write_cache:
        cache = jax.ShapeDtypeStruct((n_batch, 1, NA_HEADS, seq_len, HEAD_DIM), F32)
        out_shape += [cache, cache]
        out_specs += [pl.BlockSpec((tm // seq_len, 1, NA_HEADS, seq_len, HEAD_DIM),
                                   lambda i: (i, 0, 0, 0, 0))] * 2
    return pl.pallas_call(
        functools.partial(_in_odd_kernel, write_cache=write_cache, tm=tm),
        out_shape=out_shape,
        grid=(n // tm,),
        in_specs=in_specs,
        out_specs=out_specs,
        compiler_params=_params(1),
        name="in_proj_odd_latent" if latent else "in_proj_odd_ctx",
    )(x2d, mods, mods, norm_g, w_bf)


def _pair_attention(q, key_pieces, val_pieces, bias_pieces):
    outs = []
    for half in range(2):
        scores = []
        for kp, bp in zip(key_pieces, bias_pieces):
            lo = _lane_lo(kp.shape[0])
            keep = lo if half == 0 else jnp.logical_not(lo)
            s = _nt(q, jnp.where(keep, kp, jnp.zeros_like(kp)))
            scores.append(s if bp is None else s + bp[half])
        probs = _softmax_rows(scores)
        o = None
        for pr, vp in zip(probs, val_pieces):
            t = _nn(pr.astype(BF16), vp)
            o = t if o is None else o + t
        outs.append(o)
    return jnp.where(_lane_lo(q.shape[0]), outs[0], outs[1])


def _dense_pairs_kernel(q_ref, k_ref, v_ref, o_ref):
    for p in range(NA_HEADS // 2):
        cols = slice(p * LANES, (p + 1) * LANES)
        o = _pair_attention(q_ref[:, cols], [k_ref[:, cols]], [v_ref[:, cols]], [None])
        o_ref[:, cols] = o.astype(BF16)


def _dense_attention_ctx(q, k, v, *, n_batch, seq_len):
    width = NA_HEADS * HEAD_DIM
    spec = pl.BlockSpec((seq_len, width), lambda b: (b, 0))
    return pl.pallas_call(
        _dense_pairs_kernel,
        out_shape=jax.ShapeDtypeStruct(q.shape, BF16),
        grid=(n_batch,),
        in_specs=[spec, spec, spec],
        out_specs=spec,
        compiler_params=_params(1),
        name="dense_attention_ctx",
    )(q, k, v)


NA_Q_ROWS = 4
NA_KEY_ROWS = 12


def _na_window_start(r):
    return min(max(r - NA_KH // 2, 0), (1024 // GRID_W) - NA_KH)


def _na_bias_tile(bias_ref, half, tile, n_rows):
    r0 = tile * NA_Q_ROWS
    kr0 = min(max(r0 - NA_KH // 2, 0), n_rows - NA_KEY_ROWS)
    lo = _lane_lo(GRID_W)
    neg_block = jnp.full((GRID_W, LANES), NEG, F32)
    left_off = jnp.where(lo, NEG, 0.0)
    right_off = jnp.where(lo, 0.0, NEG)
    rows = []
    for rq in range(NA_Q_ROWS):
        r = r0 + rq
        rs = _na_window_start(r)
        blocks = []
        for kk in range(NA_KEY_ROWS // 2):
            ka = kr0 + 2 * kk
            va = rs <= ka < rs + NA_KH
            vb = rs <= ka + 1 < rs + NA_KH
            if not (va or vb):
                blocks.append(neg_block)
                continue
            blk = bias_ref[half, ka - r + NA_KH]
            if not va:
                blk = blk + left_off
            if not vb:
                blk = blk + right_off
            blocks.append(blk)
        rows.append(jnp.concatenate(blocks, axis=1))
    return jnp.concatenate(rows, axis=0), kr0


def _na_kernel(q_ref, k_ref, v_ref, ck_ref, cv_ref, bias_ref, o_ref, *, n_rows):
    tq = NA_Q_ROWS * GRID_W
    span = NA_KEY_ROWS * GRID_W
    ck = ck_ref[...]
    cv = cv_ref[...]
    for tile in range(n_rows // NA_Q_ROWS):
        biases = []
        for half in range(2):
            b, kr0 = _na_bias_tile(bias_ref, half, tile, n_rows)
            biases.append(b)
        keys = slice(kr0 * GRID_W, kr0 * GRID_W + span)
        o = _pair_attention(q_ref[tile * tq:(tile + 1) * tq, :], [k_ref[keys, :], ck], [v_ref[keys, :], cv],
                            [biases, None])
        o_ref[tile * tq:(tile + 1) * tq, :] = o.astype(BF16)


def _na_bias_table(rpb):
    cidx = np.arange(GRID_W)
    cs = np.clip(cidx - NA_KW // 2, 0, GRID_W - NA_KW)
    kc = np.arange(GRID_W)
    inside = (kc[None, :] >= cs[:, None]) & (kc[None, :] < cs[:, None] + NA_KW)
    rel = np.clip(kc[None, :] - cidx[:, None] + NA_KW - 1, 0, 2 * NA_KW - 2)
    m = jnp.where(jnp.asarray(inside)[None, None], rpb[:, :, jnp.asarray(rel)], NEG)
    neg = jnp.full((rpb.shape[0], 1, GRID_W, GRID_W), NEG, F32)
    left = jnp.concatenate([neg, m], axis=1)
    right = jnp.concatenate([m, neg], axis=1)
    return jnp.concatenate([left, right], axis=-1)


def _na_attention(q, k, v, ctx_k, ctx_v, bias_tab, *, n_batch, seq_len):
    pairs = NA_HEADS // 2
    tok = lambda p, b: (b, p)
    ctx = lambda p, b: (b, 0, p)
    past = ctx_k.shape[1]
    return pl.pallas_call(
        functools.partial(_na_kernel, n_rows=seq_len // GRID_W),
        out_shape=jax.ShapeDtypeStruct(q.shape, BF16),
        grid=(pairs, n_batch),
        in_specs=[pl.BlockSpec((seq_len, LANES), tok)] * 3
        + [pl.BlockSpec((None, past, LANES), ctx)] * 2
        + [pl.BlockSpec((2, 2 * NA_KH, GRID_W, LANES), lambda p, b: (p, 0, 0, 0))],
        out_specs=pl.BlockSpec((seq_len, LANES), tok),
        compiler_params=_params(2),
        name="neighbourhood_attention",
    )(q, k, v, ctx_k, ctx_v, bias_tab)


def _rope_tables(n):
    t = jnp.arange(n)
    row = (t // GRID_W).astype(F32)
    col = (t % GRID_W).astype(F32)
    half = HEAD_DIM // 2
    inv = ROPE_BASE ** (-jnp.arange(0, half, 2, dtype=F32) / half)
    ang_r = row[:, None] * inv
    ang_c = col[:, None] * inv
    cos_h = jnp.concatenate([jnp.cos(ang_r)] * 2 + [jnp.cos(ang_c)] * 2, axis=-1)
    sin_h = jnp.concatenate([-jnp.sin(ang_r), jnp.sin(ang_r), -jnp.sin(ang_c), jnp.sin(ang_c)], axis=-1)
    return jnp.concatenate([cos_h, cos_h], axis=-1), jnp.concatenate([sin_h, sin_h], axis=-1)


def _ffn_weights(w_up, conv_w, conv_b, w_down):
    def regroup(a):
        lead = a.shape[:-1]
        a = a.reshape(*lead, 2, N_FF_CHUNKS, FF_CHUNK)
        a = jnp.moveaxis(a, -2, 0)
        return a.reshape(N_FF_CHUNKS, *lead, 2 * FF_CHUNK)
    return (regroup(w_up).astype(BF16), regroup(conv_w), regroup(conv_b[None, :]),
            w_down.reshape(N_FF_CHUNKS, FF_CHUNK, D_MODEL).astype(BF16))


def _token_major_dup(cache):
    b, kv, t, d = cache.shape
    c = jnp.transpose(cache, (0, 2, 1, 3))[:, :, :, None, :]
    return jnp.broadcast_to(c, (b, t, kv, 2, d)).reshape(b, t, kv * 2 * d).astype(BF16)


def _token_major(cache):
    b, h, t, d = cache.shape
    return jnp.transpose(cache, (0, 2, 1, 3)).reshape(b, t, h * d).astype(BF16)


def kernel(x_prompt, x_sample, state_ret_fwd, state_ret_bwd, cache_gqa_k, cache_gqa_v, cache_na_k, cache_na_v,
           c, c_ctx, ada_w, ada_b, norm_mix, norm_ffn, norm_final, even_w_in, even_w_out, ret_decay_fwd,
           ret_decay_bwd, ret_gn, gqa_q_norm, gqa_k_norm, odd_w_in, odd_w_out, na_rpb, ffn_w_up, ffn_conv_w,
           ffn_conv_b, ffn_w_down):
    nb_c, len_c, _ = x_prompt.shape
    nb_s, len_s, _ = x_sample.shape
    depth = ada_w.shape[0]
    streams = {
        False: dict(n_batch=nb_c, seq_len=len_c),
        True: dict(n_batch=nb_s, seq_len=len_s),
    }
    xs = {False: x_prompt.reshape(nb_c * len_c, D_MODEL), True: x_sample.reshape(nb_s * len_s, D_MODEL)}

    rows = 8 * (-(-(1 + nb_s) // 8))
    cond = jnp.zeros((rows, D_MODEL), F32).at[0].set(c_ctx).at[1:1 + nb_s].set(c)
    mods = _ada_params(cond, ada_w, ada_b).reshape(depth, rows, 6, 1, D_MODEL)
    rope_tabs = _rope_tables(len_s)
    outs = {}

    for l in range(depth):
        g_mix = norm_mix[l][None, :]
        g_ffn = norm_ffn[l][None, :]
        if l % 2 == 0:
            e = l // 2
            w_in = even_w_in[e].astype(BF16)
            w_out = even_w_out[e].astype(BF16)
            w_out_parts = [w_out[:RET_HEADS * RET_DV], w_out[RET_HEADS * RET_DV:]]
            log_g = jnp.stack([jax.nn.log_sigmoid(ret_decay_fwd[e].astype(F32)),
                               jax.nn.log_sigmoid(ret_decay_bwd[e].astype(F32))])
            gn = ret_gn[e][None, :]
            qg2 = jnp.tile(gqa_q_norm[e], 2)[None, :]
            kg2 = jnp.tile(gqa_k_norm[e], 2)[None, :]
            for latent in (False, True):
                st = streams[latent]
                res = _in_proj_even(xs[latent], mods, l, g_mix, w_in, qg2, kg2, rope_tabs, latent=latent, **st)
                qr, kr, vr, gr, qa, kd, vd = res[:7]
                if latent:
                    ret = _retention(log_g, qr, kr, vr, gr, gn, state_ret_fwd[:, e:e + 1],
                                     state_ret_bwd[:, e:e + 1], write_state=False, **st)[0]
                    att = _gqa_attention(qa, kd, vd, _token_major_dup(cache_gqa_k[:, e]),
                                         _token_major_dup(cache_gqa_v[:, e]), tq=256, **st)
                else:
                    outs.setdefault("gk", []).append(res[7])
                    outs.setdefault("gv", []).append(res[8])
                    ret, s_f, s_b = _retention(log_g, qr, kr, vr, gr, gn, None, None, write_state=True, **st)
                    outs.setdefault("sf", []).append(s_f)
                    outs.setdefault("sb", []).append(s_b)
                    att = _gqa_attention(qa, kd, vd, None, None, tq=st["seq_len"], **st)
                xs[latent] = _out_proj(xs[latent], mods, l, [ret, att], w_out_parts,
                                       seq_len=st["seq_len"], latent=latent)
        else:
            o = l // 2
            w_in = odd_w_in[o].astype(BF16)
            w_out = odd_w_out[o].astype(BF16)
            for latent in (False, True):
                st = streams[latent]
                res = _in_proj_odd(xs[latent], mods, l, g_mix, w_in, latent=latent, **st)
                q, k, v = res[:3]
                if latent:
                    att = _na_attention(q, k, v, _token_major(cache_na_k[:, o]), _token_major(cache_na_v[:, o]),
                                        _na_bias_table(na_rpb[o]), **st)
                else:
                    outs.setdefault("nk", []).append(res[3])
                    outs.setdefault("nv", []).append(res[4])
                    att = _dense_attention_ctx(q, k, v, **st)
                xs[latent] = _out_proj(xs[latent], mods, l, [att], [w_out], seq_len=st["seq_len"], latent=latent)
        ffn_w = _ffn_weights(ffn_w_up[l], ffn_conv_w[l], ffn_conv_b[l], ffn_w_down[l])
        final_g = norm_final[None, :] if l == depth - 1 else None
        for latent in (False, True):
            xs[latent] = _conv_ffn(xs[latent], mods, l, g_ffn, *ffn_w, final_g,
                                   seq_len=streams[latent]["seq_len"], latent=latent)

    cat = lambda name: outs[name][0] if len(outs[name]) == 1 else jnp.concatenate(outs[name], axis=1)
    return (xs[False].reshape(nb_c, len_c, D_MODEL), xs[True].reshape(nb_s, len_s, D_MODEL),
            cat("sf"), cat("sb"), cat("gk"), cat("gv"), cat("nk"), cat("nv"))
```

```python
import functools

import numpy as np
import jax
import jax.numpy as jnp
from jax import lax
from jax.experimental import pallas as pl
from jax.experimental.pallas import tpu as pltpu

F32 = jnp.float32
BF16 = jnp.bfloat16

D_MODEL = 1024
GRID_W = 64
HEAD_DIM = 64
ROPE_BASE = 10000.0
EPS = 1e-6
GN_EPS = 1e-5
RET_HEADS = 8
RET_DK = 64
RET_DV = 128
RET_CHUNK = 128
GQA_HEADS = 8
GQA_KV_HEADS = 2
NA_HEADS = 16
NA_KH = 8
NA_KW = 16
D_FF = 2816
LANES = 128
MXU_DIM = 256
FF_CHUNK = MXU_DIM
N_FF_CHUNKS = D_FF // FF_CHUNK
NEG = -1e30
LOG2E = 1.4426950408889634
VMEM_LIMIT = 56 * 1024 * 1024


def _nn(a, b):
    return jnp.dot(a, b, preferred_element_type=F32)


def _nt(a, b):
    return lax.dot_general(a, b, (((1,), (1,)), ((), ())), preferred_element_type=F32)


def _tn(a, b):
    return lax.dot_general(a, b, (((0,), (0,)), ((), ())), preferred_element_type=F32)


def _sigmoid(x):
    return 1.0 / (1.0 + jnp.exp(-x))


def _params(n_axes):
    return pltpu.CompilerParams(dimension_semantics=("arbitrary",) * n_axes, vmem_limit_bytes=VMEM_LIMIT)


def _modulated_norm(x, g, scale, shift):
    ms = jnp.mean(x * x, axis=-1, keepdims=True)
    return (x * lax.rsqrt(ms + EPS) * g) * (1.0 + scale) + shift


def _lane_lo(rows):
    return lax.broadcasted_iota(jnp.int32, (rows, LANES), 1) < HEAD_DIM


def _head_rms_norm(xb, gain, lo):
    sq = xb * xb
    s_lo = jnp.sum(jnp.where(lo, sq, 0.0), axis=-1, keepdims=True)
    s_hi = jnp.sum(jnp.where(lo, 0.0, sq), axis=-1, keepdims=True)
    r = jnp.where(lo, lax.rsqrt(s_lo * (1.0 / HEAD_DIM) + EPS), lax.rsqrt(s_hi * (1.0 / HEAD_DIM) + EPS))
    return xb * r * gain


def _rope(xb, cos, sin_signed, first16):
    partner = jnp.where(first16, pltpu.roll(xb, LANES - 16, 1), pltpu.roll(xb, 16, 1))
    return xb * cos + partner * sin_signed


def _softmax_apply(scores, values):
    m = functools.reduce(jnp.maximum, [jnp.max(s, axis=-1, keepdims=True) for s in scores])
    es = [jnp.exp2(s - m) for s in scores]
    l = functools.reduce(jnp.add, [jnp.sum(e, axis=-1, keepdims=True) for e in es])
    o = None
    for e, v in zip(es, values):
        t = _nn(e.astype(BF16), v)
        o = t if o is None else o + t
    return o * (1.0 / l)


def _ada_kernel(c_ref, w_ref, b_ref, o_ref):
    c = c_ref[...]
    a = (c * _sigmoid(c)).astype(BF16)
    o_ref[...] = _nn(a, w_ref[...].astype(BF16)) + b_ref[...]


def _ada_params(cond, ada_w, ada_b):
    depth = ada_w.shape[0]
    rows = cond.shape[0]
    tn = 1024
    return pl.pallas_call(
        _ada_kernel,
        out_shape=jax.ShapeDtypeStruct((depth, rows, 6 * D_MODEL), F32),
        grid=(depth, 6 * D_MODEL // tn),
        in_specs=[
            pl.BlockSpec((rows, D_MODEL), lambda l, j: (0, 0)),
            pl.BlockSpec((None, D_MODEL, tn), lambda l, j: (l, 0, j)),
            pl.BlockSpec((None, 1, tn), lambda l, j: (l, 0, j)),
        ],
        out_specs=pl.BlockSpec((None, rows, tn), lambda l, j: (l, 0, j)),
        compiler_params=_params(2),
        name="ada_params",
    )(cond, ada_w, ada_b.reshape(depth, 1, 6 * D_MODEL))


def _mod_spec(layer, which, bidx):
    return pl.BlockSpec((None, None, None, 1, D_MODEL), lambda i: (layer, bidx(i), which, 0, 0))


def _batch_index_fn(latent, rows_per_tile, seq_len):
    if not latent:
        return lambda i: 0
    return lambda i: 1 + (i * rows_per_tile) // seq_len


def _in_even_kernel(*refs, latent, tm):
    if latent:
        (x_ref, shift_ref, scale_ref, g_ref, w_ref, qg_ref, kg_ref, cos_ref, sin_ref,
         qr_ref, kr_ref, vr_ref, gr_ref, qa_ref, kd_ref, vd_ref) = refs
    else:
        (x_ref, shift_ref, scale_ref, g_ref, w_ref, qg_ref, kg_ref,
         qr_ref, kr_ref, vr_ref, gr_ref, qa_ref, kd_ref, vd_ref, ck_ref, cv_ref) = refs
    hb = _modulated_norm(x_ref[...], g_ref[...], scale_ref[...], shift_ref[...]).astype(BF16)
    lane = lax.broadcasted_iota(jnp.int32, (tm, LANES), 1)
    lo = lane < HEAD_DIM
    if latent:
        cos = cos_ref[...]
        sin = sin_ref[...]
        first16 = (lane % 32) < 16
        rope = lambda v: _rope(v, cos, sin, first16)
    else:
        rope = lambda v: v
    dk_scale = RET_DK ** -0.5
    q_scale = HEAD_DIM ** -0.5 * LOG2E

    r = _nn(hb, w_ref[:, 0:512])
    for b in range(4):
        qr_ref[:, b * LANES:(b + 1) * LANES] = rope(r[:, b * LANES:(b + 1) * LANES])
    r = _nn(hb, w_ref[:, 512:1024]) * dk_scale
    for b in range(4):
        kr_ref[:, b * LANES:(b + 1) * LANES] = rope(r[:, b * LANES:(b + 1) * LANES])
    for c in range(2):
        vr_ref[:, c * 512:(c + 1) * 512] = _nn(hb, w_ref[:, 1024 + c * 512:1536 + c * 512]).astype(BF16)
    for c in range(2):
        r = _nn(hb, w_ref[:, 2048 + c * 512:2560 + c * 512])
        gr_ref[:, c * 512:(c + 1) * 512] = r * _sigmoid(r)
    r = _nn(hb, w_ref[:, 3072:3584])
    qg = qg_ref[...]
    for b in range(4):
        blk = rope(_head_rms_norm(r[:, b * LANES:(b + 1) * LANES], qg, lo)) * q_scale
        qa_ref[:, b * LANES:(b + 1) * LANES] = blk.astype(BF16)
    r = _nn(hb, w_ref[:, 3584:3840])
    kn = _head_rms_norm(r[:, 0:LANES], kg_ref[...], lo)
    vn = r[:, LANES:2 * LANES]
    if not latent:
        for bb in range(tm // ck_ref.shape[3]):
            rows = slice(bb * ck_ref.shape[3], (bb + 1) * ck_ref.shape[3])
            for kv in range(GQA_KV_HEADS):
                ck_ref[bb, 0, kv] = kn[rows, kv * HEAD_DIM:(kv + 1) * HEAD_DIM]
                cv_ref[bb, 0, kv] = vn[rows, kv * HEAD_DIM:(kv + 1) * HEAD_DIM]
    kn = rope(kn)
    for src, dst in ((kn, kd_ref), (vn, vd_ref)):
        sw = pltpu.roll(src, HEAD_DIM, 1)
        dst[:, 0:LANES] = jnp.where(lo, src, sw).astype(BF16)
        dst[:, LANES:2 * LANES] = jnp.where(lo, sw, src).astype(BF16)


def _in_proj_even(x2d, mods, layer, norm_g, w_bf, q_gain2, k_gain2, rope_tabs, *, n_batch, seq_len, latent):
    n = x2d.shape[0]
    tm = 512
    bidx = _batch_index_fn(latent, tm, seq_len)
    row = lambda i: (i, 0)
    const = lambda i: (0, 0)
    in_specs = [
        pl.BlockSpec((tm, D_MODEL), row),
        _mod_spec(layer, 0, bidx),
        _mod_spec(layer, 1, bidx),
        pl.BlockSpec((1, D_MODEL), const),
        pl.BlockSpec(w_bf.shape, const),
        pl.BlockSpec((1, LANES), const),
        pl.BlockSpec((1, LANES), const),
    ]
    args = [x2d, mods, mods, norm_g, w_bf, q_gain2, k_gain2]
    if latent:
        tiles_per_seq = seq_len // tm
        in_specs += [pl.BlockSpec((tm, LANES), lambda i: (i % tiles_per_seq, 0))] * 2
        args += list(rope_tabs)
    out_shape = [
        jax.ShapeDtypeStruct((n, 512), F32),
        jax.ShapeDtypeStruct((n, 512), F32),
        jax.ShapeDtypeStruct((n, 1024), BF16),
        jax.ShapeDtypeStruct((n, 1024), F32),
        jax.ShapeDtypeStruct((n, 512), BF16),
        jax.ShapeDtypeStruct((n, 256), BF16),
        jax.ShapeDtypeStruct((n, 256), BF16),
    ]
    out_specs = [pl.BlockSpec((tm, s.shape[1]), row) for s in out_shape]
    if not latent:
        cache = jax.ShapeDtypeStruct((n_batch, 1, GQA_KV_HEADS, seq_len, HEAD_DIM), F32)
        out_shape += [cache, cache]
        out_specs += [pl.BlockSpec((tm // seq_len, 1, GQA_KV_HEADS, seq_len, HEAD_DIM),
                                   lambda i: (i, 0, 0, 0, 0))] * 2
    return pl.pallas_call(
        functools.partial(_in_even_kernel, latent=latent, tm=tm),
        out_shape=out_shape,
        grid=(n // tm,),
        in_specs=in_specs,
        out_specs=out_specs,
        compiler_params=_params(1),
        name="in_proj_even_latent" if latent else "in_proj_even_ctx",
    )(*args)


def _retention_kernel(*refs, n, nb, has_state, write_state):
    lg_ref, q_ref, k_ref, v_ref, gr_ref, gn_ref = refs[:6]
    refs = refs[6:]
    if has_state:
        s0f_ref, s0b_ref = refs[:2]
        refs = refs[2:]
    o_ref = refs[0]
    if write_state:
        sf_ref, sb_ref = refs[1:3]
    c = RET_CHUNK
    assert c == LANES and 2 * RET_DK == LANES
    nc = n // c
    p = pl.program_id(0)
    lgf = [lg_ref[0, 2 * p + hh] for hh in range(2)]
    lgb = [lg_ref[1, 2 * p + hh] for hh in range(2)]
    row = lax.broadcasted_iota(jnp.int32, (c, c), 0)
    col = lax.broadcasted_iota(jnp.int32, (c, c), 1)
    diff = (row - col).astype(F32)
    pos = row.astype(F32)
    lane_lo = col < RET_DK

    def both_scans(f, b):
        return (jnp.where(diff >= 0, jnp.exp(f * jnp.maximum(diff, 0.0)), 0.0)
                + jnp.where(diff <= 0, jnp.exp(b * jnp.maximum(-diff, 0.0)), 0.0))

    decay2 = jnp.concatenate([both_scans(lgf[0], lgb[0]), both_scans(lgf[1], lgb[1])], axis=1)
    lgf_lane = jnp.where(lane_lo, lgf[0], lgf[1])
    lgb_lane = jnp.where(lane_lo, lgb[0], lgb[1])
    qd_f = jnp.exp(lgf_lane * (pos + 1.0))
    kd_f = jnp.exp(lgf_lane * (c - 1.0 - pos))
    qd_b = jnp.exp(lgb_lane * (c - pos))
    kd_b = jnp.exp(lgb_lane * pos)
    srow = lax.broadcasted_iota(jnp.int32, (c, 2 * RET_DV), 0)
    scol = lax.broadcasted_iota(jnp.int32, (c, 2 * RET_DV), 1)
    row_a = srow < RET_DK
    col_a = scol < RET_DV
    own = row_a == col_a
    cd_f = jnp.exp(jnp.where(row_a, lgf[0], lgf[1]) * float(c))
    cd_b = jnp.exp(jnp.where(row_a, lgb[0], lgb[1]) * float(c))
    zeros_half = jnp.zeros((RET_DK, RET_DV), F32)
    gn = gn_ref[...]

    def place(s0_ref, bi):
        top = jnp.concatenate([s0_ref[bi, 0, 0], zeros_half], axis=1)
        bot = jnp.concatenate([zeros_half, s0_ref[bi, 0, 1]], axis=1)
        return jnp.concatenate([top, bot], axis=0)

    for bi in range(nb):
        rows = [slice(bi * n + i * c, bi * n + (i + 1) * c) for i in range(nc)]
        if has_state:
            s_f = place(s0f_ref, bi)
            s_b = place(s0b_ref, bi)
        else:
            s_f = jnp.zeros((c, 2 * RET_DV), F32)
            s_b = jnp.zeros((c, 2 * RET_DV), F32)
        before_f = []
        for i in range(nc):
            before_f.append(s_f)
            kv = _tn((k_ref[rows[i], :] * kd_f).astype(BF16), v_ref[rows[i], :])
            s_f = s_f * cd_f + jnp.where(own, kv, 0.0)
        before_b = [None] * nc
        for i in reversed(range(nc)):
            before_b[i] = s_b
            kv = _tn((k_ref[rows[i], :] * kd_b).astype(BF16), v_ref[rows[i], :])
            s_b = s_b * cd_b + jnp.where(own, kv, 0.0)
        if write_state:
            for hh in range(2):
                blk = (slice(hh * RET_DK, (hh + 1) * RET_DK), slice(hh * RET_DV, (hh + 1) * RET_DV))
                sf_ref[bi, 0, hh] = s_f[blk]
                sb_ref[bi, 0, hh] = s_b[blk]
        for i in range(nc):
            qc = q_ref[rows[i], :]
            kc = k_ref[rows[i], :]
            vc = v_ref[rows[i], :]
            k_cat = jnp.concatenate([jnp.where(lane_lo, kc, 0.0), jnp.where(lane_lo, 0.0, kc)], axis=0)
            v_blk = jnp.concatenate([jnp.where(col_a, vc, jnp.zeros_like(vc)),
                                     jnp.where(col_a, jnp.zeros_like(vc), vc)], axis=0)
            scores = _nt(qc.astype(BF16), k_cat.astype(BF16)) * decay2
            q_cat = jnp.concatenate([(qc * qd_f).astype(BF16), (qc * qd_b).astype(BF16)], axis=1)
            s_cat = jnp.concatenate([before_f[i], before_b[i]], axis=0).astype(BF16)
            o = _nn(scores.astype(BF16), v_blk) + _nn(q_cat, s_cat)
            for hh in range(2):
                vcols = slice(hh * RET_DV, (hh + 1) * RET_DV)
                oh = o[:, vcols]
                mu = jnp.mean(oh, axis=-1, keepdims=True)
                d = oh - mu
                var = jnp.mean(d * d, axis=-1, keepdims=True)
                y = d * lax.rsqrt(var + GN_EPS) * gn[:, vcols] * gr_ref[rows[i], vcols]
                o_ref[rows[i], vcols] = y.astype(BF16)


def _retention(log_g, qr, kr, vr, gr, gn, state_f, state_b, *, n_batch, seq_len, write_state):
    n = qr.shape[0]
    pairs = RET_HEADS // 2
    has_state = state_f is not None
    nb = max(1, 1024 // seq_len)
    rows = nb * seq_len
    tok = lambda p, g: (g, p)
    in_specs = [
        pl.BlockSpec(memory_space=pltpu.SMEM),
        pl.BlockSpec((rows, LANES), tok),
        pl.BlockSpec((rows, LANES), tok),
        pl.BlockSpec((rows, 2 * RET_DV), tok),
        pl.BlockSpec((rows, 2 * RET_DV), tok),
        pl.BlockSpec((1, 2 * RET_DV), lambda p, g: (0, p)),
    ]
    args = [log_g, qr, kr, vr, gr, gn]
    state_spec = pl.BlockSpec((nb, 1, 2, RET_DK, RET_DV), lambda p, g: (g, 0, p, 0, 0))
    if has_state:
        in_specs += [state_spec, state_spec]
        args += [state_f, state_b]
    out_shape = [jax.ShapeDtypeStruct((n, RET_HEADS * RET_DV), BF16)]
    out_specs = [pl.BlockSpec((rows, 2 * RET_DV), tok)]
    if write_state:
        st = jax.ShapeDtypeStruct((n_batch, 1, RET_HEADS, RET_DK, RET_DV), F32)
        out_shape += [st, st]
        out_specs += [state_spec, state_spec]
    return pl.pallas_call(
        functools.partial(_retention_kernel, n=seq_len, nb=nb, has_state=has_state, write_state=write_state),
        out_shape=out_shape,
        grid=(pairs, n_batch // nb),
        in_specs=in_specs,
        out_specs=out_specs,
        compiler_params=_params(2),
        name="retention_latent" if has_state else "retention_ctx",
    )(*args)


def _gqa_kernel(*refs, n_src, tq):
    q_ref = refs[0]
    k_refs = refs[1:1 + 2 * n_src:2]
    v_refs = refs[2:2 + 2 * n_src:2]
    o_ref = refs[1 + 2 * n_src]
    for g in range(GQA_KV_HEADS):
        base = g * 2 * LANES
        q = jnp.concatenate([q_ref[:, base:base + LANES], q_ref[:, base + LANES:base + 2 * LANES]], axis=0)
        kcols = slice(g * LANES, (g + 1) * LANES)
        outs = []
        for half in range(2):
            scores = []
            for k_ref in k_refs:
                kd = k_ref[:, kcols]
                lo = _lane_lo(kd.shape[0])
                keep = lo if half == 0 else jnp.logical_not(lo)
                scores.append(_nt(q, jnp.where(keep, kd, jnp.zeros_like(kd))))
            outs.append(_softmax_apply(scores, [v_ref[:, kcols] for v_ref in v_refs]))
        o = jnp.where(_lane_lo(2 * tq), outs[0], outs[1]).astype(BF16)
        o_ref[:, base:base + LANES] = o[:tq]
        o_ref[:, base + LANES:base + 2 * LANES] = o[tq:]


def _gqa_attention(qa, kd, vd, ctx_kd, ctx_vd, *, n_batch, seq_len, tq):
    n = qa.shape[0]
    tiles = seq_len // tq
    n_src = 1 if ctx_kd is None else 2
    qmap = lambda b, t: (b * tiles + t, 0)
    kmap = lambda b, t: (b, 0)
    in_specs = [pl.BlockSpec((tq, GQA_HEADS * HEAD_DIM), qmap),
                pl.BlockSpec((seq_len, 2 * LANES), kmap),
                pl.BlockSpec((seq_len, 2 * LANES), kmap)]
    args = [qa, kd, vd]
    if n_src == 2:
        past = ctx_kd.shape[1]
        cmap = lambda b, t: (b, 0, 0)
        in_specs += [pl.BlockSpec((None, past, 2 * LANES), cmap)] * 2
        args += [ctx_kd, ctx_vd]
    return pl.pallas_call(
        functools.partial(_gqa_kernel, n_src=n_src, tq=tq),
        out_shape=jax.ShapeDtypeStruct((n, GQA_HEADS * HEAD_DIM), BF16),
        grid=(n_batch, tiles),
        in_specs=in_specs,
        out_specs=pl.BlockSpec((tq, GQA_HEADS * HEAD_DIM), qmap),
        compiler_params=_params(2),
        name="gqa_latent" if n_src == 2 else "gqa_ctx",
    )(*args)


def _out_proj_kernel(*refs, n_mix):
    x_ref, gate_ref = refs[:2]
    m_refs = refs[2:2 + n_mix]
    w_refs = refs[2 + n_mix:2 + 2 * n_mix]
    o_ref = refs[2 + 2 * n_mix]
    acc = None
    for m_ref, w_ref in zip(m_refs, w_refs):
        t = _nn(m_ref[...], w_ref[...])
        acc = t if acc is None else acc + t
    o_ref[...] = x_ref[...] + gate_ref[...] * acc


def _out_proj(x2d, mods, layer, mixes, weights, *, seq_len, latent):
    n = x2d.shape[0]
    tm = 512
    bidx = _batch_index_fn(latent, tm, seq_len)
    row = lambda i: (i, 0)
    const = lambda i: (0, 0)
    in_specs = [pl.BlockSpec((tm, D_MODEL), row), _mod_spec(layer, 2, bidx)]
    in_specs += [pl.BlockSpec((tm, m.shape[1]), row) for m in mixes]
    in_specs += [pl.BlockSpec(w.shape, const) for w in weights]
    return pl.pallas_call(
        functools.partial(_out_proj_kernel, n_mix=len(mixes)),
        out_shape=jax.ShapeDtypeStruct((n, D_MODEL), F32),
        grid=(n // tm,),
        in_specs=in_specs,
        out_specs=pl.BlockSpec((tm, D_MODEL), row),
        compiler_params=_params(1),
        name="out_proj_latent" if latent else "out_proj_ctx",
    )(x2d, mods, *mixes, *weights)


def _zero_rows(arr, rows):
    pieces, cur = [], 0
    sub = lax.broadcasted_iota(jnp.int32, (8, arr.shape[1]), 0)
    for r in sorted(rows):
        g0 = (r // 8) * 8
        if g0 > cur:
            pieces.append(arr[cur:g0])
        pieces.append(jnp.where(sub == r - g0, 0.0, arr[g0:g0 + 8]))
        cur = g0 + 8
    if cur < arr.shape[0]:
        pieces.append(arr[cur:])
    return jnp.concatenate(pieces, axis=0)


def _ffn_kernel(*refs, tm, seq_len, final):
    x_ref, shift_ref, scale_ref, gate_ref, g_ref, wup_ref, cw_ref, cb_ref, wd_ref = refs[:9]
    refs = refs[9:]
    if final:
        gfin_ref = refs[0]
        refs = refs[1:]
    o_ref, hb_ref, act_ref = refs
    hb_ref[...] = _modulated_norm(x_ref[...], g_ref[...], scale_ref[...], shift_ref[...]).astype(BF16)
    seq_starts = list(range(0, tm, seq_len))
    seq_ends = [s + seq_len - 1 for s in seq_starts]
    for j in range(N_FF_CHUNKS):
        u = _nn(hb_ref[...], wup_ref[j])
        cw = cw_ref[j]
        prev = _zero_rows(pltpu.roll(u, 1, 0), seq_starts)
        nxt = _zero_rows(pltpu.roll(u, tm - 1, 0), seq_ends)
        cv = prev * cw[0:1] + u * cw[1:2] + nxt * cw[2:3] + cb_ref[j]
        a = cv[:, :FF_CHUNK]
        act_ref[:, j * FF_CHUNK:(j + 1) * FF_CHUNK] = (a * _sigmoid(a) * cv[:, FF_CHUNK:]).astype(BF16)
    y = x_ref[...] + gate_ref[...] * _nn(act_ref[...], wd_ref[...])
    if final:
        ms = jnp.mean(y * y, axis=-1, keepdims=True)
        y = y * lax.rsqrt(ms + EPS) * gfin_ref[...]
    o_ref[...] = y


def _conv_ffn(x2d, mods, layer, norm_g, wup_c, cw_c, cb_c, wd_c, final_g, *, seq_len, latent):
    n = x2d.shape[0]
    tm = 1024
    bidx = _batch_index_fn(latent, tm, seq_len)
    row = lambda i: (i, 0)
    const2 = lambda i: (0, 0)
    resident = lambda a: pl.BlockSpec((None,) + a.shape[1:], lambda i: (layer,) + (0,) * (a.ndim - 1),
                                      pipeline_mode=pl.Buffered(1))
    in_specs = [pl.BlockSpec((tm, D_MODEL), row), _mod_spec(layer, 3, bidx), _mod_spec(layer, 4, bidx),
                _mod_spec(layer, 5, bidx), pl.BlockSpec((1, D_MODEL), const2),
                resident(wup_c), resident(cw_c), resident(cb_c), resident(wd_c)]
    args = [x2d, mods, mods, mods, norm_g, wup_c, cw_c, cb_c, wd_c]
    final = final_g is not None
    if final:
        in_specs.append(pl.BlockSpec((1, D_MODEL), const2))
        args.append(final_g)
    return pl.pallas_call(
        functools.partial(_ffn_kernel, tm=tm, seq_len=seq_len, final=final),
        out_shape=jax.ShapeDtypeStruct((n, D_MODEL), F32),
        grid=(n // tm,),
        in_specs=in_specs,
        out_specs=pl.BlockSpec((tm, D_MODEL), row),
        scratch_shapes=[pltpu.VMEM((tm, D_MODEL), BF16), pltpu.VMEM((tm, D_FF), BF16)],
        compiler_params=_params(1),
        name="conv_ffn_latent" if latent else "conv_ffn_ctx",
    )(*args)


def _regroup_up_kernel(a_ref, g_ref, o_ref):
    o_ref[:, :FF_CHUNK] = a_ref[...].astype(BF16)
    o_ref[:, FF_CHUNK:] = g_ref[...].astype(BF16)


def _regroup_w_up(w_up):
    depth = w_up.shape[0]
    return pl.pallas_call(
        _regroup_up_kernel,
        out_shape=jax.ShapeDtypeStruct((depth, N_FF_CHUNKS, D_MODEL, 2 * FF_CHUNK), BF16),
        grid=(depth, N_FF_CHUNKS),
        in_specs=[pl.BlockSpec((None, D_MODEL, FF_CHUNK), lambda l, j: (l, 0, j)),
                  pl.BlockSpec((None, D_MODEL, FF_CHUNK), lambda l, j: (l, 0, N_FF_CHUNKS + j))],
        out_specs=pl.BlockSpec((None, None, D_MODEL, 2 * FF_CHUNK), lambda l, j: (l, j, 0, 0)),
        compiler_params=_params(2),
        name="regroup_w_up",
    )(w_up, w_up)


def _in_odd_kernel(*refs, write_cache, tm):
    x_ref, shift_ref, scale_ref, g_ref, w_ref, q_ref, k_ref, v_ref = refs[:8]
    hb = _modulated_norm(x_ref[...], g_ref[...], scale_ref[...], shift_ref[...]).astype(BF16)
    width = NA_HEADS * HEAD_DIM
    q_scale = HEAD_DIM ** -0.5 * LOG2E
    for c in range(2):
        cols = slice(c * 512, (c + 1) * 512)
        q_ref[:, cols] = (_nn(hb, w_ref[:, c * 512:(c + 1) * 512]) * q_scale).astype(BF16)
    for which, dst in ((1, k_ref), (2, v_ref)):
        for c in range(2):
            r = _nn(hb, w_ref[:, which * width + c * 512:which * width + (c + 1) * 512])
            dst[:, c * 512:(c + 1) * 512] = r.astype(BF16)
            if write_cache:
                cache_ref = refs[8 + which - 1]
                seq = cache_ref.shape[3]
                for bb in range(tm // seq):
                    for hd in range(512 // HEAD_DIM):
                        cache_ref[bb, 0, c * (512 // HEAD_DIM) + hd] = (
                            r[bb * seq:(bb + 1) * seq, hd * HEAD_DIM:(hd + 1) * HEAD_DIM])


def _in_proj_odd(x2d, mods, layer, norm_g, w_bf, *, n_batch, seq_len, latent):
    n = x2d.shape[0]
    tm = 512
    width = NA_HEADS * HEAD_DIM
    bidx = _batch_index_fn(latent, tm, seq_len)
    row = lambda i: (i, 0)
    const = lambda i: (0, 0)
    in_specs = [pl.BlockSpec((tm, D_MODEL), row), _mod_spec(layer, 0, bidx), _mod_spec(layer, 1, bidx),
                pl.BlockSpec((1, D_MODEL), const), pl.BlockSpec(w_bf.shape, const)]
    out_shape = [jax.ShapeDtypeStruct((n, width), BF16)] * 3
    out_specs = [pl.BlockSpec((tm, width), row)] * 3
    write_cache = not latent
    if write_cache:
        cache = jax.ShapeDtypeStruct((n_batch, 1, NA_HEADS, seq_len, HEAD_DIM), F32)
        out_shape += [cache, cache]
        out_specs += [pl.BlockSpec((tm // seq_len, 1, NA_HEADS, seq_len, HEAD_DIM),
                                   lambda i: (i, 0, 0, 0, 0))] * 2
    return pl.pallas_call(
        functools.partial(_in_odd_kernel, write_cache=write_cache, tm=tm),
        out_shape=out_shape,
        grid=(n // tm,),
        in_specs=in_specs,
        out_specs=out_specs,
        compiler_params=_params(1),
        name="in_proj_odd_latent" if latent else "in_proj_odd_ctx",
    )(x2d, mods, mods, norm_g, w_bf)


def _pair_attention(q, key_pieces, val_pieces, bias_pieces):
    outs = []
    for half in range(2):
        scores = []
        for kp, bp in zip(key_pieces, bias_pieces):
            lo = _lane_lo(kp.shape[0])
            keep = lo if half == 0 else jnp.logical_not(lo)
            s = _nt(q, jnp.where(keep, kp, jnp.zeros_like(kp)))
            scores.append(s if bp is None else s + bp[half])
        outs.append(_softmax_apply(scores, val_pieces))
    return jnp.where(_lane_lo(q.shape[0]), outs[0], outs[1])


def _dense_pairs_kernel(q_ref, k_ref, v_ref, o_ref):
    for p in range(NA_HEADS // 2):
        cols = slice(p * LANES, (p + 1) * LANES)
        o = _pair_attention(q_ref[:, cols], [k_ref[:, cols]], [v_ref[:, cols]], [None])
        o_ref[:, cols] = o.astype(BF16)


def _dense_attention_ctx(q, k, v, *, n_batch, seq_len):
    width = NA_HEADS * HEAD_DIM
    spec = pl.BlockSpec((seq_len, width), lambda b: (b, 0))
    return pl.pallas_call(
        _dense_pairs_kernel,
        out_shape=jax.ShapeDtypeStruct(q.shape, BF16),
        grid=(n_batch,),
        in_specs=[spec, spec, spec],
        out_specs=spec,
        compiler_params=_params(1),
        name="dense_attention_ctx",
    )(q, k, v)


NA_Q_ROWS = 8
NA_KEY_ROWS = 12


def _na_window_start(r):
    return min(max(r - NA_KH // 2, 0), (1024 // GRID_W) - NA_KH)


def _na_bias_tile(bias_ref, half, tile, n_rows):
    r0 = tile * NA_Q_ROWS
    kr0 = min(max(r0 - NA_KH // 2, 0), n_rows - NA_KEY_ROWS)
    lo = _lane_lo(GRID_W)
    neg_block = jnp.full((GRID_W, LANES), NEG, F32)
    left_off = jnp.where(lo, NEG, 0.0)
    right_off = jnp.where(lo, 0.0, NEG)
    rows = []
    for rq in range(NA_Q_ROWS):
        r = r0 + rq
        rs = _na_window_start(r)
        blocks = []
        for kk in range(NA_KEY_ROWS // 2):
            ka = kr0 + 2 * kk
            va = rs <= ka < rs + NA_KH
            vb = rs <= ka + 1 < rs + NA_KH
            if not (va or vb):
                blocks.append(neg_block)
                continue
            blk = bias_ref[half, ka - r + NA_KH]
            if not va:
                blk = blk + left_off
            if not vb:
                blk = blk + right_off
            blocks.append(blk)
        rows.append(jnp.concatenate(blocks, axis=1))
    return jnp.concatenate(rows, axis=0), kr0


def _na_kernel(q_ref, k_ref, v_ref, ck_ref, cv_ref, bias_ref, o_ref, *, n_rows):
    tq = NA_Q_ROWS * GRID_W
    span = NA_KEY_ROWS * GRID_W
    ck = ck_ref[...]
    cv = cv_ref[...]
    for tile in range(n_rows // NA_Q_ROWS):
        biases = []
        for half in range(2):
            b, kr0 = _na_bias_tile(bias_ref, half, tile, n_rows)
            biases.append(b)
        keys = slice(kr0 * GRID_W, kr0 * GRID_W + span)
        o = _pair_attention(q_ref[tile * tq:(tile + 1) * tq, :], [k_ref[keys, :], ck], [v_ref[keys, :], cv],
                            [biases, None])
        o_ref[tile * tq:(tile + 1) * tq, :] = o.astype(BF16)


def _na_bias_table(rpb):
    cidx = np.arange(GRID_W)
    cs = np.clip(cidx - NA_KW // 2, 0, GRID_W - NA_KW)
    kc = np.arange(GRID_W)
    inside = (kc[None, :] >= cs[:, None]) & (kc[None, :] < cs[:, None] + NA_KW)
    rel = kc[None, :] - cidx[:, None] + NA_KW - 1
    onehot = (rel[None] == np.arange(2 * NA_KW - 1)[:, None, None]) & inside[None]
    m = jnp.einsum("hdj,jck->hdck", rpb * LOG2E, jnp.asarray(onehot, F32), precision=lax.Precision.HIGHEST)
    m = jnp.where(jnp.asarray(inside)[None, None], m, NEG)
    neg = jnp.full((rpb.shape[0], 1, GRID_W, GRID_W), NEG, F32)
    left = jnp.concatenate([neg, m], axis=1)
    right = jnp.concatenate([m, neg], axis=1)
    return jnp.concatenate([left, right], axis=-1)


def _na_attention(q, k, v, ctx_k, ctx_v, bias_tab, *, n_batch, seq_len):
    pairs = NA_HEADS // 2
    tok = lambda p, b: (b, p)
    ctx = lambda p, b: (b, 0, p)
    past = ctx_k.shape[1]
    return pl.pallas_call(
        functools.partial(_na_kernel, n_rows=seq_len // GRID_W),
        out_shape=jax.ShapeDtypeStruct(q.shape, BF16),
        grid=(pairs, n_batch),
        in_specs=[pl.BlockSpec((seq_len, LANES), tok)] * 3
        + [pl.BlockSpec((None, past, LANES), ctx)] * 2
        + [pl.BlockSpec((2, 2 * NA_KH, GRID_W, LANES), lambda p, b: (p, 0, 0, 0))],
        out_specs=pl.BlockSpec((seq_len, LANES), tok),
        compiler_params=_params(2),
        name="neighbourhood_attention",
    )(q, k, v, ctx_k, ctx_v, bias_tab)


def _rope_tables(n):
    t = jnp.arange(n)
    row = (t // GRID_W).astype(F32)
    col = (t % GRID_W).astype(F32)
    half = HEAD_DIM // 2
    inv = ROPE_BASE ** (-jnp.arange(0, half, 2, dtype=F32) / half)
    ang_r = row[:, None] * inv
    ang_c = col[:, None] * inv
    cos_h = jnp.concatenate([jnp.cos(ang_r)] * 2 + [jnp.cos(ang_c)] * 2, axis=-1)
    sin_h = jnp.concatenate([-jnp.sin(ang_r), jnp.sin(ang_r), -jnp.sin(ang_c), jnp.sin(ang_c)], axis=-1)
    return jnp.concatenate([cos_h, cos_h], axis=-1), jnp.concatenate([sin_h, sin_h], axis=-1)


def _ffn_weights(w_up, conv_w, conv_b, w_down):
    depth = w_up.shape[0]

    def regroup(a):
        lead = a.shape[1:-1]
        a = a.reshape(depth, *lead, 2, N_FF_CHUNKS, FF_CHUNK)
        a = jnp.moveaxis(a, -2, 1)
        return a.reshape(depth, N_FF_CHUNKS, *lead, 2 * FF_CHUNK)
    return (_regroup_w_up(w_up), regroup(conv_w), regroup(conv_b[:, None, :]),
            w_down.astype(BF16))


def _token_major_dup(cache):
    b, kv, t, d = cache.shape
    c = jnp.transpose(cache, (0, 2, 1, 3))[:, :, :, None, :]
    return jnp.broadcast_to(c, (b, t, kv, 2, d)).reshape(b, t, kv * 2 * d).astype(BF16)


def _token_major(cache):
    b, h, t, d = cache.shape
    return jnp.transpose(cache, (0, 2, 1, 3)).reshape(b, t, h * d).astype(BF16)


def kernel(x_prompt, x_sample, state_ret_fwd, state_ret_bwd, cache_gqa_k, cache_gqa_v, cache_na_k, cache_na_v,
           c, c_ctx, ada_w, ada_b, norm_mix, norm_ffn, norm_final, even_w_in, even_w_out, ret_decay_fwd,
           ret_decay_bwd, ret_gn, gqa_q_norm, gqa_k_norm, odd_w_in, odd_w_out, na_rpb, ffn_w_up, ffn_conv_w,
           ffn_conv_b, ffn_w_down):
    nb_c, len_c, _ = x_prompt.shape
    nb_s, len_s, _ = x_sample.shape
    depth = ada_w.shape[0]
    streams = {
        False: dict(n_batch=nb_c, seq_len=len_c),
        True: dict(n_batch=nb_s, seq_len=len_s),
    }
    xs = {False: x_prompt.reshape(nb_c * len_c, D_MODEL), True: x_sample.reshape(nb_s * len_s, D_MODEL)}

    rows = 8 * (-(-(1 + nb_s) // 8))
    cond = jnp.zeros((rows, D_MODEL), F32).at[0].set(c_ctx).at[1:1 + nb_s].set(c)
    mods = _ada_params(cond, ada_w, ada_b).reshape(depth, rows, 6, 1, D_MODEL)
    rope_tabs = _rope_tables(len_s)
    ffn_w = _ffn_weights(ffn_w_up, ffn_conv_w, ffn_conv_b, ffn_w_down)
    outs = {}

    for l in range(depth):
        g_mix = norm_mix[l][None, :]
        g_ffn = norm_ffn[l][None, :]
        if l % 2 == 0:
            e = l // 2
            w_in = even_w_in[e].astype(BF16)
            w_out = even_w_out[e].astype(BF16)
            w_out_parts = [w_out[:RET_HEADS * RET_DV], w_out[RET_HEADS * RET_DV:]]
            log_g = jnp.stack([jax.nn.log_sigmoid(ret_decay_fwd[e].astype(F32)),
                               jax.nn.log_sigmoid(ret_decay_bwd[e].astype(F32))])
            gn = ret_gn[e][None, :]
            qg2 = jnp.tile(gqa_q_norm[e], 2)[None, :]
            kg2 = jnp.tile(gqa_k_norm[e], 2)[None, :]
            for latent in (False, True):
                st = streams[latent]
                res = _in_proj_even(xs[latent], mods, l, g_mix, w_in, qg2, kg2, rope_tabs, latent=latent, **st)
                qr, kr, vr, gr, qa, kd, vd = res[:7]
                if latent:
                    ret = _retention(log_g, qr, kr, vr, gr, gn, state_ret_fwd[:, e:e + 1],
                                     state_ret_bwd[:, e:e + 1], write_state=False, **st)[0]
                    att = _gqa_attention(qa, kd, vd, _token_major_dup(cache_gqa_k[:, e]),
                                         _token_major_dup(cache_gqa_v[:, e]), tq=256, **st)
                else:
                    outs.setdefault("gk", []).append(res[7])
                    outs.setdefault("gv", []).append(res[8])
                    ret, s_f, s_b = _retention(log_g, qr, kr, vr, gr, gn, None, None, write_state=True, **st)
                    outs.setdefault("sf", []).append(s_f)
                    outs.setdefault("sb", []).append(s_b)
                    att = _gqa_attention(qa, kd, vd, None, None, tq=st["seq_len"], **st)
                xs[latent] = _out_proj(xs[latent], mods, l, [ret, att], w_out_parts,
                                       seq_len=st["seq_len"], latent=latent)
        else:
            o = l // 2
            w_in = odd_w_in[o].astype(BF16)
            w_out = odd_w_out[o].astype(BF16)
            for latent in (False, True):
                st = streams[latent]
                res = _in_proj_odd(xs[latent], mods, l, g_mix, w_in, latent=latent, **st)
                q, k, v = res[:3]
                if latent:
                    att = _na_attention(q, k, v, _token_major(cache_na_k[:, o]), _token_major(cache_na_v[:, o]),
                                        _na_bias_table(na_rpb[o]), **st)
                else:
                    outs.setdefault("nk", []).append(res[3])
                    outs.setdefault("nv", []).append(res[4])
                    att = _dense_attention_ctx(q, k, v, **st)
                xs[latent] = _out_proj(xs[latent], mods, l, [att], [w_out], seq_len=st["seq_len"], latent=latent)
        final_g = norm_final[None, :] if l == depth - 1 else None
        for latent in (False, True):
            xs[latent] = _conv_ffn(xs[latent], mods, l, g_ffn, *ffn_w, final_g,
                                   seq_len=streams[latent]["seq_len"], latent=latent)

    cat = lambda name: outs[name][0] if len(outs[name]) == 1 else jnp.concatenate(outs[name], axis=1)
    return (xs[False].reshape(nb_c, len_c, D_MODEL), xs[True].reshape(nb_s, len_s, D_MODEL),
            cat("sf"), cat("sb"), cat("gk"), cat("gv"), cat("nk"), cat("nv"))
```

```python
import functools

import numpy as np
import jax
import jax.numpy as jnp
from jax import lax
from jax.experimental import pallas as pl
from jax.experimental.pallas import tpu as pltpu

F32 = jnp.float32
BF16 = jnp.bfloat16

D_MODEL = 1024
GRID_W = 64
HEAD_DIM = 64
ROPE_BASE = 10000.0
EPS = 1e-6
GN_EPS = 1e-5
RET_HEADS = 8
RET_DK = 64
RET_DV = 128
RET_CHUNK = 128
GQA_HEADS = 8
GQA_KV_HEADS = 2
NA_HEADS = 16
NA_KH = 8
NA_KW = 16
D_FF = 2816
LANES = 128
MXU_DIM = 256
FF_CHUNK = MXU_DIM
N_FF_CHUNKS = D_FF // FF_CHUNK
NEG = -1e30
LOG2E = 1.4426950408889634
VMEM_LIMIT = 56 * 1024 * 1024


def _nn(a, b):
    return jnp.dot(a, b, preferred_element_type=F32)


def _nt(a, b):
    return lax.dot_general(a, b, (((1,), (1,)), ((), ())), preferred_element_type=F32)


def _tn(a, b):
    return lax.dot_general(a, b, (((0,), (0,)), ((), ())), preferred_element_type=F32)


def _sigmoid(x):
    return 1.0 / (1.0 + jnp.exp(-x))


def _params(n_axes):
    return pltpu.CompilerParams(dimension_semantics=("arbitrary",) * n_axes, vmem_limit_bytes=VMEM_LIMIT)


def _modulated_norm(x, g, scale, shift):
    ms = jnp.mean(x * x, axis=-1, keepdims=True)
    return (x * lax.rsqrt(ms + EPS) * g) * (1.0 + scale) + shift


def _lane_lo(rows):
    return lax.broadcasted_iota(jnp.int32, (rows, LANES), 1) < HEAD_DIM


def _head_rms_norm(xb, gain, lo):
    sq = xb * xb
    s_lo = jnp.sum(jnp.where(lo, sq, 0.0), axis=-1, keepdims=True)
    s_hi = jnp.sum(jnp.where(lo, 0.0, sq), axis=-1, keepdims=True)
    r = jnp.where(lo, lax.rsqrt(s_lo * (1.0 / HEAD_DIM) + EPS), lax.rsqrt(s_hi * (1.0 / HEAD_DIM) + EPS))
    return xb * r * gain


def _rope(xb, cos, sin_signed, first16):
    partner = jnp.where(first16, pltpu.roll(xb, LANES - 16, 1), pltpu.roll(xb, 16, 1))
    return xb * cos + partner * sin_signed


def _softmax_apply(scores, values):
    m = functools.reduce(jnp.maximum, [jnp.max(s, axis=-1, keepdims=True) for s in scores])
    es = [jnp.exp2(s - m) for s in scores]
    l = functools.reduce(jnp.add, [jnp.sum(e, axis=-1, keepdims=True) for e in es])
    o = None
    for e, v in zip(es, values):
        t = _nn(e.astype(BF16), v)
        o = t if o is None else o + t
    return o * (1.0 / l)


def _run_pipelined(units, scores_of, finish):
    pending = scores_of(units[0])
    for n, unit in enumerate(units):
        nxt = scores_of(units[n + 1]) if n + 1 < len(units) else None
        finish(unit, pending)
        pending = nxt


def _masked_half(kp, half):
    lo = _lane_lo(kp.shape[0])
    return jnp.where(lo if half == 0 else jnp.logical_not(lo), kp, jnp.zeros_like(kp))


def _ada_kernel(c_ref, w_ref, b_ref, o_ref):
    c = c_ref[...]
    a = (c * _sigmoid(c)).astype(BF16)
    o_ref[...] = _nn(a, w_ref[...].astype(BF16)) + b_ref[...]


def _ada_params(cond, ada_w, ada_b):
    depth = ada_w.shape[0]
    rows = cond.shape[0]
    tn = 1024
    return pl.pallas_call(
        _ada_kernel,
        out_shape=jax.ShapeDtypeStruct((depth, rows, 6 * D_MODEL), F32),
        grid=(depth, 6 * D_MODEL // tn),
        in_specs=[
            pl.BlockSpec((rows, D_MODEL), lambda l, j: (0, 0)),
            pl.BlockSpec((None, D_MODEL, tn), lambda l, j: (l, 0, j)),
            pl.BlockSpec((None, 1, tn), lambda l, j: (l, 0, j)),
        ],
        out_specs=pl.BlockSpec((None, rows, tn), lambda l, j: (l, 0, j)),
        compiler_params=_params(2),
        name="ada_params",
    )(cond, ada_w, ada_b.reshape(depth, 1, 6 * D_MODEL))


def _mod_spec(layer, which, bidx):
    return pl.BlockSpec((None, None, None, 1, D_MODEL), lambda i: (layer, bidx(i), which, 0, 0))


def _batch_index_fn(latent, rows_per_tile, seq_len):
    if not latent:
        return lambda i: 0
    return lambda i: 1 + (i * rows_per_tile) // seq_len


def _in_even_kernel(*refs, latent, tm):
    if latent:
        (x_ref, shift_ref, scale_ref, g_ref, w_ref, qg_ref, kg_ref, cos_ref, sin_ref,
         qr_ref, kr_ref, vr_ref, gr_ref, qa_ref, kd_ref, vd_ref) = refs
    else:
        (x_ref, shift_ref, scale_ref, g_ref, w_ref, qg_ref, kg_ref,
         qr_ref, kr_ref, vr_ref, gr_ref, qa_ref, kd_ref, vd_ref, ck_ref, cv_ref) = refs
    hb = _modulated_norm(x_ref[...], g_ref[...], scale_ref[...], shift_ref[...]).astype(BF16)
    lane = lax.broadcasted_iota(jnp.int32, (tm, LANES), 1)
    lo = lane < HEAD_DIM
    if latent:
        cos = cos_ref[...]
        sin = sin_ref[...]
        first16 = (lane % 32) < 16
        rope = lambda v: _rope(v, cos, sin, first16)
    else:
        rope = lambda v: v
    dk_scale = RET_DK ** -0.5
    q_scale = HEAD_DIM ** -0.5 * LOG2E

    r = _nn(hb, w_ref[:, 3072:3584])
    qg = qg_ref[...]
    for b in range(4):
        blk = rope(_head_rms_norm(r[:, b * LANES:(b + 1) * LANES], qg, lo)) * q_scale
        qa_ref[:, b * LANES:(b + 1) * LANES] = blk.astype(BF16)
    r = _nn(hb, w_ref[:, 3584:3840])
    kn = _head_rms_norm(r[:, 0:LANES], kg_ref[...], lo)
    vn = r[:, LANES:2 * LANES]
    if not latent:
        for bb in range(tm // ck_ref.shape[3]):
            rows = slice(bb * ck_ref.shape[3], (bb + 1) * ck_ref.shape[3])
            for kv in range(GQA_KV_HEADS):
                ck_ref[bb, 0, kv] = kn[rows, kv * HEAD_DIM:(kv + 1) * HEAD_DIM]
                cv_ref[bb, 0, kv] = vn[rows, kv * HEAD_DIM:(kv + 1) * HEAD_DIM]
    kn = rope(kn)
    for src, dst in ((kn, kd_ref), (vn, vd_ref)):
        sw = pltpu.roll(src, HEAD_DIM, 1)
        dst[:, 0:LANES] = jnp.where(lo, src, sw).astype(BF16)
        dst[:, LANES:2 * LANES] = jnp.where(lo, sw, src).astype(BF16)
    r = _nn(hb, w_ref[:, 0:512])
    for b in range(4):
        qr_ref[:, b * LANES:(b + 1) * LANES] = rope(r[:, b * LANES:(b + 1) * LANES])
    r = _nn(hb, w_ref[:, 512:1024]) * dk_scale
    for b in range(4):
        kr_ref[:, b * LANES:(b + 1) * LANES] = rope(r[:, b * LANES:(b + 1) * LANES])
    for c in range(2):
        r = _nn(hb, w_ref[:, 2048 + c * 512:2560 + c * 512])
        gr_ref[:, c * 512:(c + 1) * 512] = r * _sigmoid(r)
    for c in range(2):
        vr_ref[:, c * 512:(c + 1) * 512] = _nn(hb, w_ref[:, 1024 + c * 512:1536 + c * 512]).astype(BF16)


def _in_proj_even(x2d, mods, layer, norm_g, w_bf, q_gain2, k_gain2, rope_tabs, *, n_batch, seq_len, latent):
    n = x2d.shape[0]
    tm = 512
    bidx = _batch_index_fn(latent, tm, seq_len)
    row = lambda i: (i, 0)
    const = lambda i: (0, 0)
    in_specs = [
        pl.BlockSpec((tm, D_MODEL), row),
        _mod_spec(layer, 0, bidx),
        _mod_spec(layer, 1, bidx),
        pl.BlockSpec((1, D_MODEL), const),
        pl.BlockSpec(w_bf.shape, const),
        pl.BlockSpec((1, LANES), const),
        pl.BlockSpec((1, LANES), const),
    ]
    args = [x2d, mods, mods, norm_g, w_bf, q_gain2, k_gain2]
    if latent:
        tiles_per_seq = seq_len // tm
        in_specs += [pl.BlockSpec((tm, LANES), lambda i: (i % tiles_per_seq, 0))] * 2
        args += list(rope_tabs)
    out_shape = [
        jax.ShapeDtypeStruct((n, 512), F32),
        jax.ShapeDtypeStruct((n, 512), F32),
        jax.ShapeDtypeStruct((n, 1024), BF16),
        jax.ShapeDtypeStruct((n, 1024), F32),
        jax.ShapeDtypeStruct((n, 512), BF16),
        jax.ShapeDtypeStruct((n, 256), BF16),
        jax.ShapeDtypeStruct((n, 256), BF16),
    ]
    out_specs = [pl.BlockSpec((tm, s.shape[1]), row) for s in out_shape]
    if not latent:
        cache = jax.ShapeDtypeStruct((n_batch, 1, GQA_KV_HEADS, seq_len, HEAD_DIM), F32)
        out_shape += [cache, cache]
        out_specs += [pl.BlockSpec((tm // seq_len, 1, GQA_KV_HEADS, seq_len, HEAD_DIM),
                                   lambda i: (i, 0, 0, 0, 0))] * 2
    return pl.pallas_call(
        functools.partial(_in_even_kernel, latent=latent, tm=tm),
        out_shape=out_shape,
        grid=(n // tm,),
        in_specs=in_specs,
        out_specs=out_specs,
        compiler_params=_params(1),
        name="in_proj_even_latent" if latent else "in_proj_even_ctx",
    )(*args)


def _retention_kernel(*refs, n, nb, has_state, write_state):
    lg_ref, q_ref, k_ref, v_ref, gr_ref, gn_ref = refs[:6]
    refs = refs[6:]
    if has_state:
        s0f_ref, s0b_ref = refs[:2]
        refs = refs[2:]
    o_ref = refs[0]
    if write_state:
        sf_ref, sb_ref = refs[1:3]
    c = RET_CHUNK
    assert c == LANES and 2 * RET_DK == LANES
    nc = n // c
    p = pl.program_id(0)
    lgf = [lg_ref[0, 2 * p + hh] for hh in range(2)]
    lgb = [lg_ref[1, 2 * p + hh] for hh in range(2)]
    row = lax.broadcasted_iota(jnp.int32, (c, c), 0)
    col = lax.broadcasted_iota(jnp.int32, (c, c), 1)
    diff = (row - col).astype(F32)
    pos = row.astype(F32)
    lane_lo = col < RET_DK

    def both_scans(f, b):
        return (jnp.where(diff >= 0, jnp.exp(f * jnp.maximum(diff, 0.0)), 0.0)
                + jnp.where(diff <= 0, jnp.exp(b * jnp.maximum(-diff, 0.0)), 0.0))

    decay2 = jnp.concatenate([both_scans(lgf[0], lgb[0]), both_scans(lgf[1], lgb[1])], axis=1)
    lgf_lane = jnp.where(lane_lo, lgf[0], lgf[1])
    lgb_lane = jnp.where(lane_lo, lgb[0], lgb[1])
    qd_f = jnp.exp(lgf_lane * (pos + 1.0))
    kd_f = jnp.exp(lgf_lane * (c - 1.0 - pos))
    qd_b = jnp.exp(lgb_lane * (c - pos))
    kd_b = jnp.exp(lgb_lane * pos)
    srow = lax.broadcasted_iota(jnp.int32, (c, 2 * RET_DV), 0)
    scol = lax.broadcasted_iota(jnp.int32, (c, 2 * RET_DV), 1)
    row_a = srow < RET_DK
    col_a = scol < RET_DV
    own = row_a == col_a
    cd_f = jnp.exp(jnp.where(row_a, lgf[0], lgf[1]) * float(c))
    cd_b = jnp.exp(jnp.where(row_a, lgb[0], lgb[1]) * float(c))
    zeros_half = jnp.zeros((RET_DK, RET_DV), F32)
    gn = gn_ref[...]

    def place(s0_ref, bi):
        top = jnp.concatenate([s0_ref[bi, 0, 0], zeros_half], axis=1)
        bot = jnp.concatenate([zeros_half, s0_ref[bi, 0, 1]], axis=1)
        return jnp.concatenate([top, bot], axis=0)

    for bi in range(nb):
        rows = [slice(bi * n + i * c, bi * n + (i + 1) * c) for i in range(nc)]
        if has_state:
            s_f = place(s0f_ref, bi)
            s_b = place(s0b_ref, bi)
        else:
            s_f = jnp.zeros((c, 2 * RET_DV), F32)
            s_b = jnp.zeros((c, 2 * RET_DV), F32)
        before_f = []
        for i in range(nc):
            before_f.append(s_f)
            kv = _tn((k_ref[rows[i], :] * kd_f).astype(BF16), v_ref[rows[i], :])
            s_f = s_f * cd_f + jnp.where(own, kv, 0.0)
        before_b = [None] * nc
        for i in reversed(range(nc)):
            before_b[i] = s_b
            kv = _tn((k_ref[rows[i], :] * kd_b).astype(BF16), v_ref[rows[i], :])
            s_b = s_b * cd_b + jnp.where(own, kv, 0.0)
        if write_state:
            for hh in range(2):
                blk = (slice(hh * RET_DK, (hh + 1) * RET_DK), slice(hh * RET_DV, (hh + 1) * RET_DV))
                sf_ref[bi, 0, hh] = s_f[blk]
                sb_ref[bi, 0, hh] = s_b[blk]
        for i in range(nc):
            qc = q_ref[rows[i], :]
            kc = k_ref[rows[i], :]
            vc = v_ref[rows[i], :]
            k_cat = jnp.concatenate([jnp.where(lane_lo, kc, 0.0), jnp.where(lane_lo, 0.0, kc)], axis=0)
            v_blk = jnp.concatenate([jnp.where(col_a, vc, jnp.zeros_like(vc)),
                                     jnp.where(col_a, jnp.zeros_like(vc), vc)], axis=0)
            scores = _nt(qc.astype(BF16), k_cat.astype(BF16)) * decay2
            q_cat = jnp.concatenate([(qc * qd_f).astype(BF16), (qc * qd_b).astype(BF16)], axis=1)
            s_cat = jnp.concatenate([before_f[i], before_b[i]], axis=0).astype(BF16)
            o = _nn(scores.astype(BF16), v_blk) + _nn(q_cat, s_cat)
            for hh in range(2):
                vcols = slice(hh * RET_DV, (hh + 1) * RET_DV)
                oh = o[:, vcols]
                mu = jnp.mean(oh, axis=-1, keepdims=True)
                d = oh - mu
                var = jnp.mean(d * d, axis=-1, keepdims=True)
                y = d * lax.rsqrt(var + GN_EPS) * gn[:, vcols] * gr_ref[rows[i], vcols]
                o_ref[rows[i], vcols] = y.astype(BF16)


def _retention(log_g, qr, kr, vr, gr, gn, state_f, state_b, *, n_batch, seq_len, write_state):
    n = qr.shape[0]
    pairs = RET_HEADS // 2
    has_state = state_f is not None
    nb = max(1, 1024 // seq_len)
    rows = nb * seq_len
    tok = lambda p, g: (g, p)
    in_specs = [
        pl.BlockSpec(memory_space=pltpu.SMEM),
        pl.BlockSpec((rows, LANES), tok),
        pl.BlockSpec((rows, LANES), tok),
        pl.BlockSpec((rows, 2 * RET_DV), tok),
        pl.BlockSpec((rows, 2 * RET_DV), tok),
        pl.BlockSpec((1, 2 * RET_DV), lambda p, g: (0, p)),
    ]
    args = [log_g, qr, kr, vr, gr, gn]
    state_spec = pl.BlockSpec((nb, 1, 2, RET_DK, RET_DV), lambda p, g: (g, 0, p, 0, 0))
    if has_state:
        in_specs += [state_spec, state_spec]
        args += [state_f, state_b]
    out_shape = [jax.ShapeDtypeStruct((n, RET_HEADS * RET_DV), BF16)]
    out_specs = [pl.BlockSpec((rows, 2 * RET_DV), tok)]
    if write_state:
        st = jax.ShapeDtypeStruct((n_batch, 1, RET_HEADS, RET_DK, RET_DV), F32)
        out_shape += [st, st]
        out_specs += [state_spec, state_spec]
    return pl.pallas_call(
        functools.partial(_retention_kernel, n=seq_len, nb=nb, has_state=has_state, write_state=write_state),
        out_shape=out_shape,
        grid=(pairs, n_batch // nb),
        in_specs=in_specs,
        out_specs=out_specs,
        compiler_params=_params(2),
        name="retention_latent" if has_state else "retention_ctx",
    )(*args)


def _gqa_kernel(*refs, n_src, tq):
    q_ref = refs[0]
    k_refs = refs[1:1 + 2 * n_src:2]
    v_refs = refs[2:2 + 2 * n_src:2]
    o_ref = refs[1 + 2 * n_src]
    units = [(g, half) for g in range(GQA_KV_HEADS) for half in range(2)]
    outs = {}

    def scores_of(unit):
        g, half = unit
        base = g * 2 * LANES
        q = jnp.concatenate([q_ref[:, base:base + LANES], q_ref[:, base + LANES:base + 2 * LANES]], axis=0)
        return [_nt(q, _masked_half(k_ref[:, g * LANES:(g + 1) * LANES], half)) for k_ref in k_refs]

    def finish(unit, scores):
        g, half = unit
        outs[half] = _softmax_apply(scores, [v_ref[:, g * LANES:(g + 1) * LANES] for v_ref in v_refs])
        if half == 1:
            base = g * 2 * LANES
            o = jnp.where(_lane_lo(2 * tq), outs[0], outs[1]).astype(BF16)
            o_ref[:, base:base + LANES] = o[:tq]
            o_ref[:, base + LANES:base + 2 * LANES] = o[tq:]

    _run_pipelined(units, scores_of, finish)


def _gqa_attention(qa, kd, vd, ctx_kd, ctx_vd, *, n_batch, seq_len, tq):
    n = qa.shape[0]
    tiles = seq_len // tq
    n_src = 1 if ctx_kd is None else 2
    qmap = lambda b, t: (b * tiles + t, 0)
    kmap = lambda b, t: (b, 0)
    in_specs = [pl.BlockSpec((tq, GQA_HEADS * HEAD_DIM), qmap),
                pl.BlockSpec((seq_len, 2 * LANES), kmap),
                pl.BlockSpec((seq_len, 2 * LANES), kmap)]
    args = [qa, kd, vd]
    if n_src == 2:
        past = ctx_kd.shape[1]
        cmap = lambda b, t: (b, 0, 0)
        in_specs += [pl.BlockSpec((None, past, 2 * LANES), cmap)] * 2
        args += [ctx_kd, ctx_vd]
    return pl.pallas_call(
        functools.partial(_gqa_kernel, n_src=n_src, tq=tq),
        out_shape=jax.ShapeDtypeStruct((n, GQA_HEADS * HEAD_DIM), BF16),
        grid=(n_batch, tiles),
        in_specs=in_specs,
        out_specs=pl.BlockSpec((tq, GQA_HEADS * HEAD_DIM), qmap),
        compiler_params=_params(2),
        name="gqa_latent" if n_src == 2 else "gqa_ctx",
    )(*args)


def _out_proj_kernel(*refs, n_mix):
    x_ref, gate_ref, g_ref, scale_ref, shift_ref = refs[:5]
    m_refs = refs[5:5 + n_mix]
    w_refs = refs[5 + n_mix:5 + 2 * n_mix]
    o_ref, h_ref = refs[5 + 2 * n_mix:]
    acc = None
    for m_ref, w_ref in zip(m_refs, w_refs):
        t = _nn(m_ref[...], w_ref[...])
        acc = t if acc is None else acc + t
    y = x_ref[...] + gate_ref[...] * acc
    o_ref[...] = y
    h_ref[...] = _modulated_norm(y, g_ref[...], scale_ref[...], shift_ref[...]).astype(BF16)


def _out_proj(x2d, mods, layer, mixes, weights, ffn_norm_g, *, seq_len, latent):
    n = x2d.shape[0]
    tm = 512
    bidx = _batch_index_fn(latent, tm, seq_len)
    row = lambda i: (i, 0)
    const = lambda i: (0, 0)
    in_specs = [pl.BlockSpec((tm, D_MODEL), row), _mod_spec(layer, 2, bidx), pl.BlockSpec((1, D_MODEL), const),
                _mod_spec(layer, 4, bidx), _mod_spec(layer, 3, bidx)]
    in_specs += [pl.BlockSpec((tm, m.shape[1]), row) for m in mixes]
    in_specs += [pl.BlockSpec(w.shape, const) for w in weights]
    return pl.pallas_call(
        functools.partial(_out_proj_kernel, n_mix=len(mixes)),
        out_shape=[jax.ShapeDtypeStruct((n, D_MODEL), F32), jax.ShapeDtypeStruct((n, D_MODEL), BF16)],
        grid=(n // tm,),
        in_specs=in_specs,
        out_specs=[pl.BlockSpec((tm, D_MODEL), row), pl.BlockSpec((tm, D_MODEL), row)],
        compiler_params=_params(1),
        name="out_proj_latent" if latent else "out_proj_ctx",
    )(x2d, mods, ffn_norm_g, mods, mods, *mixes, *weights)


def _zero_rows(arr, rows):
    pieces, cur = [], 0
    sub = lax.broadcasted_iota(jnp.int32, (8, arr.shape[1]), 0)
    for r in sorted(rows):
        g0 = (r // 8) * 8
        if g0 > cur:
            pieces.append(arr[cur:g0])
        pieces.append(jnp.where(sub == r - g0, 0.0, arr[g0:g0 + 8]))
        cur = g0 + 8
    if cur < arr.shape[0]:
        pieces.append(arr[cur:])
    return jnp.concatenate(pieces, axis=0)


def _ffn_kernel(*refs, tm, seq_len, final):
    x_ref, hb_ref, gate_ref, wup_ref, cw_ref, cb_ref, wd_ref = refs[:7]
    refs = refs[7:]
    if final:
        gfin_ref = refs[0]
        refs = refs[1:]
    o_ref, act_ref, hbs_ref = refs
    hbs_ref[...] = hb_ref[...]
    seq_starts = list(range(0, tm, seq_len))
    seq_ends = [s + seq_len - 1 for s in seq_starts]
    for j in range(N_FF_CHUNKS):
        u = _nn(hbs_ref[...], wup_ref[j])
        cw = cw_ref[j]
        prev = _zero_rows(pltpu.roll(u, 1, 0), seq_starts)
        nxt = _zero_rows(pltpu.roll(u, tm - 1, 0), seq_ends)
        cv = prev * cw[0:1] + u * cw[1:2] + nxt * cw[2:3] + cb_ref[j]
        a = cv[:, :FF_CHUNK]
        act_ref[:, j * FF_CHUNK:(j + 1) * FF_CHUNK] = (a * _sigmoid(a) * cv[:, FF_CHUNK:]).astype(BF16)
    y = x_ref[...] + gate_ref[...] * _nn(act_ref[...], wd_ref[...])
    if final:
        ms = jnp.mean(y * y, axis=-1, keepdims=True)
        y = y * lax.rsqrt(ms + EPS) * gfin_ref[...]
    o_ref[...] = y


def _conv_ffn(x2d, hb2d, mods, layer, wup_c, cw_c, cb_c, wd_c, final_g, *, seq_len, latent):
    n = x2d.shape[0]
    tm = 1024
    bidx = _batch_index_fn(latent, tm, seq_len)
    row = lambda i: (i, 0)
    const2 = lambda i: (0, 0)
    resident = lambda a: pl.BlockSpec((None,) + a.shape[1:], lambda i: (layer,) + (0,) * (a.ndim - 1),
                                      pipeline_mode=pl.Buffered(1))
    in_specs = [pl.BlockSpec((tm, D_MODEL), row), pl.BlockSpec((tm, D_MODEL), row), _mod_spec(layer, 5, bidx),
                resident(wup_c), resident(cw_c), resident(cb_c), resident(wd_c)]
    args = [x2d, hb2d, mods, wup_c, cw_c, cb_c, wd_c]
    final = final_g is not None
    if final:
        in_specs.append(pl.BlockSpec((1, D_MODEL), const2))
        args.append(final_g)
    return pl.pallas_call(
        functools.partial(_ffn_kernel, tm=tm, seq_len=seq_len, final=final),
        out_shape=jax.ShapeDtypeStruct((n, D_MODEL), F32),
        grid=(n // tm,),
        in_specs=in_specs,
        out_specs=pl.BlockSpec((tm, D_MODEL), row),
        scratch_shapes=[pltpu.VMEM((tm, D_FF), BF16), pltpu.VMEM((tm, D_MODEL), BF16)],
        compiler_params=_params(1),
        name="conv_ffn_latent" if latent else "conv_ffn_ctx",
    )(*args)


def _regroup_up_kernel(a_ref, g_ref, o_ref):
    o_ref[:, :FF_CHUNK] = a_ref[...].astype(BF16)
    o_ref[:, FF_CHUNK:] = g_ref[...].astype(BF16)


def _regroup_w_up(w_up):
    depth = w_up.shape[0]
    return pl.pallas_call(
        _regroup_up_kernel,
        out_shape=jax.ShapeDtypeStruct((depth, N_FF_CHUNKS, D_MODEL, 2 * FF_CHUNK), BF16),
        grid=(depth, N_FF_CHUNKS),
        in_specs=[pl.BlockSpec((None, D_MODEL, FF_CHUNK), lambda l, j: (l, 0, j)),
                  pl.BlockSpec((None, D_MODEL, FF_CHUNK), lambda l, j: (l, 0, N_FF_CHUNKS + j))],
        out_specs=pl.BlockSpec((None, None, D_MODEL, 2 * FF_CHUNK), lambda l, j: (l, j, 0, 0)),
        compiler_params=_params(2),
        name="regroup_w_up",
    )(w_up, w_up)


def _in_odd_kernel(*refs, write_cache, tm):
    x_ref, shift_ref, scale_ref, g_ref, w_ref, q_ref, k_ref, v_ref = refs[:8]
    hb = _modulated_norm(x_ref[...], g_ref[...], scale_ref[...], shift_ref[...]).astype(BF16)
    width = NA_HEADS * HEAD_DIM
    q_scale = HEAD_DIM ** -0.5 * LOG2E
    for which, dst in ((1, k_ref), (2, v_ref)):
        for c in range(2):
            r = _nn(hb, w_ref[:, which * width + c * 512:which * width + (c + 1) * 512])
            dst[:, c * 512:(c + 1) * 512] = r.astype(BF16)
            if write_cache:
                cache_ref = refs[8 + which - 1]
                seq = cache_ref.shape[3]
                for bb in range(tm // seq):
                    for hd in range(512 // HEAD_DIM):
                        cache_ref[bb, 0, c * (512 // HEAD_DIM) + hd] = (
                            r[bb * seq:(bb + 1) * seq, hd * HEAD_DIM:(hd + 1) * HEAD_DIM])
    for c in range(2):
        cols = slice(c * 512, (c + 1) * 512)
        q_ref[:, cols] = (_nn(hb, w_ref[:, c * 512:(c + 1) * 512]) * q_scale).astype(BF16)


def _in_proj_odd(x2d, mods, layer, norm_g, w_bf, *, n_batch, seq_len, latent):
    n = x2d.shape[0]
    tm = 512
    width = NA_HEADS * HEAD_DIM
    bidx = _batch_index_fn(latent, tm, seq_len)
    row = lambda i: (i, 0)
    const = lambda i: (0, 0)
    in_specs = [pl.BlockSpec((tm, D_MODEL), row), _mod_spec(layer, 0, bidx), _mod_spec(layer, 1, bidx),
                pl.BlockSpec((1, D_MODEL), const), pl.BlockSpec(w_bf.shape, const)]
    out_shape = [jax.ShapeDtypeStruct((n, width), BF16)] * 3
    out_specs = [pl.BlockSpec((tm, width), row)] * 3
    write_cache = not latent
    if write_cache:
        cache = jax.ShapeDtypeStruct((n_batch, 1, NA_HEADS, seq_len, HEAD_DIM), F32)
        out_shape += [cache, cache]
        out_specs += [pl.BlockSpec((tm // seq_len, 1, NA_HEADS, seq_len, HEAD_DIM),
                                   lambda i: (i, 0, 0, 0, 0))] * 2
    return pl.pallas_call(
        functools.partial(_in_odd_kernel, write_cache=write_cache, tm=tm),
        out_shape=out_shape,
        grid=(n // tm,),
        in_specs=in_specs,
        out_specs=out_specs,
        compiler_params=_params(1),
        name="in_proj_odd_latent" if latent else "in_proj_odd_ctx",
    )(x2d, mods, mods, norm_g, w_bf)


def _dense_pairs_kernel(q_ref, k_ref, v_ref, o_ref):
    units = [(p, half) for p in range(NA_HEADS // 2) for half in range(2)]
    outs = {}

    def scores_of(unit):
        p, half = unit
        cols = slice(p * LANES, (p + 1) * LANES)
        return [_nt(q_ref[:, cols], _masked_half(k_ref[:, cols], half))]

    def finish(unit, scores):
        p, half = unit
        cols = slice(p * LANES, (p + 1) * LANES)
        outs[half] = _softmax_apply(scores, [v_ref[:, cols]])
        if half == 1:
            o_ref[:, cols] = jnp.where(_lane_lo(q_ref.shape[0]), outs[0], outs[1]).astype(BF16)

    _run_pipelined(units, scores_of, finish)


def _dense_attention_ctx(q, k, v, *, n_batch, seq_len):
    width = NA_HEADS * HEAD_DIM
    spec = pl.BlockSpec((seq_len, width), lambda b: (b, 0))
    return pl.pallas_call(
        _dense_pairs_kernel,
        out_shape=jax.ShapeDtypeStruct(q.shape, BF16),
        grid=(n_batch,),
        in_specs=[spec, spec, spec],
        out_specs=spec,
        compiler_params=_params(1),
        name="dense_attention_ctx",
    )(q, k, v)


NA_Q_ROWS = 8
NA_KEY_ROWS = 12


def _na_window_start(r):
    return min(max(r - NA_KH // 2, 0), (1024 // GRID_W) - NA_KH)


def _na_bias_tile(bias_ref, half, tile, n_rows):
    r0 = tile * NA_Q_ROWS
    kr0 = min(max(r0 - NA_KH // 2, 0), n_rows - NA_KEY_ROWS)
    lo = _lane_lo(GRID_W)
    neg_block = jnp.full((GRID_W, LANES), NEG, F32)
    left_off = jnp.where(lo, NEG, 0.0)
    right_off = jnp.where(lo, 0.0, NEG)
    rows = []
    for rq in range(NA_Q_ROWS):
        r = r0 + rq
        rs = _na_window_start(r)
        blocks = []
        for kk in range(NA_KEY_ROWS // 2):
            ka = kr0 + 2 * kk
            va = rs <= ka < rs + NA_KH
            vb = rs <= ka + 1 < rs + NA_KH
            if not (va or vb):
                blocks.append(neg_block)
                continue
            blk = bias_ref[half, ka - r + NA_KH]
            if not va:
                blk = blk + left_off
            if not vb:
                blk = blk + right_off
            blocks.append(blk)
        rows.append(jnp.concatenate(blocks, axis=1))
    return jnp.concatenate(rows, axis=0), kr0


def _na_kernel(q_ref, k_ref, v_ref, ck_ref, cv_ref, bias_ref, o_ref, *, n_rows):
    tq = NA_Q_ROWS * GRID_W
    span = NA_KEY_ROWS * GRID_W
    units = [(tile, half) for tile in range(n_rows // NA_Q_ROWS) for half in range(2)]
    outs = {}

    def window(tile):
        kr0 = min(max(tile * NA_Q_ROWS - NA_KH // 2, 0), n_rows - NA_KEY_ROWS)
        return slice(kr0 * GRID_W, kr0 * GRID_W + span)

    def scores_of(unit):
        tile, half = unit
        q = q_ref[tile * tq:(tile + 1) * tq, :]
        bias, _ = _na_bias_tile(bias_ref, half, tile, n_rows)
        return [_nt(q, _masked_half(k_ref[window(tile), :], half)) + bias, _nt(q, _masked_half(ck_ref[...], half))]

    def finish(unit, scores):
        tile, half = unit
        outs[half] = _softmax_apply(scores, [v_ref[window(tile), :], cv_ref[...]])
        if half == 1:
            o_ref[tile * tq:(tile + 1) * tq, :] = jnp.where(_lane_lo(tq), outs[0], outs[1]).astype(BF16)

    _run_pipelined(units, scores_of, finish)


def _na_bias_table(rpb):
    cidx = np.arange(GRID_W)
    cs = np.clip(cidx - NA_KW // 2, 0, GRID_W - NA_KW)
    kc = np.arange(GRID_W)
    inside = (kc[None, :] >= cs[:, None]) & (kc[None, :] < cs[:, None] + NA_KW)
    rel = kc[None, :] - cidx[:, None] + NA_KW - 1
    onehot = (rel[None] == np.arange(2 * NA_KW - 1)[:, None, None]) & inside[None]
    m = jnp.einsum("hdj,jck->hdck", rpb * LOG2E, jnp.asarray(onehot, F32), precision=lax.Precision.HIGHEST)
    m = jnp.where(jnp.asarray(inside)[None, None], m, NEG)
    neg = jnp.full((rpb.shape[0], 1, GRID_W, GRID_W), NEG, F32)
    left = jnp.concatenate([neg, m], axis=1)
    right = jnp.concatenate([m, neg], axis=1)
    return jnp.concatenate([left, right], axis=-1)


def _na_attention(q, k, v, ctx_k, ctx_v, bias_tab, *, n_batch, seq_len):
    pairs = NA_HEADS // 2
    tok = lambda p, b: (b, p)
    ctx = lambda p, b: (b, 0, p)
    past = ctx_k.shape[1]
    return pl.pallas_call(
        functools.partial(_na_kernel, n_rows=seq_len // GRID_W),
        out_shape=jax.ShapeDtypeStruct(q.shape, BF16),
        grid=(pairs, n_batch),
        in_specs=[pl.BlockSpec((seq_len, LANES), tok)] * 3
        + [pl.BlockSpec((None, past, LANES), ctx)] * 2
        + [pl.BlockSpec((2, 2 * NA_KH, GRID_W, LANES), lambda p, b: (p, 0, 0, 0))],
        out_specs=pl.BlockSpec((seq_len, LANES), tok),
        compiler_params=_params(2),
        name="neighbourhood_attention",
    )(q, k, v, ctx_k, ctx_v, bias_tab)


def _rope_tables(n):
    t = jnp.arange(n)
    row = (t // GRID_W).astype(F32)
    col = (t % GRID_W).astype(F32)
    half = HEAD_DIM // 2
    inv = ROPE_BASE ** (-jnp.arange(0, half, 2, dtype=F32) / half)
    ang_r = row[:, None] * inv
    ang_c = col[:, None] * inv
    cos_h = jnp.concatenate([jnp.cos(ang_r)] * 2 + [jnp.cos(ang_c)] * 2, axis=-1)
    sin_h = jnp.concatenate([-jnp.sin(ang_r), jnp.sin(ang_r), -jnp.sin(ang_c), jnp.sin(ang_c)], axis=-1)
    return jnp.concatenate([cos_h, cos_h], axis=-1), jnp.concatenate([sin_h, sin_h], axis=-1)


def _ffn_weights(w_up, conv_w, conv_b, w_down):
    depth = w_up.shape[0]

    def regroup(a):
        lead = a.shape[1:-1]
        a = a.reshape(depth, *lead, 2, N_FF_CHUNKS, FF_CHUNK)
        a = jnp.moveaxis(a, -2, 1)
        return a.reshape(depth, N_FF_CHUNKS, *lead, 2 * FF_CHUNK)
    return (_regroup_w_up(w_up), regroup(conv_w), regroup(conv_b[:, None, :]),
            w_down.astype(BF16))


def _token_major_dup(cache):
    b, kv, t, d = cache.shape
    c = jnp.transpose(cache, (0, 2, 1, 3))[:, :, :, None, :]
    return jnp.broadcast_to(c, (b, t, kv, 2, d)).reshape(b, t, kv * 2 * d).astype(BF16)


def _token_major(cache):
    b, h, t, d = cache.shape
    return jnp.transpose(cache, (0, 2, 1, 3)).reshape(b, t, h * d).astype(BF16)


def kernel(x_prompt, x_sample, state_ret_fwd, state_ret_bwd, cache_gqa_k, cache_gqa_v, cache_na_k, cache_na_v,
           c, c_ctx, ada_w, ada_b, norm_mix, norm_ffn, norm_final, even_w_in, even_w_out, ret_decay_fwd,
           ret_decay_bwd, ret_gn, gqa_q_norm, gqa_k_norm, odd_w_in, odd_w_out, na_rpb, ffn_w_up, ffn_conv_w,
           ffn_conv_b, ffn_w_down):
    nb_c, len_c, _ = x_prompt.shape
    nb_s, len_s, _ = x_sample.shape
    depth = ada_w.shape[0]
    streams = {
        False: dict(n_batch=nb_c, seq_len=len_c),
        True: dict(n_batch=nb_s, seq_len=len_s),
    }
    xs = {False: x_prompt.reshape(nb_c * len_c, D_MODEL), True: x_sample.reshape(nb_s * len_s, D_MODEL)}

    rows = 8 * (-(-(1 + nb_s) // 8))
    cond = jnp.zeros((rows, D_MODEL), F32).at[0].set(c_ctx).at[1:1 + nb_s].set(c)
    mods = _ada_params(cond, ada_w, ada_b).reshape(depth, rows, 6, 1, D_MODEL)
    rope_tabs = _rope_tables(len_s)
    ffn_w = _ffn_weights(ffn_w_up, ffn_conv_w, ffn_conv_b, ffn_w_down)
    outs = {}
    hs = {}

    for l in range(depth):
        g_mix = norm_mix[l][None, :]
        g_ffn = norm_ffn[l][None, :]
        if l % 2 == 0:
            e = l // 2
            w_in = even_w_in[e].astype(BF16)
            w_out = even_w_out[e].astype(BF16)
            w_out_parts = [w_out[:RET_HEADS * RET_DV], w_out[RET_HEADS * RET_DV:]]
            log_g = jnp.stack([jax.nn.log_sigmoid(ret_decay_fwd[e].astype(F32)),
                               jax.nn.log_sigmoid(ret_decay_bwd[e].astype(F32))])
            gn = ret_gn[e][None, :]
            qg2 = jnp.tile(gqa_q_norm[e], 2)[None, :]
            kg2 = jnp.tile(gqa_k_norm[e], 2)[None, :]
            for latent in (False, True):
                st = streams[latent]
                res = _in_proj_even(xs[latent], mods, l, g_mix, w_in, qg2, kg2, rope_tabs, latent=latent, **st)
                qr, kr, vr, gr, qa, kd, vd = res[:7]
                if latent:
                    ret = _retention(log_g, qr, kr, vr, gr, gn, state_ret_fwd[:, e:e + 1],
                                     state_ret_bwd[:, e:e + 1], write_state=False, **st)[0]
                    att = _gqa_attention(qa, kd, vd, _token_major_dup(cache_gqa_k[:, e]),
                                         _token_major_dup(cache_gqa_v[:, e]), tq=512, **st)
                else:
                    outs.setdefault("gk", []).append(res[7])
                    outs.setdefault("gv", []).append(res[8])
                    ret, s_f, s_b = _retention(log_g, qr, kr, vr, gr, gn, None, None, write_state=True, **st)
                    outs.setdefault("sf", []).append(s_f)
                    outs.setdefault("sb", []).append(s_b)
                    att = _gqa_attention(qa, kd, vd, None, None, tq=st["seq_len"], **st)
                xs[latent], hs[latent] = _out_proj(xs[latent], mods, l, [ret, att], w_out_parts, g_ffn,
                                                   seq_len=st["seq_len"], latent=latent)
        else:
            o = l // 2
            w_in = odd_w_in[o].astype(BF16)
            w_out = odd_w_out[o].astype(BF16)
            for latent in (False, True):
                st = streams[latent]
                res = _in_proj_odd(xs[latent], mods, l, g_mix, w_in, latent=latent, **st)
                q, k, v = res[:3]
                if latent:
                    att = _na_attention(q, k, v, _token_major(cache_na_k[:, o]), _token_major(cache_na_v[:, o]),
                                        _na_bias_table(na_rpb[o]), **st)
                else:
                    outs.setdefault("nk", []).append(res[3])
                    outs.setdefault("nv", []).append(res[4])
                    att = _dense_attention_ctx(q, k, v, **st)
                xs[latent], hs[latent] = _out_proj(xs[latent], mods, l, [att], [w_out], g_ffn,
                                                   seq_len=st["seq_len"], latent=latent)
        final_g = norm_final[None, :] if l == depth - 1 else None
        for latent in (False, True):
            xs[latent] = _conv_ffn(xs[latent], hs[latent], mods, l, *ffn_w, final_g,
                                   seq_len=streams[latent]["seq_len"], latent=latent)

    cat = lambda name: outs[name][0] if len(outs[name]) == 1 else jnp.concatenate(outs[name], axis=1)
    return (xs[False].reshape(nb_c, len_c, D_MODEL), xs[True].reshape(nb_s, len_s, D_MODEL),
            cat("sf"), cat("sb"), cat("gk"), cat("gv"), cat("nk"), cat("nv"))
```

```python
import functools

import numpy as np
import jax
import jax.numpy as jnp
from jax import lax
from jax.experimental import pallas as pl
from jax.experimental.pallas import tpu as pltpu

F32 = jnp.float32
BF16 = jnp.bfloat16

D_MODEL = 1024
GRID_W = 64
HEAD_DIM = 64
ROPE_BASE = 10000.0
EPS = 1e-6
GN_EPS = 1e-5
RET_HEADS = 8
RET_DK = 64
RET_DV = 128
RET_CHUNK = 128
GQA_HEADS = 8
GQA_KV_HEADS = 2
NA_HEADS = 16
NA_KH = 8
NA_KW = 16
D_FF = 2816
LANES = 128
MXU_DIM = 256
FF_CHUNK = MXU_DIM
N_FF_CHUNKS = D_FF // FF_CHUNK
NEG = -1e30
LOG2E = 1.4426950408889634
VMEM_LIMIT = 56 * 1024 * 1024


def _nn(a, b):
    return jnp.dot(a, b, preferred_element_type=F32)


def _nt(a, b):
    return lax.dot_general(a, b, (((1,), (1,)), ((), ())), preferred_element_type=F32)


def _tn(a, b):
    return lax.dot_general(a, b, (((0,), (0,)), ((), ())), preferred_element_type=F32)


def _sigmoid(x):
    return 1.0 / (1.0 + jnp.exp(-x))


def _params(n_axes):
    return pltpu.CompilerParams(dimension_semantics=("arbitrary",) * n_axes, vmem_limit_bytes=VMEM_LIMIT)


def _modulated_norm(x, g, scale, shift):
    ms = jnp.mean(x * x, axis=-1, keepdims=True)
    return (x * lax.rsqrt(ms + EPS) * g) * (1.0 + scale) + shift


def _lane_lo(rows):
    return lax.broadcasted_iota(jnp.int32, (rows, LANES), 1) < HEAD_DIM


def _head_rms_norm(xb, gain, lo):
    sq = xb * xb
    s_lo = jnp.sum(jnp.where(lo, sq, 0.0), axis=-1, keepdims=True)
    s_hi = jnp.sum(jnp.where(lo, 0.0, sq), axis=-1, keepdims=True)
    r = jnp.where(lo, lax.rsqrt(s_lo * (1.0 / HEAD_DIM) + EPS), lax.rsqrt(s_hi * (1.0 / HEAD_DIM) + EPS))
    return xb * r * gain


def _rope(xb, cos, sin_signed, first16):
    partner = jnp.where(first16, pltpu.roll(xb, LANES - 16, 1), pltpu.roll(xb, 16, 1))
    return xb * cos + partner * sin_signed


def _softmax_apply(scores, values):
    m = functools.reduce(jnp.maximum, [jnp.max(s, axis=-1, keepdims=True) for s in scores])
    es = [jnp.exp2(s - m) for s in scores]
    l = functools.reduce(jnp.add, [jnp.sum(e, axis=-1, keepdims=True) for e in es])
    o = None
    for e, v in zip(es, values):
        t = _nt(e.astype(BF16), v[1]) if isinstance(v, tuple) else _nn(e.astype(BF16), v)
        o = t if o is None else o + t
    return o * (1.0 / l)


def _run_pipelined(units, scores_of, finish):
    pending = scores_of(units[0])
    for n, unit in enumerate(units):
        nxt = scores_of(units[n + 1]) if n + 1 < len(units) else None
        finish(unit, pending)
        pending = nxt


def _masked_half(kp, half):
    lo = _lane_lo(kp.shape[0])
    return jnp.where(lo if half == 0 else jnp.logical_not(lo), kp, jnp.zeros_like(kp))


def _ada_kernel(c_ref, w_ref, b_ref, o_ref):
    c = c_ref[...]
    a = (c * _sigmoid(c)).astype(BF16)
    o_ref[...] = _nn(a, w_ref[...].astype(BF16)) + b_ref[...]


def _ada_params(cond, ada_w, ada_b):
    depth = ada_w.shape[0]
    rows = cond.shape[0]
    tn = 1024
    return pl.pallas_call(
        _ada_kernel,
        out_shape=jax.ShapeDtypeStruct((depth, rows, 6 * D_MODEL), F32),
        grid=(depth, 6 * D_MODEL // tn),
        in_specs=[
            pl.BlockSpec((rows, D_MODEL), lambda l, j: (0, 0)),
            pl.BlockSpec((None, D_MODEL, tn), lambda l, j: (l, 0, j)),
            pl.BlockSpec((None, 1, tn), lambda l, j: (l, 0, j)),
        ],
        out_specs=pl.BlockSpec((None, rows, tn), lambda l, j: (l, 0, j)),
        compiler_params=_params(2),
        name="ada_params",
    )(cond, ada_w, ada_b.reshape(depth, 1, 6 * D_MODEL))


def _mod_spec(layer, which, bidx):
    return pl.BlockSpec((None, None, None, 1, D_MODEL), lambda i: (layer, bidx(i), which, 0, 0))


def _batch_index_fn(latent, rows_per_tile, seq_len):
    if not latent:
        return lambda i: 0
    return lambda i: 1 + (i * rows_per_tile) // seq_len


def _in_even_kernel(*refs, latent, tm):
    if latent:
        (x_ref, shift_ref, scale_ref, g_ref, w_ref, qg_ref, kg_ref, cos_ref, sin_ref,
         qr_ref, kr_ref, vr_ref, gr_ref, qa_ref, kd_ref, vd_ref) = refs
    else:
        (x_ref, shift_ref, scale_ref, g_ref, w_ref, qg_ref, kg_ref,
         qr_ref, kr_ref, vr_ref, gr_ref, qa_ref, kd_ref, vd_ref, ck_ref, cv_ref) = refs
    hb = _modulated_norm(x_ref[...], g_ref[...], scale_ref[...], shift_ref[...]).astype(BF16)
    lane = lax.broadcasted_iota(jnp.int32, (tm, LANES), 1)
    lo = lane < HEAD_DIM
    if latent:
        cos = cos_ref[...]
        sin = sin_ref[...]
        first16 = (lane % 32) < 16
        rope = lambda v: _rope(v, cos, sin, first16)
    else:
        rope = lambda v: v
    dk_scale = RET_DK ** -0.5
    q_scale = HEAD_DIM ** -0.5 * LOG2E

    r = _nn(hb, w_ref[:, 3072:3584])
    qg = qg_ref[...]
    for b in range(4):
        blk = rope(_head_rms_norm(r[:, b * LANES:(b + 1) * LANES], qg, lo)) * q_scale
        qa_ref[:, b * LANES:(b + 1) * LANES] = blk.astype(BF16)
    r = _nn(hb, w_ref[:, 3584:3840])
    kn = _head_rms_norm(r[:, 0:LANES], kg_ref[...], lo)
    vn = r[:, LANES:2 * LANES]
    if not latent:
        seq = ck_ref.shape[4]
        for bb in range(tm // seq):
            for src, dst in ((kn, ck_ref), (vn, cv_ref)):
                t = src[bb * seq:(bb + 1) * seq, :].T
                for kv in range(GQA_KV_HEADS):
                    dst[bb, 0, kv] = t[kv * HEAD_DIM:(kv + 1) * HEAD_DIM]
    kn = rope(kn)
    for src, dst in ((kn, kd_ref), (vn, vd_ref)):
        sw = pltpu.roll(src, HEAD_DIM, 1)
        dst[:, 0:LANES] = jnp.where(lo, src, sw).astype(BF16)
        dst[:, LANES:2 * LANES] = jnp.where(lo, sw, src).astype(BF16)
    r = _nn(hb, w_ref[:, 0:512])
    for b in range(4):
        qr_ref[:, b * LANES:(b + 1) * LANES] = rope(r[:, b * LANES:(b + 1) * LANES])
    r = _nn(hb, w_ref[:, 512:1024]) * dk_scale
    for b in range(4):
        kr_ref[:, b * LANES:(b + 1) * LANES] = rope(r[:, b * LANES:(b + 1) * LANES])
    for c in range(2):
        r = _nn(hb, w_ref[:, 2048 + c * 512:2560 + c * 512])
        gr_ref[:, c * 512:(c + 1) * 512] = r * _sigmoid(r)
    for c in range(2):
        vr_ref[:, c * 512:(c + 1) * 512] = _nn(hb, w_ref[:, 1024 + c * 512:1536 + c * 512]).astype(BF16)


def _in_proj_even(x2d, mods, layer, norm_g, w_bf, q_gain2, k_gain2, rope_tabs, *, n_batch, seq_len, latent):
    n = x2d.shape[0]
    tm = 512
    bidx = _batch_index_fn(latent, tm, seq_len)
    row = lambda i: (i, 0)
    const = lambda i: (0, 0)
    in_specs = [
        pl.BlockSpec((tm, D_MODEL), row),
        _mod_spec(layer, 0, bidx),
        _mod_spec(layer, 1, bidx),
        pl.BlockSpec((1, D_MODEL), const),
        pl.BlockSpec(w_bf.shape, const),
        pl.BlockSpec((1, LANES), const),
        pl.BlockSpec((1, LANES), const),
    ]
    args = [x2d, mods, mods, norm_g, w_bf, q_gain2, k_gain2]
    if latent:
        tiles_per_seq = seq_len // tm
        in_specs += [pl.BlockSpec((tm, LANES), lambda i: (i % tiles_per_seq, 0))] * 2
        args += list(rope_tabs)
    out_shape = [
        jax.ShapeDtypeStruct((n, 512), F32),
        jax.ShapeDtypeStruct((n, 512), F32),
        jax.ShapeDtypeStruct((n, 1024), BF16),
        jax.ShapeDtypeStruct((n, 1024), F32),
        jax.ShapeDtypeStruct((n, 512), BF16),
        jax.ShapeDtypeStruct((n, 256), BF16),
        jax.ShapeDtypeStruct((n, 256), BF16),
    ]
    out_specs = [pl.BlockSpec((tm, s.shape[1]), row) for s in out_shape]
    if not latent:
        cache = jax.ShapeDtypeStruct((n_batch, 1, GQA_KV_HEADS, HEAD_DIM, seq_len), F32)
        out_shape += [cache, cache]
        out_specs += [pl.BlockSpec((tm // seq_len, 1, GQA_KV_HEADS, HEAD_DIM, seq_len),
                                   lambda i: (i, 0, 0, 0, 0))] * 2
    return pl.pallas_call(
        functools.partial(_in_even_kernel, latent=latent, tm=tm),
        out_shape=out_shape,
        grid=(n // tm,),
        in_specs=in_specs,
        out_specs=out_specs,
        compiler_params=_params(1),
        name="in_proj_even_latent" if latent else "in_proj_even_ctx",
    )(*args)


def _retention_kernel(*refs, n, nb, has_state, write_state):
    lg_ref, q_ref, k_ref, v_ref, gr_ref, gn_ref = refs[:6]
    refs = refs[6:]
    if has_state:
        s0f_ref, s0b_ref = refs[:2]
        refs = refs[2:]
    o_ref = refs[0]
    if write_state:
        sf_ref, sb_ref = refs[1:3]
    c = RET_CHUNK
    assert c == LANES and 2 * RET_DK == LANES
    nc = n // c
    p = pl.program_id(0)
    lgf = [lg_ref[0, 2 * p + hh] for hh in range(2)]
    lgb = [lg_ref[1, 2 * p + hh] for hh in range(2)]
    row = lax.broadcasted_iota(jnp.int32, (c, c), 0)
    col = lax.broadcasted_iota(jnp.int32, (c, c), 1)
    diff = (row - col).astype(F32)
    pos = row.astype(F32)
    lane_lo = col < RET_DK

    def both_scans(f, b):
        return (jnp.where(diff >= 0, jnp.exp(f * jnp.maximum(diff, 0.0)), 0.0)
                + jnp.where(diff <= 0, jnp.exp(b * jnp.maximum(-diff, 0.0)), 0.0))

    decay2 = jnp.concatenate([both_scans(lgf[0], lgb[0]), both_scans(lgf[1], lgb[1])], axis=1)
    lgf_lane = jnp.where(lane_lo, lgf[0], lgf[1])
    lgb_lane = jnp.where(lane_lo, lgb[0], lgb[1])
    qd_f = jnp.exp(lgf_lane * (pos + 1.0))
    kd_f = jnp.exp(lgf_lane * (c - 1.0 - pos))
    qd_b = jnp.exp(lgb_lane * (c - pos))
    kd_b = jnp.exp(lgb_lane * pos)
    srow = lax.broadcasted_iota(jnp.int32, (c, 2 * RET_DV), 0)
    scol = lax.broadcasted_iota(jnp.int32, (c, 2 * RET_DV), 1)
    row_a = srow < RET_DK
    col_a = scol < RET_DV
    own = row_a == col_a
    cd_f = jnp.exp(jnp.where(row_a, lgf[0], lgf[1]) * float(c))
    cd_b = jnp.exp(jnp.where(row_a, lgb[0], lgb[1]) * float(c))
    zeros_half = jnp.zeros((RET_DK, RET_DV), F32)
    gn = gn_ref[...]

    def place(s0_ref, bi):
        top = jnp.concatenate([s0_ref[bi, 0, 0], zeros_half], axis=1)
        bot = jnp.concatenate([zeros_half, s0_ref[bi, 0, 1]], axis=1)
        return jnp.concatenate([top, bot], axis=0)

    for bi in range(nb):
        rows = [slice(bi * n + i * c, bi * n + (i + 1) * c) for i in range(nc)]
        if has_state:
            s_f = place(s0f_ref, bi)
            s_b = place(s0b_ref, bi)
        else:
            s_f = jnp.zeros((c, 2 * RET_DV), F32)
            s_b = jnp.zeros((c, 2 * RET_DV), F32)
        before_f = []
        for i in range(nc):
            before_f.append(s_f)
            kv = _tn((k_ref[rows[i], :] * kd_f).astype(BF16), v_ref[rows[i], :])
            s_f = s_f * cd_f + jnp.where(own, kv, 0.0)
        before_b = [None] * nc
        for i in reversed(range(nc)):
            before_b[i] = s_b
            kv = _tn((k_ref[rows[i], :] * kd_b).astype(BF16), v_ref[rows[i], :])
            s_b = s_b * cd_b + jnp.where(own, kv, 0.0)
        if write_state:
            for hh in range(2):
                blk = (slice(hh * RET_DK, (hh + 1) * RET_DK), slice(hh * RET_DV, (hh + 1) * RET_DV))
                sf_ref[bi, 0, hh] = s_f[blk]
                sb_ref[bi, 0, hh] = s_b[blk]
        for i in range(nc):
            qc = q_ref[rows[i], :]
            kc = k_ref[rows[i], :]
            vc = v_ref[rows[i], :]
            k_cat = jnp.concatenate([jnp.where(lane_lo, kc, 0.0), jnp.where(lane_lo, 0.0, kc)], axis=0)
            v_blk = jnp.concatenate([jnp.where(col_a, vc, jnp.zeros_like(vc)),
                                     jnp.where(col_a, jnp.zeros_like(vc), vc)], axis=0)
            scores = _nt(qc.astype(BF16), k_cat.astype(BF16)) * decay2
            q_cat = jnp.concatenate([(qc * qd_f).astype(BF16), (qc * qd_b).astype(BF16)], axis=1)
            s_cat = jnp.concatenate([before_f[i], before_b[i]], axis=0).astype(BF16)
            o = _nn(scores.astype(BF16), v_blk) + _nn(q_cat, s_cat)
            for hh in range(2):
                vcols = slice(hh * RET_DV, (hh + 1) * RET_DV)
                oh = o[:, vcols]
                mu = jnp.mean(oh, axis=-1, keepdims=True)
                d = oh - mu
                var = jnp.mean(d * d, axis=-1, keepdims=True)
                y = d * lax.rsqrt(var + GN_EPS) * gn[:, vcols] * gr_ref[rows[i], vcols]
                o_ref[rows[i], vcols] = y.astype(BF16)


def _retention(log_g, qr, kr, vr, gr, gn, state_f, state_b, *, n_batch, seq_len, write_state):
    n = qr.shape[0]
    pairs = RET_HEADS // 2
    has_state = state_f is not None
    nb = max(1, 1024 // seq_len)
    rows = nb * seq_len
    tok = lambda p, g: (g, p)
    in_specs = [
        pl.BlockSpec(memory_space=pltpu.SMEM),
        pl.BlockSpec((rows, LANES), tok),
        pl.BlockSpec((rows, LANES), tok),
        pl.BlockSpec((rows, 2 * RET_DV), tok),
        pl.BlockSpec((rows, 2 * RET_DV), tok),
        pl.BlockSpec((1, 2 * RET_DV), lambda p, g: (0, p)),
    ]
    args = [log_g, qr, kr, vr, gr, gn]
    state_spec = pl.BlockSpec((nb, 1, 2, RET_DK, RET_DV), lambda p, g: (g, 0, p, 0, 0))
    if has_state:
        in_specs += [state_spec, state_spec]
        args += [state_f, state_b]
    out_shape = [jax.ShapeDtypeStruct((n, RET_HEADS * RET_DV), BF16)]
    out_specs = [pl.BlockSpec((rows, 2 * RET_DV), tok)]
    if write_state:
        st = jax.ShapeDtypeStruct((n_batch, 1, RET_HEADS, RET_DK, RET_DV), F32)
        out_shape += [st, st]
        out_specs += [state_spec, state_spec]
    return pl.pallas_call(
        functools.partial(_retention_kernel, n=seq_len, nb=nb, has_state=has_state, write_state=write_state),
        out_shape=out_shape,
        grid=(pairs, n_batch // nb),
        in_specs=in_specs,
        out_specs=out_specs,
        compiler_params=_params(2),
        name="retention_latent" if has_state else "retention_ctx",
    )(*args)


def _gqa_kernel(*refs, n_src, tq):
    q_ref = refs[0]
    k_refs = refs[1:1 + 2 * n_src:2]
    v_refs = refs[2:2 + 2 * n_src:2]
    o_ref = refs[1 + 2 * n_src]
    units = [(g, half) for g in range(GQA_KV_HEADS) for half in range(2)]
    outs = {}

    def scores_of(unit):
        g, half = unit
        base = g * 2 * LANES
        q = jnp.concatenate([q_ref[:, base:base + LANES], q_ref[:, base + LANES:base + 2 * LANES]], axis=0)
        return [_nt(q, _masked_half(k_ref[:, g * LANES:(g + 1) * LANES], half)) for k_ref in k_refs]

    def finish(unit, scores):
        g, half = unit
        outs[half] = _softmax_apply(scores, [v_ref[:, g * LANES:(g + 1) * LANES] for v_ref in v_refs])
        if half == 1:
            base = g * 2 * LANES
            o = jnp.where(_lane_lo(2 * tq), outs[0], outs[1]).astype(BF16)
            o_ref[:, base:base + LANES] = o[:tq]
            o_ref[:, base + LANES:base + 2 * LANES] = o[tq:]

    _run_pipelined(units, scores_of, finish)


def _gqa_attention(qa, kd, vd, ctx_kd, ctx_vd, *, n_batch, seq_len, tq):
    n = qa.shape[0]
    tiles = seq_len // tq
    n_src = 1 if ctx_kd is None else 2
    qmap = lambda b, t: (b * tiles + t, 0)
    kmap = lambda b, t: (b, 0)
    in_specs = [pl.BlockSpec((tq, GQA_HEADS * HEAD_DIM), qmap),
                pl.BlockSpec((seq_len, 2 * LANES), kmap),
                pl.BlockSpec((seq_len, 2 * LANES), kmap)]
    args = [qa, kd, vd]
    if n_src == 2:
        past = ctx_kd.shape[1]
        cmap = lambda b, t: (b, 0, 0)
        in_specs += [pl.BlockSpec((None, past, 2 * LANES), cmap)] * 2
        args += [ctx_kd, ctx_vd]
    return pl.pallas_call(
        functools.partial(_gqa_kernel, n_src=n_src, tq=tq),
        out_shape=jax.ShapeDtypeStruct((n, GQA_HEADS * HEAD_DIM), BF16),
        grid=(n_batch, tiles),
        in_specs=in_specs,
        out_specs=pl.BlockSpec((tq, GQA_HEADS * HEAD_DIM), qmap),
        compiler_params=_params(2),
        name="gqa_latent" if n_src == 2 else "gqa_ctx",
    )(*args)


def _out_proj_kernel(*refs, n_mix):
    x_ref, gate_ref, g_ref, scale_ref, shift_ref = refs[:5]
    m_refs = refs[5:5 + n_mix]
    w_refs = refs[5 + n_mix:5 + 2 * n_mix]
    o_ref, h_ref = refs[5 + 2 * n_mix:]
    acc = None
    for m_ref, w_ref in zip(m_refs, w_refs):
        t = _nn(m_ref[...], w_ref[...])
        acc = t if acc is None else acc + t
    y = x_ref[...] + gate_ref[...] * acc
    o_ref[...] = y
    h_ref[...] = _modulated_norm(y, g_ref[...], scale_ref[...], shift_ref[...]).astype(BF16)


def _out_proj(x2d, mods, layer, mixes, weights, ffn_norm_g, *, seq_len, latent):
    n = x2d.shape[0]
    tm = 512
    bidx = _batch_index_fn(latent, tm, seq_len)
    row = lambda i: (i, 0)
    const = lambda i: (0, 0)
    in_specs = [pl.BlockSpec((tm, D_MODEL), row), _mod_spec(layer, 2, bidx), pl.BlockSpec((1, D_MODEL), const),
                _mod_spec(layer, 4, bidx), _mod_spec(layer, 3, bidx)]
    in_specs += [pl.BlockSpec((tm, m.shape[1]), row) for m in mixes]
    in_specs += [pl.BlockSpec(w.shape, const) for w in weights]
    return pl.pallas_call(
        functools.partial(_out_proj_kernel, n_mix=len(mixes)),
        out_shape=[jax.ShapeDtypeStruct((n, D_MODEL), F32), jax.ShapeDtypeStruct((n, D_MODEL), BF16)],
        grid=(n // tm,),
        in_specs=in_specs,
        out_specs=[pl.BlockSpec((tm, D_MODEL), row), pl.BlockSpec((tm, D_MODEL), row)],
        compiler_params=_params(1),
        name="out_proj_latent" if latent else "out_proj_ctx",
    )(x2d, mods, ffn_norm_g, mods, mods, *mixes, *weights)


def _zero_rows(arr, rows):
    pieces, cur = [], 0
    sub = lax.broadcasted_iota(jnp.int32, (8, arr.shape[1]), 0)
    for r in sorted(rows):
        g0 = (r // 8) * 8
        if g0 > cur:
            pieces.append(arr[cur:g0])
        pieces.append(jnp.where(sub == r - g0, 0.0, arr[g0:g0 + 8]))
        cur = g0 + 8
    if cur < arr.shape[0]:
        pieces.append(arr[cur:])
    return jnp.concatenate(pieces, axis=0)


def _ffn_kernel(*refs, tm, seq_len, final):
    x_ref, hb_ref, gate_ref, wup_ref, cw_ref, cb_ref, wd_ref = refs[:7]
    refs = refs[7:]
    if final:
        gfin_ref = refs[0]
        refs = refs[1:]
    o_ref, act_ref, hbs_ref = refs
    hbs_ref[...] = hb_ref[...]
    seq_starts = list(range(0, tm, seq_len))
    seq_ends = [s + seq_len - 1 for s in seq_starts]
    for j in range(N_FF_CHUNKS):
        u = _nn(hbs_ref[...], wup_ref[j])
        cw = cw_ref[j]
        prev = _zero_rows(pltpu.roll(u, 1, 0), seq_starts)
        nxt = _zero_rows(pltpu.roll(u, tm - 1, 0), seq_ends)
        cv = prev * cw[0:1] + u * cw[1:2] + nxt * cw[2:3] + cb_ref[j]
        a = cv[:, :FF_CHUNK]
        act_ref[:, j * FF_CHUNK:(j + 1) * FF_CHUNK] = (a * _sigmoid(a) * cv[:, FF_CHUNK:]).astype(BF16)
    for rb in range(2):
        rows = slice(rb * (tm // 2), (rb + 1) * (tm // 2))
        y = x_ref[rows, :] + gate_ref[...] * _nn(act_ref[rows, :], wd_ref[...])
        if final:
            ms = jnp.mean(y * y, axis=-1, keepdims=True)
            y = y * lax.rsqrt(ms + EPS) * gfin_ref[...]
        o_ref[rows, :] = y


def _conv_ffn(x2d, hb2d, mods, layer, wup_c, cw_c, cb_c, wd_c, final_g, *, seq_len, latent):
    n = x2d.shape[0]
    tm = 1024
    bidx = _batch_index_fn(latent, tm, seq_len)
    row = lambda i: (i, 0)
    const2 = lambda i: (0, 0)
    resident = lambda a: pl.BlockSpec((None,) + a.shape[1:], lambda i: (layer,) + (0,) * (a.ndim - 1),
                                      pipeline_mode=pl.Buffered(1))
    in_specs = [pl.BlockSpec((tm, D_MODEL), row), pl.BlockSpec((tm, D_MODEL), row), _mod_spec(layer, 5, bidx),
                resident(wup_c), resident(cw_c), resident(cb_c), resident(wd_c)]
    args = [x2d, hb2d, mods, wup_c, cw_c, cb_c, wd_c]
    final = final_g is not None
    if final:
        in_specs.append(pl.BlockSpec((1, D_MODEL), const2))
        args.append(final_g)
    return pl.pallas_call(
        functools.partial(_ffn_kernel, tm=tm, seq_len=seq_len, final=final),
        out_shape=jax.ShapeDtypeStruct((n, D_MODEL), F32),
        grid=(n // tm,),
        in_specs=in_specs,
        out_specs=pl.BlockSpec((tm, D_MODEL), row),
        scratch_shapes=[pltpu.VMEM((tm, D_FF), BF16), pltpu.VMEM((tm, D_MODEL), BF16)],
        compiler_params=_params(1),
        name="conv_ffn_latent" if latent else "conv_ffn_ctx",
    )(*args)


def _regroup_up_kernel(a_ref, g_ref, o_ref):
    o_ref[:, :FF_CHUNK] = a_ref[...].astype(BF16)
    o_ref[:, FF_CHUNK:] = g_ref[...].astype(BF16)


def _regroup_w_up(w_up):
    depth = w_up.shape[0]
    return pl.pallas_call(
        _regroup_up_kernel,
        out_shape=jax.ShapeDtypeStruct((depth, N_FF_CHUNKS, D_MODEL, 2 * FF_CHUNK), BF16),
        grid=(depth, N_FF_CHUNKS),
        in_specs=[pl.BlockSpec((None, D_MODEL, FF_CHUNK), lambda l, j: (l, 0, j)),
                  pl.BlockSpec((None, D_MODEL, FF_CHUNK), lambda l, j: (l, 0, N_FF_CHUNKS + j))],
        out_specs=pl.BlockSpec((None, None, D_MODEL, 2 * FF_CHUNK), lambda l, j: (l, j, 0, 0)),
        compiler_params=_params(2),
        name="regroup_w_up",
    )(w_up, w_up)


def _in_odd_kernel(*refs, write_cache, tm):
    x_ref, shift_ref, scale_ref, g_ref, w_ref, q_ref, k_ref, v_ref = refs[:8]
    hb = _modulated_norm(x_ref[...], g_ref[...], scale_ref[...], shift_ref[...]).astype(BF16)
    width = NA_HEADS * HEAD_DIM
    q_scale = HEAD_DIM ** -0.5 * LOG2E
    for which, dst in ((1, k_ref), (2, v_ref)):
        for c in range(2):
            r = _nn(hb, w_ref[:, which * width + c * 512:which * width + (c + 1) * 512])
            dst[:, c * 512:(c + 1) * 512] = r.astype(BF16)
            if write_cache:
                cache_ref = refs[8 + which - 1]
                seq = cache_ref.shape[4]
                for bb in range(tm // seq):
                    for blk in range(512 // LANES):
                        t = r[bb * seq:(bb + 1) * seq, blk * LANES:(blk + 1) * LANES].T
                        for hh in range(2):
                            head = c * (512 // HEAD_DIM) + 2 * blk + hh
                            cache_ref[bb, 0, head] = t[hh * HEAD_DIM:(hh + 1) * HEAD_DIM]
    for c in range(2):
        cols = slice(c * 512, (c + 1) * 512)
        q_ref[:, cols] = (_nn(hb, w_ref[:, c * 512:(c + 1) * 512]) * q_scale).astype(BF16)


def _in_proj_odd(x2d, mods, layer, norm_g, w_bf, *, n_batch, seq_len, latent):
    n = x2d.shape[0]
    tm = 512
    width = NA_HEADS * HEAD_DIM
    bidx = _batch_index_fn(latent, tm, seq_len)
    row = lambda i: (i, 0)
    const = lambda i: (0, 0)
    in_specs = [pl.BlockSpec((tm, D_MODEL), row), _mod_spec(layer, 0, bidx), _mod_spec(layer, 1, bidx),
                pl.BlockSpec((1, D_MODEL), const), pl.BlockSpec(w_bf.shape, const)]
    out_shape = [jax.ShapeDtypeStruct((n, width), BF16)] * 3
    out_specs = [pl.BlockSpec((tm, width), row)] * 3
    write_cache = not latent
    if write_cache:
        cache = jax.ShapeDtypeStruct((n_batch, 1, NA_HEADS, HEAD_DIM, seq_len), F32)
        out_shape += [cache, cache]
        out_specs += [pl.BlockSpec((tm // seq_len, 1, NA_HEADS, HEAD_DIM, seq_len),
                                   lambda i: (i, 0, 0, 0, 0))] * 2
    return pl.pallas_call(
        functools.partial(_in_odd_kernel, write_cache=write_cache, tm=tm),
        out_shape=out_shape,
        grid=(n // tm,),
        in_specs=in_specs,
        out_specs=out_specs,
        compiler_params=_params(1),
        name="in_proj_odd_latent" if latent else "in_proj_odd_ctx",
    )(x2d, mods, mods, norm_g, w_bf)


def _dense_pairs_kernel(q_ref, k_ref, v_ref, o_ref):
    units = [(p, half) for p in range(NA_HEADS // 2) for half in range(2)]
    outs = {}

    def scores_of(unit):
        p, half = unit
        cols = slice(p * LANES, (p + 1) * LANES)
        return [_nt(q_ref[:, cols], _masked_half(k_ref[:, cols], half))]

    def finish(unit, scores):
        p, half = unit
        cols = slice(p * LANES, (p + 1) * LANES)
        outs[half] = _softmax_apply(scores, [v_ref[:, cols]])
        if half == 1:
            o_ref[:, cols] = jnp.where(_lane_lo(q_ref.shape[0]), outs[0], outs[1]).astype(BF16)

    _run_pipelined(units, scores_of, finish)


def _dense_attention_ctx(q, k, v, *, n_batch, seq_len):
    width = NA_HEADS * HEAD_DIM
    spec = pl.BlockSpec((seq_len, width), lambda b: (b, 0))
    return pl.pallas_call(
        _dense_pairs_kernel,
        out_shape=jax.ShapeDtypeStruct(q.shape, BF16),
        grid=(n_batch,),
        in_specs=[spec, spec, spec],
        out_specs=spec,
        compiler_params=_params(1),
        name="dense_attention_ctx",
    )(q, k, v)


NA_Q_ROWS = 8
NA_KEY_ROWS = 12


def _na_window_start(r):
    return min(max(r - NA_KH // 2, 0), (1024 // GRID_W) - NA_KH)


def _na_bias_tile(bias_ref, half, tile, n_rows):
    r0 = tile * NA_Q_ROWS
    kr0 = min(max(r0 - NA_KH // 2, 0), n_rows - NA_KEY_ROWS)
    lo = _lane_lo(GRID_W)
    neg_block = jnp.full((GRID_W, LANES), NEG, F32)
    left_off = jnp.where(lo, NEG, 0.0)
    right_off = jnp.where(lo, 0.0, NEG)
    rows = []
    for rq in range(NA_Q_ROWS):
        r = r0 + rq
        rs = _na_window_start(r)
        blocks = []
        for kk in range(NA_KEY_ROWS // 2):
            ka = kr0 + 2 * kk
            va = rs <= ka < rs + NA_KH
            vb = rs <= ka + 1 < rs + NA_KH
            if not (va or vb):
                blocks.append(neg_block)
                continue
            blk = bias_ref[half, ka - r + NA_KH]
            if not va:
                blk = blk + left_off
            if not vb:
                blk = blk + right_off
            blocks.append(blk)
        rows.append(jnp.concatenate(blocks, axis=1))
    return jnp.concatenate(rows, axis=0), kr0


def _na_kernel(q_ref, k_ref, v_ref, ck_ref, cv_ref, bias_ref, o_ref, *, n_rows):
    tq = NA_Q_ROWS * GRID_W
    span = NA_KEY_ROWS * GRID_W
    units = [(tile, half) for tile in range(n_rows // NA_Q_ROWS) for half in range(2)]
    outs = {}

    def window(tile):
        kr0 = min(max(tile * NA_Q_ROWS - NA_KH // 2, 0), n_rows - NA_KEY_ROWS)
        return slice(kr0 * GRID_W, kr0 * GRID_W + span)

    def scores_of(unit):
        tile, half = unit
        q = q_ref[tile * tq:(tile + 1) * tq, :]
        bias, _ = _na_bias_tile(bias_ref, half, tile, n_rows)
        ck_t = jnp.concatenate([ck_ref[0], ck_ref[1]], axis=0)
        own = (lax.broadcasted_iota(jnp.int32, ck_t.shape, 0) < HEAD_DIM) == (half == 0)
        return [_nt(q, _masked_half(k_ref[window(tile), :], half)) + bias,
                _nn(q, jnp.where(own, ck_t, 0.0).astype(BF16))]

    def finish(unit, scores):
        tile, half = unit
        cv_t = jnp.concatenate([cv_ref[0], cv_ref[1]], axis=0).astype(BF16)
        outs[half] = _softmax_apply(scores, [v_ref[window(tile), :], ("t", cv_t)])
        if half == 1:
            o_ref[tile * tq:(tile + 1) * tq, :] = jnp.where(_lane_lo(tq), outs[0], outs[1]).astype(BF16)

    _run_pipelined(units, scores_of, finish)


def _na_bias_table(rpb):
    cidx = np.arange(GRID_W)
    cs = np.clip(cidx - NA_KW // 2, 0, GRID_W - NA_KW)
    kc = np.arange(GRID_W)
    inside = (kc[None, :] >= cs[:, None]) & (kc[None, :] < cs[:, None] + NA_KW)
    rel = kc[None, :] - cidx[:, None] + NA_KW - 1
    onehot = (rel[None] == np.arange(2 * NA_KW - 1)[:, None, None]) & inside[None]
    m = jnp.einsum("hdj,jck->hdck", rpb * LOG2E, jnp.asarray(onehot, F32), precision=lax.Precision.HIGHEST)
    m = jnp.where(jnp.asarray(inside)[None, None], m, NEG)
    neg = jnp.full((rpb.shape[0], 1, GRID_W, GRID_W), NEG, F32)
    left = jnp.concatenate([neg, m], axis=1)
    right = jnp.concatenate([m, neg], axis=1)
    return jnp.concatenate([left, right], axis=-1)


def _na_attention(q, k, v, ctx_k, ctx_v, bias_tab, *, n_batch, seq_len):
    pairs = NA_HEADS // 2
    tok = lambda p, b: (b, p)
    ctx = lambda p, b: (b, p, 0, 0)
    past = ctx_k.shape[-1]
    return pl.pallas_call(
        functools.partial(_na_kernel, n_rows=seq_len // GRID_W),
        out_shape=jax.ShapeDtypeStruct(q.shape, BF16),
        grid=(pairs, n_batch),
        in_specs=[pl.BlockSpec((seq_len, LANES), tok)] * 3
        + [pl.BlockSpec((None, 2, HEAD_DIM, past), ctx)] * 2
        + [pl.BlockSpec((2, 2 * NA_KH, GRID_W, LANES), lambda p, b: (p, 0, 0, 0))],
        out_specs=pl.BlockSpec((seq_len, LANES), tok),
        compiler_params=_params(2),
        name="neighbourhood_attention",
    )(q, k, v, ctx_k, ctx_v, bias_tab)


def _rope_tables(n):
    t = jnp.arange(n)
    row = (t // GRID_W).astype(F32)
    col = (t % GRID_W).astype(F32)
    half = HEAD_DIM // 2
    inv = ROPE_BASE ** (-jnp.arange(0, half, 2, dtype=F32) / half)
    ang_r = row[:, None] * inv
    ang_c = col[:, None] * inv
    cos_h = jnp.concatenate([jnp.cos(ang_r)] * 2 + [jnp.cos(ang_c)] * 2, axis=-1)
    sin_h = jnp.concatenate([-jnp.sin(ang_r), jnp.sin(ang_r), -jnp.sin(ang_c), jnp.sin(ang_c)], axis=-1)
    return jnp.concatenate([cos_h, cos_h], axis=-1), jnp.concatenate([sin_h, sin_h], axis=-1)


def _ffn_weights(w_up, conv_w, conv_b, w_down):
    depth = w_up.shape[0]

    def regroup(a):
        lead = a.shape[1:-1]
        a = a.reshape(depth, *lead, 2, N_FF_CHUNKS, FF_CHUNK)
        a = jnp.moveaxis(a, -2, 1)
        return a.reshape(depth, N_FF_CHUNKS, *lead, 2 * FF_CHUNK)
    return (_regroup_w_up(w_up), regroup(conv_w), regroup(conv_b[:, None, :]),
            w_down.astype(BF16))


def _token_major_dup(cache):
    b, kv, t, d = cache.shape
    c = jnp.transpose(cache, (0, 2, 1, 3))[:, :, :, None, :]
    return jnp.broadcast_to(c, (b, t, kv, 2, d)).reshape(b, t, kv * 2 * d).astype(BF16)


def _head_transposed(cache):
    return jnp.swapaxes(cache, -1, -2)


def kernel(x_prompt, x_sample, state_ret_fwd, state_ret_bwd, cache_gqa_k, cache_gqa_v, cache_na_k, cache_na_v,
           c, c_ctx, ada_w, ada_b, norm_mix, norm_ffn, norm_final, even_w_in, even_w_out, ret_decay_fwd,
           ret_decay_bwd, ret_gn, gqa_q_norm, gqa_k_norm, odd_w_in, odd_w_out, na_rpb, ffn_w_up, ffn_conv_w,
           ffn_conv_b, ffn_w_down):
    nb_c, len_c, _ = x_prompt.shape
    nb_s, len_s, _ = x_sample.shape
    depth = ada_w.shape[0]
    streams = {
        False: dict(n_batch=nb_c, seq_len=len_c),
        True: dict(n_batch=nb_s, seq_len=len_s),
    }
    xs = {False: x_prompt.reshape(nb_c * len_c, D_MODEL), True: x_sample.reshape(nb_s * len_s, D_MODEL)}

    rows = 8 * (-(-(1 + nb_s) // 8))
    cond = jnp.zeros((rows, D_MODEL), F32).at[0].set(c_ctx).at[1:1 + nb_s].set(c)
    mods = _ada_params(cond, ada_w, ada_b).reshape(depth, rows, 6, 1, D_MODEL)
    rope_tabs = _rope_tables(len_s)
    ffn_w = _ffn_weights(ffn_w_up, ffn_conv_w, ffn_conv_b, ffn_w_down)
    outs = {}
    hs = {}

    for l in range(depth):
        g_mix = norm_mix[l][None, :]
        g_ffn = norm_ffn[l][None, :]
        if l % 2 == 0:
            e = l // 2
            w_in = even_w_in[e].astype(BF16)
            w_out = even_w_out[e].astype(BF16)
            w_out_parts = [w_out[:RET_HEADS * RET_DV], w_out[RET_HEADS * RET_DV:]]
            log_g = jnp.stack([jax.nn.log_sigmoid(ret_decay_fwd[e].astype(F32)),
                               jax.nn.log_sigmoid(ret_decay_bwd[e].astype(F32))])
            gn = ret_gn[e][None, :]
            qg2 = jnp.tile(gqa_q_norm[e], 2)[None, :]
            kg2 = jnp.tile(gqa_k_norm[e], 2)[None, :]
            for latent in (False, True):
                st = streams[latent]
                res = _in_proj_even(xs[latent], mods, l, g_mix, w_in, qg2, kg2, rope_tabs, latent=latent, **st)
                qr, kr, vr, gr, qa, kd, vd = res[:7]
                if latent:
                    ret = _retention(log_g, qr, kr, vr, gr, gn, state_ret_fwd[:, e:e + 1],
                                     state_ret_bwd[:, e:e + 1], write_state=False, **st)[0]
                    att = _gqa_attention(qa, kd, vd, _token_major_dup(cache_gqa_k[:, e]),
                                         _token_major_dup(cache_gqa_v[:, e]), tq=512, **st)
                else:
                    outs.setdefault("gk", []).append(res[7])
                    outs.setdefault("gv", []).append(res[8])
                    ret, s_f, s_b = _retention(log_g, qr, kr, vr, gr, gn, None, None, write_state=True, **st)
                    outs.setdefault("sf", []).append(s_f)
                    outs.setdefault("sb", []).append(s_b)
                    att = _gqa_attention(qa, kd, vd, None, None, tq=st["seq_len"], **st)
                xs[latent], hs[latent] = _out_proj(xs[latent], mods, l, [ret, att], w_out_parts, g_ffn,
                                                   seq_len=st["seq_len"], latent=latent)
        else:
            o = l // 2
            w_in = odd_w_in[o].astype(BF16)
            w_out = odd_w_out[o].astype(BF16)
            for latent in (False, True):
                st = streams[latent]
                res = _in_proj_odd(xs[latent], mods, l, g_mix, w_in, latent=latent, **st)
                q, k, v = res[:3]
                if latent:
                    att = _na_attention(q, k, v, _head_transposed(cache_na_k[:, o]), _head_transposed(cache_na_v[:, o]),
                                        _na_bias_table(na_rpb[o]), **st)
                else:
                    outs.setdefault("nk", []).append(res[3])
                    outs.setdefault("nv", []).append(res[4])
                    att = _dense_attention_ctx(q, k, v, **st)
                xs[latent], hs[latent] = _out_proj(xs[latent], mods, l, [att], [w_out], g_ffn,
                                                   seq_len=st["seq_len"], latent=latent)
        final_g = norm_final[None, :] if l == depth - 1 else None
        for latent in (False, True):
            xs[latent] = _conv_ffn(xs[latent], hs[latent], mods, l, *ffn_w, final_g,
                                   seq_len=streams[latent]["seq_len"], latent=latent)

    tr = lambda a: jnp.swapaxes(a, -1, -2)
    cat = lambda name: outs[name][0] if len(outs[name]) == 1 else jnp.concatenate(outs[name], axis=1)
    return (xs[False].reshape(nb_c, len_c, D_MODEL), xs[True].reshape(nb_s, len_s, D_MODEL),
            cat("sf"), cat("sb"), tr(cat("gk")), tr(cat("gv")), tr(cat("nk")), tr(cat("nv")))
```

```python
import functools

import numpy as np
import jax
import jax.numpy as jnp
from jax import lax
from jax.experimental import pallas as pl
from jax.experimental.pallas import tpu as pltpu

F32 = jnp.float32
BF16 = jnp.bfloat16

D_MODEL = 1024
GRID_W = 64
HEAD_DIM = 64
ROPE_BASE = 10000.0
EPS = 1e-6
GN_EPS = 1e-5
RET_HEADS = 8
RET_DK = 64
RET_DV = 128
RET_CHUNK = 128
GQA_HEADS = 8
GQA_KV_HEADS = 2
NA_HEADS = 16
NA_KH = 8
NA_KW = 16
D_FF = 2816
LANES = 128
MXU_DIM = 256
FF_CHUNK = MXU_DIM
N_FF_CHUNKS = D_FF // FF_CHUNK
NEG = -1e30
LOG2E = 1.4426950408889634
VMEM_LIMIT = 56 * 1024 * 1024


def _nn(a, b):
    return jnp.dot(a, b, preferred_element_type=F32)


def _nt(a, b):
    return lax.dot_general(a, b, (((1,), (1,)), ((), ())), preferred_element_type=F32)


def _tn(a, b):
    return lax.dot_general(a, b, (((0,), (0,)), ((), ())), preferred_element_type=F32)


def _sigmoid(x):
    return 1.0 / (1.0 + jnp.exp(-x))


def _params(n_axes):
    return pltpu.CompilerParams(dimension_semantics=("arbitrary",) * n_axes, vmem_limit_bytes=VMEM_LIMIT)


def _modulated_norm(x, g, scale, shift):
    ms = jnp.mean(x * x, axis=-1, keepdims=True)
    return (x * lax.rsqrt(ms + EPS) * g) * (1.0 + scale) + shift


def _lane_lo(rows):
    return lax.broadcasted_iota(jnp.int32, (rows, LANES), 1) < HEAD_DIM


def _head_rms_norm(xb, gain, lo):
    sq = xb * xb
    s_lo = jnp.sum(jnp.where(lo, sq, 0.0), axis=-1, keepdims=True)
    s_hi = jnp.sum(jnp.where(lo, 0.0, sq), axis=-1, keepdims=True)
    r = jnp.where(lo, lax.rsqrt(s_lo * (1.0 / HEAD_DIM) + EPS), lax.rsqrt(s_hi * (1.0 / HEAD_DIM) + EPS))
    return xb * r * gain


def _rope(xb, cos, sin_signed, first16):
    partner = jnp.where(first16, pltpu.roll(xb, LANES - 16, 1), pltpu.roll(xb, 16, 1))
    return xb * cos + partner * sin_signed


def _softmax_apply(scores, values):
    m = functools.reduce(jnp.maximum, [jnp.max(s, axis=-1, keepdims=True) for s in scores])
    es = [jnp.exp2(s - m) for s in scores]
    l = functools.reduce(jnp.add, [jnp.sum(e, axis=-1, keepdims=True) for e in es])
    o = None
    for e, v in zip(es, values):
        t = _nt(e.astype(BF16), v[1]) if isinstance(v, tuple) else _nn(e.astype(BF16), v)
        o = t if o is None else o + t
    return o * (1.0 / l)


def _run_pipelined(units, scores_of, finish):
    pending = scores_of(units[0])
    for n, unit in enumerate(units):
        nxt = scores_of(units[n + 1]) if n + 1 < len(units) else None
        finish(unit, pending)
        pending = nxt


def _masked_half(kp, half):
    lo = _lane_lo(kp.shape[0])
    return jnp.where(lo if half == 0 else jnp.logical_not(lo), kp, jnp.zeros_like(kp))


def _ada_kernel(c_ref, w_ref, b_ref, o_ref):
    c = c_ref[...]
    a = (c * _sigmoid(c)).astype(BF16)
    o_ref[...] = _nn(a, w_ref[...].astype(BF16)) + b_ref[...]


def _ada_params(cond, ada_w, ada_b):
    depth = ada_w.shape[0]
    rows = cond.shape[0]
    tn = 1024
    return pl.pallas_call(
        _ada_kernel,
        out_shape=jax.ShapeDtypeStruct((depth, rows, 6 * D_MODEL), F32),
        grid=(depth, 6 * D_MODEL // tn),
        in_specs=[
            pl.BlockSpec((rows, D_MODEL), lambda l, j: (0, 0)),
            pl.BlockSpec((None, D_MODEL, tn), lambda l, j: (l, 0, j)),
            pl.BlockSpec((None, 1, tn), lambda l, j: (l, 0, j)),
        ],
        out_specs=pl.BlockSpec((None, rows, tn), lambda l, j: (l, 0, j)),
        compiler_params=_params(2),
        name="ada_params",
    )(cond, ada_w, ada_b.reshape(depth, 1, 6 * D_MODEL))


def _mod_spec(layer, which, bidx):
    return pl.BlockSpec((None, None, None, 1, D_MODEL), lambda i: (layer, bidx(i), which, 0, 0))


def _batch_index_fn(latent, rows_per_tile, seq_len):
    if not latent:
        return lambda i: 0
    return lambda i: 1 + (i * rows_per_tile) // seq_len


def _in_even_kernel(*refs, latent, tm):
    if latent:
        (x_ref, shift_ref, scale_ref, g_ref, w_ref, qg_ref, kg_ref, cos_ref, sin_ref,
         qr_ref, kr_ref, vr_ref, gr_ref, qa_ref, kd_ref, vd_ref) = refs
    else:
        (x_ref, shift_ref, scale_ref, g_ref, w_ref, qg_ref, kg_ref,
         qr_ref, kr_ref, vr_ref, gr_ref, qa_ref, kd_ref, vd_ref, ck_ref, cv_ref) = refs
    hb = _modulated_norm(x_ref[...], g_ref[...], scale_ref[...], shift_ref[...]).astype(BF16)
    lane = lax.broadcasted_iota(jnp.int32, (tm, LANES), 1)
    lo = lane < HEAD_DIM
    if latent:
        cos = cos_ref[...]
        sin = sin_ref[...]
        first16 = (lane % 32) < 16
        rope = lambda v: _rope(v, cos, sin, first16)
    else:
        rope = lambda v: v
    dk_scale = RET_DK ** -0.5
    q_scale = HEAD_DIM ** -0.5 * LOG2E

    r = _nn(hb, w_ref[:, 3072:3584])
    qg = qg_ref[...]
    for b in range(4):
        blk = rope(_head_rms_norm(r[:, b * LANES:(b + 1) * LANES], qg, lo)) * q_scale
        qa_ref[:, b * LANES:(b + 1) * LANES] = blk.astype(BF16)
    r = _nn(hb, w_ref[:, 3584:3840])
    kn = _head_rms_norm(r[:, 0:LANES], kg_ref[...], lo)
    vn = r[:, LANES:2 * LANES]
    if not latent:
        seq = ck_ref.shape[4]
        for bb in range(tm // seq):
            for src, dst in ((kn, ck_ref), (vn, cv_ref)):
                t = src[bb * seq:(bb + 1) * seq, :].T
                for kv in range(GQA_KV_HEADS):
                    dst[bb, 0, kv] = t[kv * HEAD_DIM:(kv + 1) * HEAD_DIM]
    kn = rope(kn)
    for src, dst in ((kn, kd_ref), (vn, vd_ref)):
        sw = pltpu.roll(src, HEAD_DIM, 1)
        dst[:, 0:LANES] = jnp.where(lo, src, sw).astype(BF16)
        dst[:, LANES:2 * LANES] = jnp.where(lo, sw, src).astype(BF16)
    r = _nn(hb, w_ref[:, 0:512])
    for b in range(4):
        qr_ref[:, b * LANES:(b + 1) * LANES] = rope(r[:, b * LANES:(b + 1) * LANES])
    r = _nn(hb, w_ref[:, 512:1024]) * dk_scale
    for b in range(4):
        kr_ref[:, b * LANES:(b + 1) * LANES] = rope(r[:, b * LANES:(b + 1) * LANES])
    for c in range(2):
        r = _nn(hb, w_ref[:, 2048 + c * 512:2560 + c * 512])
        gr_ref[:, c * 512:(c + 1) * 512] = r * _sigmoid(r)
    for c in range(2):
        vr_ref[:, c * 512:(c + 1) * 512] = _nn(hb, w_ref[:, 1024 + c * 512:1536 + c * 512]).astype(BF16)


def _in_proj_even(x2d, mods, layer, norm_g, w_bf, q_gain2, k_gain2, rope_tabs, *, n_batch, seq_len, latent):
    n = x2d.shape[0]
    tm = 512
    bidx = _batch_index_fn(latent, tm, seq_len)
    row = lambda i: (i, 0)
    const = lambda i: (0, 0)
    in_specs = [
        pl.BlockSpec((tm, D_MODEL), row),
        _mod_spec(layer, 0, bidx),
        _mod_spec(layer, 1, bidx),
        pl.BlockSpec((1, D_MODEL), const),
        pl.BlockSpec(w_bf.shape, const),
        pl.BlockSpec((1, LANES), const),
        pl.BlockSpec((1, LANES), const),
    ]
    args = [x2d, mods, mods, norm_g, w_bf, q_gain2, k_gain2]
    if latent:
        tiles_per_seq = seq_len // tm
        in_specs += [pl.BlockSpec((tm, LANES), lambda i: (i % tiles_per_seq, 0))] * 2
        args += list(rope_tabs)
    out_shape = [
        jax.ShapeDtypeStruct((n, 512), F32),
        jax.ShapeDtypeStruct((n, 512), F32),
        jax.ShapeDtypeStruct((n, 1024), BF16),
        jax.ShapeDtypeStruct((n, 1024), F32),
        jax.ShapeDtypeStruct((n, 512), BF16),
        jax.ShapeDtypeStruct((n, 256), BF16),
        jax.ShapeDtypeStruct((n, 256), BF16),
    ]
    out_specs = [pl.BlockSpec((tm, s.shape[1]), row) for s in out_shape]
    if not latent:
        cache = jax.ShapeDtypeStruct((n_batch, 1, GQA_KV_HEADS, HEAD_DIM, seq_len), F32)
        out_shape += [cache, cache]
        out_specs += [pl.BlockSpec((tm // seq_len, 1, GQA_KV_HEADS, HEAD_DIM, seq_len),
                                   lambda i: (i, 0, 0, 0, 0))] * 2
    return pl.pallas_call(
        functools.partial(_in_even_kernel, latent=latent, tm=tm),
        out_shape=out_shape,
        grid=(n // tm,),
        in_specs=in_specs,
        out_specs=out_specs,
        compiler_params=_params(1),
        name="in_proj_even_latent" if latent else "in_proj_even_ctx",
    )(*args)


def _retention_kernel(*refs, n, nb, has_state, write_state):
    lg_ref, q_ref, k_ref, v_ref, gr_ref, gn_ref = refs[:6]
    refs = refs[6:]
    if has_state:
        s0f_ref, s0b_ref = refs[:2]
        refs = refs[2:]
    o_ref = refs[0]
    if write_state:
        sf_ref, sb_ref = refs[1:3]
    c = RET_CHUNK
    assert c == LANES and 2 * RET_DK == LANES
    nc = n // c
    p = pl.program_id(0)
    lgf = [lg_ref[0, 2 * p + hh] for hh in range(2)]
    lgb = [lg_ref[1, 2 * p + hh] for hh in range(2)]
    row = lax.broadcasted_iota(jnp.int32, (c, c), 0)
    col = lax.broadcasted_iota(jnp.int32, (c, c), 1)
    diff = (row - col).astype(F32)
    pos = row.astype(F32)
    lane_lo = col < RET_DK

    def both_scans(f, b):
        return (jnp.where(diff >= 0, jnp.exp(f * jnp.maximum(diff, 0.0)), 0.0)
                + jnp.where(diff <= 0, jnp.exp(b * jnp.maximum(-diff, 0.0)), 0.0))

    decay2 = jnp.concatenate([both_scans(lgf[0], lgb[0]), both_scans(lgf[1], lgb[1])], axis=1)
    lgf_lane = jnp.where(lane_lo, lgf[0], lgf[1])
    lgb_lane = jnp.where(lane_lo, lgb[0], lgb[1])
    qd_f = jnp.exp(lgf_lane * (pos + 1.0))
    kd_f = jnp.exp(lgf_lane * (c - 1.0 - pos))
    qd_b = jnp.exp(lgb_lane * (c - pos))
    kd_b = jnp.exp(lgb_lane * pos)
    srow = lax.broadcasted_iota(jnp.int32, (c, 2 * RET_DV), 0)
    scol = lax.broadcasted_iota(jnp.int32, (c, 2 * RET_DV), 1)
    row_a = srow < RET_DK
    col_a = scol < RET_DV
    own = row_a == col_a
    cd_f = jnp.exp(jnp.where(row_a, lgf[0], lgf[1]) * float(c))
    cd_b = jnp.exp(jnp.where(row_a, lgb[0], lgb[1]) * float(c))
    zeros_half = jnp.zeros((RET_DK, RET_DV), F32)
    gn = gn_ref[...]

    def place(s0_ref, bi):
        top = jnp.concatenate([s0_ref[bi, 0, 0], zeros_half], axis=1)
        bot = jnp.concatenate([zeros_half, s0_ref[bi, 0, 1]], axis=1)
        return jnp.concatenate([top, bot], axis=0)

    for bi in range(nb):
        rows = [slice(bi * n + i * c, bi * n + (i + 1) * c) for i in range(nc)]
        if has_state:
            s_f = place(s0f_ref, bi)
            s_b = place(s0b_ref, bi)
        else:
            s_f = jnp.zeros((c, 2 * RET_DV), F32)
            s_b = jnp.zeros((c, 2 * RET_DV), F32)
        before_f = []
        for i in range(nc):
            before_f.append(s_f)
            kv = _tn((k_ref[rows[i], :] * kd_f).astype(BF16), v_ref[rows[i], :])
            s_f = s_f * cd_f + jnp.where(own, kv, 0.0)
        before_b = [None] * nc
        for i in reversed(range(nc)):
            before_b[i] = s_b
            kv = _tn((k_ref[rows[i], :] * kd_b).astype(BF16), v_ref[rows[i], :])
            s_b = s_b * cd_b + jnp.where(own, kv, 0.0)
        if write_state:
            for hh in range(2):
                blk = (slice(hh * RET_DK, (hh + 1) * RET_DK), slice(hh * RET_DV, (hh + 1) * RET_DV))
                sf_ref[bi, 0, hh] = s_f[blk]
                sb_ref[bi, 0, hh] = s_b[blk]
        for i in range(nc):
            qc = q_ref[rows[i], :]
            kc = k_ref[rows[i], :]
            vc = v_ref[rows[i], :]
            k_cat = jnp.concatenate([jnp.where(lane_lo, kc, 0.0), jnp.where(lane_lo, 0.0, kc)], axis=0)
            v_blk = jnp.concatenate([jnp.where(col_a, vc, jnp.zeros_like(vc)),
                                     jnp.where(col_a, jnp.zeros_like(vc), vc)], axis=0)
            scores = _nt(qc.astype(BF16), k_cat.astype(BF16)) * decay2
            q_cat = jnp.concatenate([(qc * qd_f).astype(BF16), (qc * qd_b).astype(BF16)], axis=1)
            s_cat = jnp.concatenate([before_f[i], before_b[i]], axis=0).astype(BF16)
            o = _nn(scores.astype(BF16), v_blk) + _nn(q_cat, s_cat)
            for hh in range(2):
                vcols = slice(hh * RET_DV, (hh + 1) * RET_DV)
                oh = o[:, vcols]
                mu = jnp.mean(oh, axis=-1, keepdims=True)
                d = oh - mu
                var = jnp.mean(d * d, axis=-1, keepdims=True)
                y = d * lax.rsqrt(var + GN_EPS) * gn[:, vcols] * gr_ref[rows[i], vcols]
                o_ref[rows[i], vcols] = y.astype(BF16)


def _retention(log_g, qr, kr, vr, gr, gn, state_f, state_b, *, n_batch, seq_len, write_state):
    n = qr.shape[0]
    pairs = RET_HEADS // 2
    has_state = state_f is not None
    nb = max(1, 1024 // seq_len)
    rows = nb * seq_len
    tok = lambda p, g: (g, p)
    in_specs = [
        pl.BlockSpec(memory_space=pltpu.SMEM),
        pl.BlockSpec((rows, LANES), tok),
        pl.BlockSpec((rows, LANES), tok),
        pl.BlockSpec((rows, 2 * RET_DV), tok),
        pl.BlockSpec((rows, 2 * RET_DV), tok),
        pl.BlockSpec((1, 2 * RET_DV), lambda p, g: (0, p)),
    ]
    args = [log_g, qr, kr, vr, gr, gn]
    state_spec = pl.BlockSpec((nb, 1, 2, RET_DK, RET_DV), lambda p, g: (g, 0, p, 0, 0))
    if has_state:
        in_specs += [state_spec, state_spec]
        args += [state_f, state_b]
    out_shape = [jax.ShapeDtypeStruct((n, RET_HEADS * RET_DV), BF16)]
    out_specs = [pl.BlockSpec((rows, 2 * RET_DV), tok)]
    if write_state:
        st = jax.ShapeDtypeStruct((n_batch, 1, RET_HEADS, RET_DK, RET_DV), F32)
        out_shape += [st, st]
        out_specs += [state_spec, state_spec]
    return pl.pallas_call(
        functools.partial(_retention_kernel, n=seq_len, nb=nb, has_state=has_state, write_state=write_state),
        out_shape=out_shape,
        grid=(pairs, n_batch // nb),
        in_specs=in_specs,
        out_specs=out_specs,
        compiler_params=_params(2),
        name="retention_latent" if has_state else "retention_ctx",
    )(*args)


def _gqa_kernel(*refs, n_src, tq):
    q_ref = refs[0]
    k_refs = refs[1:1 + 2 * n_src:2]
    v_refs = refs[2:2 + 2 * n_src:2]
    o_ref = refs[1 + 2 * n_src]
    units = [(g, half) for g in range(GQA_KV_HEADS) for half in range(2)]
    outs = {}

    def scores_of(unit):
        g, half = unit
        base = g * 2 * LANES
        q = jnp.concatenate([q_ref[:, base:base + LANES], q_ref[:, base + LANES:base + 2 * LANES]], axis=0)
        return [_nt(q, _masked_half(k_ref[:, g * LANES:(g + 1) * LANES], half)) for k_ref in k_refs]

    def finish(unit, scores):
        g, half = unit
        outs[half] = _softmax_apply(scores, [v_ref[:, g * LANES:(g + 1) * LANES] for v_ref in v_refs])
        if half == 1:
            base = g * 2 * LANES
            o = jnp.where(_lane_lo(2 * tq), outs[0], outs[1]).astype(BF16)
            o_ref[:, base:base + LANES] = o[:tq]
            o_ref[:, base + LANES:base + 2 * LANES] = o[tq:]

    _run_pipelined(units, scores_of, finish)


def _gqa_attention(qa, kd, vd, ctx_kd, ctx_vd, *, n_batch, seq_len, tq):
    n = qa.shape[0]
    tiles = seq_len // tq
    n_src = 1 if ctx_kd is None else 2
    qmap = lambda b, t: (b * tiles + t, 0)
    kmap = lambda b, t: (b, 0)
    in_specs = [pl.BlockSpec((tq, GQA_HEADS * HEAD_DIM), qmap),
                pl.BlockSpec((seq_len, 2 * LANES), kmap),
                pl.BlockSpec((seq_len, 2 * LANES), kmap)]
    args = [qa, kd, vd]
    if n_src == 2:
        past = ctx_kd.shape[1]
        cmap = lambda b, t: (b, 0, 0)
        in_specs += [pl.BlockSpec((None, past, 2 * LANES), cmap)] * 2
        args += [ctx_kd, ctx_vd]
    return pl.pallas_call(
        functools.partial(_gqa_kernel, n_src=n_src, tq=tq),
        out_shape=jax.ShapeDtypeStruct((n, GQA_HEADS * HEAD_DIM), BF16),
        grid=(n_batch, tiles),
        in_specs=in_specs,
        out_specs=pl.BlockSpec((tq, GQA_HEADS * HEAD_DIM), qmap),
        compiler_params=_params(2),
        name="gqa_latent" if n_src == 2 else "gqa_ctx",
    )(*args)


def _zero_rows(arr, rows):
    pieces, cur = [], 0
    sub = lax.broadcasted_iota(jnp.int32, (8, arr.shape[1]), 0)
    for r in sorted(rows):
        g0 = (r // 8) * 8
        if g0 > cur:
            pieces.append(arr[cur:g0])
        pieces.append(jnp.where(sub == r - g0, 0.0, arr[g0:g0 + 8]))
        cur = g0 + 8
    if cur < arr.shape[0]:
        pieces.append(arr[cur:])
    return jnp.concatenate(pieces, axis=0)


def _mix_ffn_kernel(*refs, tm, seq_len, final, n_mix):
    x_ref, gate_mix_ref, g_ref, scale_ref, shift_ref, gate_ref = refs[:6]
    m_refs = refs[6:6 + n_mix]
    w_refs = refs[6 + n_mix:6 + 2 * n_mix]
    wup_ref, cw_ref, cb_ref, wd_ref = refs[6 + 2 * n_mix:10 + 2 * n_mix]
    refs = refs[10 + 2 * n_mix:]
    if final:
        gfin_ref = refs[0]
        refs = refs[1:]
    o_ref, act_ref, hb_ref = refs
    halves = [slice(rb * (tm // 2), (rb + 1) * (tm // 2)) for rb in range(2)]
    for rows in halves:
        acc = None
        for m_ref, w_ref in zip(m_refs, w_refs):
            t = _nn(m_ref[rows, :], w_ref[...])
            acc = t if acc is None else acc + t
        y = x_ref[rows, :] + gate_mix_ref[...] * acc
        o_ref[rows, :] = y
        hb_ref[rows, :] = _modulated_norm(y, g_ref[...], scale_ref[...], shift_ref[...]).astype(BF16)
    seq_starts = list(range(0, tm, seq_len))
    seq_ends = [s + seq_len - 1 for s in seq_starts]
    for j in range(N_FF_CHUNKS):
        u = _nn(hb_ref[...], wup_ref[j])
        cw = cw_ref[j]
        prev = _zero_rows(pltpu.roll(u, 1, 0), seq_starts)
        nxt = _zero_rows(pltpu.roll(u, tm - 1, 0), seq_ends)
        cv = prev * cw[0:1] + u * cw[1:2] + nxt * cw[2:3] + cb_ref[j]
        a = cv[:, :FF_CHUNK]
        act_ref[:, j * FF_CHUNK:(j + 1) * FF_CHUNK] = (a * _sigmoid(a) * cv[:, FF_CHUNK:]).astype(BF16)
    for rows in halves:
        y = o_ref[rows, :] + gate_ref[...] * _nn(act_ref[rows, :], wd_ref[...])
        if final:
            ms = jnp.mean(y * y, axis=-1, keepdims=True)
            y = y * lax.rsqrt(ms + EPS) * gfin_ref[...]
        o_ref[rows, :] = y


def _mix_ffn(x2d, mods, layer, mixes, weights, ffn_norm_g, wup_c, cw_c, cb_c, wd_c, final_g, *, seq_len, latent):
    n = x2d.shape[0]
    tm = 1024
    bidx = _batch_index_fn(latent, tm, seq_len)
    row = lambda i: (i, 0)
    const2 = lambda i: (0, 0)
    once = pl.Buffered(1)
    resident = lambda a: pl.BlockSpec((None,) + a.shape[1:], lambda i: (layer,) + (0,) * (a.ndim - 1),
                                      pipeline_mode=once)
    in_specs = [pl.BlockSpec((tm, D_MODEL), row), _mod_spec(layer, 2, bidx), pl.BlockSpec((1, D_MODEL), const2),
                _mod_spec(layer, 4, bidx), _mod_spec(layer, 3, bidx), _mod_spec(layer, 5, bidx)]
    in_specs += [pl.BlockSpec((tm, m.shape[1]), row) for m in mixes]
    in_specs += [pl.BlockSpec(w.shape, const2, pipeline_mode=once) for w in weights]
    in_specs += [resident(wup_c), resident(cw_c), resident(cb_c), resident(wd_c)]
    args = [x2d, mods, ffn_norm_g, mods, mods, mods, *mixes, *weights, wup_c, cw_c, cb_c, wd_c]
    final = final_g is not None
    if final:
        in_specs.append(pl.BlockSpec((1, D_MODEL), const2))
        args.append(final_g)
    return pl.pallas_call(
        functools.partial(_mix_ffn_kernel, tm=tm, seq_len=seq_len, final=final, n_mix=len(mixes)),
        out_shape=jax.ShapeDtypeStruct((n, D_MODEL), F32),
        grid=(n // tm,),
        in_specs=in_specs,
        out_specs=pl.BlockSpec((tm, D_MODEL), row),
        scratch_shapes=[pltpu.VMEM((tm, D_FF), BF16), pltpu.VMEM((tm, D_MODEL), BF16)],
        compiler_params=_params(1),
        name="mix_ffn_latent" if latent else "mix_ffn_ctx",
    )(*args)


def _regroup_up_kernel(a_ref, g_ref, o_ref):
    o_ref[:, :FF_CHUNK] = a_ref[...].astype(BF16)
    o_ref[:, FF_CHUNK:] = g_ref[...].astype(BF16)


def _regroup_w_up(w_up):
    depth = w_up.shape[0]
    return pl.pallas_call(
        _regroup_up_kernel,
        out_shape=jax.ShapeDtypeStruct((depth, N_FF_CHUNKS, D_MODEL, 2 * FF_CHUNK), BF16),
        grid=(depth, N_FF_CHUNKS),
        in_specs=[pl.BlockSpec((None, D_MODEL, FF_CHUNK), lambda l, j: (l, 0, j)),
                  pl.BlockSpec((None, D_MODEL, FF_CHUNK), lambda l, j: (l, 0, N_FF_CHUNKS + j))],
        out_specs=pl.BlockSpec((None, None, D_MODEL, 2 * FF_CHUNK), lambda l, j: (l, j, 0, 0)),
        compiler_params=_params(2),
        name="regroup_w_up",
    )(w_up, w_up)


def _in_odd_kernel(*refs, write_cache, tm):
    x_ref, shift_ref, scale_ref, g_ref, w_ref, q_ref, k_ref, v_ref = refs[:8]
    hb = _modulated_norm(x_ref[...], g_ref[...], scale_ref[...], shift_ref[...]).astype(BF16)
    width = NA_HEADS * HEAD_DIM
    q_scale = HEAD_DIM ** -0.5 * LOG2E
    for which, dst in ((1, k_ref), (2, v_ref)):
        for c in range(2):
            r = _nn(hb, w_ref[:, which * width + c * 512:which * width + (c + 1) * 512])
            dst[:, c * 512:(c + 1) * 512] = r.astype(BF16)
            if write_cache:
                cache_ref = refs[8 + which - 1]
                seq = cache_ref.shape[4]
                for bb in range(tm // seq):
                    for blk in range(512 // LANES):
                        t = r[bb * seq:(bb + 1) * seq, blk * LANES:(blk + 1) * LANES].T
                        for hh in range(2):
                            head = c * (512 // HEAD_DIM) + 2 * blk + hh
                            cache_ref[bb, 0, head] = t[hh * HEAD_DIM:(hh + 1) * HEAD_DIM]
    for c in range(2):
        cols = slice(c * 512, (c + 1) * 512)
        q_ref[:, cols] = (_nn(hb, w_ref[:, c * 512:(c + 1) * 512]) * q_scale).astype(BF16)


def _in_proj_odd(x2d, mods, layer, norm_g, w_bf, *, n_batch, seq_len, latent):
    n = x2d.shape[0]
    tm = 512
    width = NA_HEADS * HEAD_DIM
    bidx = _batch_index_fn(latent, tm, seq_len)
    row = lambda i: (i, 0)
    const = lambda i: (0, 0)
    in_specs = [pl.BlockSpec((tm, D_MODEL), row), _mod_spec(layer, 0, bidx), _mod_spec(layer, 1, bidx),
                pl.BlockSpec((1, D_MODEL), const), pl.BlockSpec(w_bf.shape, const)]
    out_shape = [jax.ShapeDtypeStruct((n, width), BF16)] * 3
    out_specs = [pl.BlockSpec((tm, width), row)] * 3
    write_cache = not latent
    if write_cache:
        cache = jax.ShapeDtypeStruct((n_batch, 1, NA_HEADS, HEAD_DIM, seq_len), F32)
        out_shape += [cache, cache]
        out_specs += [pl.BlockSpec((tm // seq_len, 1, NA_HEADS, HEAD_DIM, seq_len),
                                   lambda i: (i, 0, 0, 0, 0))] * 2
    return pl.pallas_call(
        functools.partial(_in_odd_kernel, write_cache=write_cache, tm=tm),
        out_shape=out_shape,
        grid=(n // tm,),
        in_specs=in_specs,
        out_specs=out_specs,
        compiler_params=_params(1),
        name="in_proj_odd_latent" if latent else "in_proj_odd_ctx",
    )(x2d, mods, mods, norm_g, w_bf)


def _dense_pairs_kernel(q_ref, k_ref, v_ref, o_ref):
    units = [(p, half) for p in range(NA_HEADS // 2) for half in range(2)]
    outs = {}

    def scores_of(unit):
        p, half = unit
        cols = slice(p * LANES, (p + 1) * LANES)
        return [_nt(q_ref[:, cols], _masked_half(k_ref[:, cols], half))]

    def finish(unit, scores):
        p, half = unit
        cols = slice(p * LANES, (p + 1) * LANES)
        outs[half] = _softmax_apply(scores, [v_ref[:, cols]])
        if half == 1:
            o_ref[:, cols] = jnp.where(_lane_lo(q_ref.shape[0]), outs[0], outs[1]).astype(BF16)

    _run_pipelined(units, scores_of, finish)


def _dense_attention_ctx(q, k, v, *, n_batch, seq_len):
    width = NA_HEADS * HEAD_DIM
    spec = pl.BlockSpec((seq_len, width), lambda b: (b, 0))
    return pl.pallas_call(
        _dense_pairs_kernel,
        out_shape=jax.ShapeDtypeStruct(q.shape, BF16),
        grid=(n_batch,),
        in_specs=[spec, spec, spec],
        out_specs=spec,
        compiler_params=_params(1),
        name="dense_attention_ctx",
    )(q, k, v)


NA_Q_ROWS = 8
NA_KEY_ROWS = 12


def _na_window_start(r):
    return min(max(r - NA_KH // 2, 0), (1024 // GRID_W) - NA_KH)


def _na_bias_tile(bias_ref, half, tile, n_rows):
    r0 = tile * NA_Q_ROWS
    kr0 = min(max(r0 - NA_KH // 2, 0), n_rows - NA_KEY_ROWS)
    lo = _lane_lo(GRID_W)
    neg_block = jnp.full((GRID_W, LANES), NEG, F32)
    left_off = jnp.where(lo, NEG, 0.0)
    right_off = jnp.where(lo, 0.0, NEG)
    rows = []
    for rq in range(NA_Q_ROWS):
        r = r0 + rq
        rs = _na_window_start(r)
        blocks = []
        for kk in range(NA_KEY_ROWS // 2):
            ka = kr0 + 2 * kk
            va = rs <= ka < rs + NA_KH
            vb = rs <= ka + 1 < rs + NA_KH
            if not (va or vb):
                blocks.append(neg_block)
                continue
            blk = bias_ref[half, ka - r + NA_KH]
            if not va:
                blk = blk + left_off
            if not vb:
                blk = blk + right_off
            blocks.append(blk)
        rows.append(jnp.concatenate(blocks, axis=1))
    return jnp.concatenate(rows, axis=0), kr0


def _na_kernel(q_ref, k_ref, v_ref, ck_ref, cv_ref, bias_ref, o_ref, *, n_rows):
    tq = NA_Q_ROWS * GRID_W
    span = NA_KEY_ROWS * GRID_W
    units = [(tile, half) for tile in range(n_rows // NA_Q_ROWS) for half in range(2)]
    outs = {}

    def window(tile):
        kr0 = min(max(tile * NA_Q_ROWS - NA_KH // 2, 0), n_rows - NA_KEY_ROWS)
        return slice(kr0 * GRID_W, kr0 * GRID_W + span)

    def scores_of(unit):
        tile, half = unit
        q = q_ref[tile * tq:(tile + 1) * tq, :]
        bias, _ = _na_bias_tile(bias_ref, half, tile, n_rows)
        ck_t = jnp.concatenate([ck_ref[0], ck_ref[1]], axis=0)
        own = (lax.broadcasted_iota(jnp.int32, ck_t.shape, 0) < HEAD_DIM) == (half == 0)
        return [_nt(q, _masked_half(k_ref[window(tile), :], half)) + bias,
                _nn(q, jnp.where(own, ck_t, 0.0).astype(BF16))]

    def finish(unit, scores):
        tile, half = unit
        cv_t = jnp.concatenate([cv_ref[0], cv_ref[1]], axis=0).astype(BF16)
        outs[half] = _softmax_apply(scores, [v_ref[window(tile), :], ("t", cv_t)])
        if half == 1:
            o_ref[tile * tq:(tile + 1) * tq, :] = jnp.where(_lane_lo(tq), outs[0], outs[1]).astype(BF16)

    _run_pipelined(units, scores_of, finish)


def _na_bias_table(rpb):
    cidx = np.arange(GRID_W)
    cs = np.clip(cidx - NA_KW // 2, 0, GRID_W - NA_KW)
    kc = np.arange(GRID_W)
    inside = (kc[None, :] >= cs[:, None]) & (kc[None, :] < cs[:, None] + NA_KW)
    rel = kc[None, :] - cidx[:, None] + NA_KW - 1
    onehot = (rel[None] == np.arange(2 * NA_KW - 1)[:, None, None]) & inside[None]
    m = jnp.einsum("hdj,jck->hdck", rpb * LOG2E, jnp.asarray(onehot, F32), precision=lax.Precision.HIGHEST)
    m = jnp.where(jnp.asarray(inside)[None, None], m, NEG)
    neg = jnp.full((rpb.shape[0], 1, GRID_W, GRID_W), NEG, F32)
    left = jnp.concatenate([neg, m], axis=1)
    right = jnp.concatenate([m, neg], axis=1)
    return jnp.concatenate([left, right], axis=-1)


def _na_attention(q, k, v, ctx_k, ctx_v, bias_tab, *, n_batch, seq_len):
    pairs = NA_HEADS // 2
    tok = lambda p, b: (b, p)
    ctx = lambda p, b: (b, p, 0, 0)
    past = ctx_k.shape[-1]
    return pl.pallas_call(
        functools.partial(_na_kernel, n_rows=seq_len // GRID_W),
        out_shape=jax.ShapeDtypeStruct(q.shape, BF16),
        grid=(pairs, n_batch),
        in_specs=[pl.BlockSpec((seq_len, LANES), tok)] * 3
        + [pl.BlockSpec((None, 2, HEAD_DIM, past), ctx)] * 2
        + [pl.BlockSpec((2, 2 * NA_KH, GRID_W, LANES), lambda p, b: (p, 0, 0, 0))],
        out_specs=pl.BlockSpec((seq_len, LANES), tok),
        compiler_params=_params(2),
        name="neighbourhood_attention",
    )(q, k, v, ctx_k, ctx_v, bias_tab)


def _rope_tables(n):
    t = jnp.arange(n)
    row = (t // GRID_W).astype(F32)
    col = (t % GRID_W).astype(F32)
    half = HEAD_DIM // 2
    inv = ROPE_BASE ** (-jnp.arange(0, half, 2, dtype=F32) / half)
    ang_r = row[:, None] * inv
    ang_c = col[:, None] * inv
    cos_h = jnp.concatenate([jnp.cos(ang_r)] * 2 + [jnp.cos(ang_c)] * 2, axis=-1)
    sin_h = jnp.concatenate([-jnp.sin(ang_r), jnp.sin(ang_r), -jnp.sin(ang_c), jnp.sin(ang_c)], axis=-1)
    return jnp.concatenate([cos_h, cos_h], axis=-1), jnp.concatenate([sin_h, sin_h], axis=-1)


def _ffn_weights(w_up, conv_w, conv_b, w_down):
    depth = w_up.shape[0]

    def regroup(a):
        lead = a.shape[1:-1]
        a = a.reshape(depth, *lead, 2, N_FF_CHUNKS, FF_CHUNK)
        a = jnp.moveaxis(a, -2, 1)
        return a.reshape(depth, N_FF_CHUNKS, *lead, 2 * FF_CHUNK)
    return (_regroup_w_up(w_up), regroup(conv_w), regroup(conv_b[:, None, :]),
            w_down.astype(BF16))


def _token_major_dup(cache):
    b, kv, t, d = cache.shape
    c = jnp.transpose(cache, (0, 2, 1, 3))[:, :, :, None, :]
    return jnp.broadcast_to(c, (b, t, kv, 2, d)).reshape(b, t, kv * 2 * d).astype(BF16)


def _head_transposed(cache):
    return jnp.swapaxes(cache, -1, -2)


def kernel(x_prompt, x_sample, state_ret_fwd, state_ret_bwd, cache_gqa_k, cache_gqa_v, cache_na_k, cache_na_v,
           c, c_ctx, ada_w, ada_b, norm_mix, norm_ffn, norm_final, even_w_in, even_w_out, ret_decay_fwd,
           ret_decay_bwd, ret_gn, gqa_q_norm, gqa_k_norm, odd_w_in, odd_w_out, na_rpb, ffn_w_up, ffn_conv_w,
           ffn_conv_b, ffn_w_down):
    nb_c, len_c, _ = x_prompt.shape
    nb_s, len_s, _ = x_sample.shape
    depth = ada_w.shape[0]
    streams = {
        False: dict(n_batch=nb_c, seq_len=len_c),
        True: dict(n_batch=nb_s, seq_len=len_s),
    }
    xs = {False: x_prompt.reshape(nb_c * len_c, D_MODEL), True: x_sample.reshape(nb_s * len_s, D_MODEL)}

    rows = 8 * (-(-(1 + nb_s) // 8))
    cond = jnp.zeros((rows, D_MODEL), F32).at[0].set(c_ctx).at[1:1 + nb_s].set(c)
    mods = _ada_params(cond, ada_w, ada_b).reshape(depth, rows, 6, 1, D_MODEL)
    rope_tabs = _rope_tables(len_s)
    ffn_w = _ffn_weights(ffn_w_up, ffn_conv_w, ffn_conv_b, ffn_w_down)
    outs = {}
    mixed = {}

    for l in range(depth):
        g_mix = norm_mix[l][None, :]
        g_ffn = norm_ffn[l][None, :]
        if l % 2 == 0:
            e = l // 2
            w_in = even_w_in[e].astype(BF16)
            w_out = even_w_out[e].astype(BF16)
            w_out_parts = [w_out[:RET_HEADS * RET_DV], w_out[RET_HEADS * RET_DV:]]
            log_g = jnp.stack([jax.nn.log_sigmoid(ret_decay_fwd[e].astype(F32)),
                               jax.nn.log_sigmoid(ret_decay_bwd[e].astype(F32))])
            gn = ret_gn[e][None, :]
            qg2 = jnp.tile(gqa_q_norm[e], 2)[None, :]
            kg2 = jnp.tile(gqa_k_norm[e], 2)[None, :]
            for latent in (False, True):
                st = streams[latent]
                res = _in_proj_even(xs[latent], mods, l, g_mix, w_in, qg2, kg2, rope_tabs, latent=latent, **st)
                qr, kr, vr, gr, qa, kd, vd = res[:7]
                if latent:
                    ret = _retention(log_g, qr, kr, vr, gr, gn, state_ret_fwd[:, e:e + 1],
                                     state_ret_bwd[:, e:e + 1], write_state=False, **st)[0]
                    att = _gqa_attention(qa, kd, vd, _token_major_dup(cache_gqa_k[:, e]),
                                         _token_major_dup(cache_gqa_v[:, e]), tq=512, **st)
                else:
                    outs.setdefault("gk", []).append(res[7])
                    outs.setdefault("gv", []).append(res[8])
                    ret, s_f, s_b = _retention(log_g, qr, kr, vr, gr, gn, None, None, write_state=True, **st)
                    outs.setdefault("sf", []).append(s_f)
                    outs.setdefault("sb", []).append(s_b)
                    att = _gqa_attention(qa, kd, vd, None, None, tq=st["seq_len"], **st)
                mixed[latent] = ([ret, att], w_out_parts)
        else:
            o = l // 2
            w_in = odd_w_in[o].astype(BF16)
            w_out = odd_w_out[o].astype(BF16)
            for latent in (False, True):
                st = streams[latent]
                res = _in_proj_odd(xs[latent], mods, l, g_mix, w_in, latent=latent, **st)
                q, k, v = res[:3]
                if latent:
                    att = _na_attention(q, k, v, _head_transposed(cache_na_k[:, o]), _head_transposed(cache_na_v[:, o]),
                                        _na_bias_table(na_rpb[o]), **st)
                else:
                    outs.setdefault("nk", []).append(res[3])
                    outs.setdefault("nv", []).append(res[4])
                    att = _dense_attention_ctx(q, k, v, **st)
                mixed[latent] = ([att], [w_out])
        final_g = norm_final[None, :] if l == depth - 1 else None
        for latent in (False, True):
            xs[latent] = _mix_ffn(xs[latent], mods, l, *mixed[latent], g_ffn, *ffn_w, final_g,
                                   seq_len=streams[latent]["seq_len"], latent=latent)

    tr = lambda a: jnp.swapaxes(a, -1, -2)
    cat = lambda name: outs[name][0] if len(outs[name]) == 1 else jnp.concatenate(outs[name], axis=1)
    return (xs[False].reshape(nb_c, len_c, D_MODEL), xs[True].reshape(nb_s, len_s, D_MODEL),
            cat("sf"), cat("sb"), tr(cat("gk")), tr(cat("gv")), tr(cat("nk")), tr(cat("nv")))
```

```python
import functools

import numpy as np
import jax
import jax.numpy as jnp
from jax import lax
from jax.experimental import pallas as pl
from jax.experimental.pallas import tpu as pltpu

F32 = jnp.float32
BF16 = jnp.bfloat16

D_MODEL = 1024
GRID_W = 64
HEAD_DIM = 64
ROPE_BASE = 10000.0
EPS = 1e-6
GN_EPS = 1e-5
RET_HEADS = 8
RET_DK = 64
RET_DV = 128
RET_CHUNK = 128
GQA_HEADS = 8
GQA_KV_HEADS = 2
NA_HEADS = 16
NA_KH = 8
NA_KW = 16
D_FF = 2816
LANES = 128
MXU_DIM = 256
FF_CHUNK = MXU_DIM
N_FF_CHUNKS = D_FF // FF_CHUNK
NEG = -1e30
LOG2E = 1.4426950408889634
VMEM_LIMIT = 56 * 1024 * 1024


def _nn(a, b):
    return jnp.dot(a, b, preferred_element_type=F32)


def _nt(a, b):
    return lax.dot_general(a, b, (((1,), (1,)), ((), ())), preferred_element_type=F32)


def _tn(a, b):
    return lax.dot_general(a, b, (((0,), (0,)), ((), ())), preferred_element_type=F32)


def _sigmoid(x):
    return 1.0 / (1.0 + jnp.exp(-x))


def _params(n_axes):
    return pltpu.CompilerParams(dimension_semantics=("arbitrary",) * n_axes, vmem_limit_bytes=VMEM_LIMIT)


def _modulated_norm(x, g, scale, shift):
    ms = jnp.mean(x * x, axis=-1, keepdims=True)
    return (x * lax.rsqrt(ms + EPS) * g) * (1.0 + scale) + shift


def _lane_lo(rows):
    return lax.broadcasted_iota(jnp.int32, (rows, LANES), 1) < HEAD_DIM


def _head_rms_norm(xb, gain, lo):
    sq = xb * xb
    s_lo = jnp.sum(jnp.where(lo, sq, 0.0), axis=-1, keepdims=True)
    s_hi = jnp.sum(jnp.where(lo, 0.0, sq), axis=-1, keepdims=True)
    r = jnp.where(lo, lax.rsqrt(s_lo * (1.0 / HEAD_DIM) + EPS), lax.rsqrt(s_hi * (1.0 / HEAD_DIM) + EPS))
    return xb * r * gain


def _rope(xb, cos, sin_signed, first16):
    partner = jnp.where(first16, pltpu.roll(xb, LANES - 16, 1), pltpu.roll(xb, 16, 1))
    return xb * cos + partner * sin_signed


def _softmax_apply(scores, values):
    m = functools.reduce(jnp.maximum, [jnp.max(s, axis=-1, keepdims=True) for s in scores])
    es = [jnp.exp2(s - m) for s in scores]
    l = functools.reduce(jnp.add, [jnp.sum(e, axis=-1, keepdims=True) for e in es])
    o = None
    for e, v in zip(es, values):
        t = _nt(e.astype(BF16), v[1]) if isinstance(v, tuple) else _nn(e.astype(BF16), v)
        o = t if o is None else o + t
    return o * (1.0 / l)


def _run_pipelined(units, scores_of, finish):
    pending = scores_of(units[0])
    for n, unit in enumerate(units):
        nxt = scores_of(units[n + 1]) if n + 1 < len(units) else None
        finish(unit, pending)
        pending = nxt


def _masked_half(kp, half):
    lo = _lane_lo(kp.shape[0])
    return jnp.where(lo if half == 0 else jnp.logical_not(lo), kp, jnp.zeros_like(kp))


def _ada_kernel(c_ref, w_ref, b_ref, o_ref):
    c = c_ref[...]
    a = (c * _sigmoid(c)).astype(BF16)
    o_ref[...] = _nn(a, w_ref[...].astype(BF16)) + b_ref[...]


def _ada_params(cond, ada_w, ada_b):
    depth = ada_w.shape[0]
    rows = cond.shape[0]
    tn = 1024
    return pl.pallas_call(
        _ada_kernel,
        out_shape=jax.ShapeDtypeStruct((depth, rows, 6 * D_MODEL), F32),
        grid=(depth, 6 * D_MODEL // tn),
        in_specs=[
            pl.BlockSpec((rows, D_MODEL), lambda l, j: (0, 0)),
            pl.BlockSpec((None, D_MODEL, tn), lambda l, j: (l, 0, j)),
            pl.BlockSpec((None, 1, tn), lambda l, j: (l, 0, j)),
        ],
        out_specs=pl.BlockSpec((None, rows, tn), lambda l, j: (l, 0, j)),
        compiler_params=_params(2),
        name="ada_params",
    )(cond, ada_w, ada_b.reshape(depth, 1, 6 * D_MODEL))


def _mod_spec(layer, which, bidx):
    return pl.BlockSpec((None, None, None, 1, D_MODEL), lambda i: (layer, bidx(i), which, 0, 0))


def _batch_index_fn(latent, rows_per_tile, seq_len):
    if not latent:
        return lambda i: 0
    return lambda i: 1 + (i * rows_per_tile) // seq_len


def _in_even_kernel(*refs, latent, tm):
    if latent:
        (x_ref, shift_ref, scale_ref, g_ref, w_ref, qg_ref, kg_ref, cos_ref, sin_ref,
         qr_ref, kr_ref, vr_ref, gr_ref, qa_ref, kd_ref, vd_ref) = refs
    else:
        (x_ref, shift_ref, scale_ref, g_ref, w_ref, qg_ref, kg_ref,
         qr_ref, kr_ref, vr_ref, gr_ref, qa_ref, kd_ref, vd_ref, ck_ref, cv_ref) = refs
    hb = _modulated_norm(x_ref[...], g_ref[...], scale_ref[...], shift_ref[...]).astype(BF16)
    lane = lax.broadcasted_iota(jnp.int32, (tm, LANES), 1)
    lo = lane < HEAD_DIM
    if latent:
        cos = cos_ref[...]
        sin = sin_ref[...]
        first16 = (lane % 32) < 16
        rope = lambda v: _rope(v, cos, sin, first16)
    else:
        rope = lambda v: v
    dk_scale = RET_DK ** -0.5
    q_scale = HEAD_DIM ** -0.5 * LOG2E

    r = _nn(hb, w_ref[:, 3072:3584])
    qg = qg_ref[...]
    for b in range(4):
        blk = rope(_head_rms_norm(r[:, b * LANES:(b + 1) * LANES], qg, lo)) * q_scale
        qa_ref[:, b * LANES:(b + 1) * LANES] = blk.astype(BF16)
    r = _nn(hb, w_ref[:, 3584:3840])
    kn = _head_rms_norm(r[:, 0:LANES], kg_ref[...], lo)
    vn = r[:, LANES:2 * LANES]
    if not latent:
        seq = ck_ref.shape[4]
        for bb in range(tm // seq):
            for src, dst in ((kn, ck_ref), (vn, cv_ref)):
                t = src[bb * seq:(bb + 1) * seq, :].T
                for kv in range(GQA_KV_HEADS):
                    dst[bb, 0, kv] = t[kv * HEAD_DIM:(kv + 1) * HEAD_DIM]
    kn = rope(kn)
    for src, dst in ((kn, kd_ref), (vn, vd_ref)):
        sw = pltpu.roll(src, HEAD_DIM, 1)
        dst[:, 0:LANES] = jnp.where(lo, src, sw).astype(BF16)
        dst[:, LANES:2 * LANES] = jnp.where(lo, sw, src).astype(BF16)
    r = _nn(hb, w_ref[:, 0:512])
    for b in range(4):
        qr_ref[:, b * LANES:(b + 1) * LANES] = rope(r[:, b * LANES:(b + 1) * LANES])
    r = _nn(hb, w_ref[:, 512:1024]) * dk_scale
    for b in range(4):
        kr_ref[:, b * LANES:(b + 1) * LANES] = rope(r[:, b * LANES:(b + 1) * LANES])
    for c in range(2):
        r = _nn(hb, w_ref[:, 2048 + c * 512:2560 + c * 512])
        gr_ref[:, c * 512:(c + 1) * 512] = r * _sigmoid(r)
    for c in range(2):
        vr_ref[:, c * 512:(c + 1) * 512] = _nn(hb, w_ref[:, 1024 + c * 512:1536 + c * 512]).astype(BF16)


def _in_proj_even(x2d, mods, layer, norm_g, w_bf, q_gain2, k_gain2, rope_tabs, *, n_batch, seq_len, latent):
    n = x2d.shape[0]
    tm = 512
    bidx = _batch_index_fn(latent, tm, seq_len)
    row = lambda i: (i, 0)
    const = lambda i: (0, 0)
    in_specs = [
        pl.BlockSpec((tm, D_MODEL), row),
        _mod_spec(layer, 0, bidx),
        _mod_spec(layer, 1, bidx),
        pl.BlockSpec((1, D_MODEL), const),
        pl.BlockSpec(w_bf.shape, const),
        pl.BlockSpec((1, LANES), const),
        pl.BlockSpec((1, LANES), const),
    ]
    args = [x2d, mods, mods, norm_g, w_bf, q_gain2, k_gain2]
    if latent:
        tiles_per_seq = seq_len // tm
        in_specs += [pl.BlockSpec((tm, LANES), lambda i: (i % tiles_per_seq, 0))] * 2
        args += list(rope_tabs)
    out_shape = [
        jax.ShapeDtypeStruct((n, 512), F32),
        jax.ShapeDtypeStruct((n, 512), F32),
        jax.ShapeDtypeStruct((n, 1024), BF16),
        jax.ShapeDtypeStruct((n, 1024), F32),
        jax.ShapeDtypeStruct((n, 512), BF16),
        jax.ShapeDtypeStruct((n, 256), BF16),
        jax.ShapeDtypeStruct((n, 256), BF16),
    ]
    out_specs = [pl.BlockSpec((tm, s.shape[1]), row) for s in out_shape]
    if not latent:
        cache = jax.ShapeDtypeStruct((n_batch, 1, GQA_KV_HEADS, HEAD_DIM, seq_len), F32)
        out_shape += [cache, cache]
        out_specs += [pl.BlockSpec((tm // seq_len, 1, GQA_KV_HEADS, HEAD_DIM, seq_len),
                                   lambda i: (i, 0, 0, 0, 0))] * 2
    return pl.pallas_call(
        functools.partial(_in_even_kernel, latent=latent, tm=tm),
        out_shape=out_shape,
        grid=(n // tm,),
        in_specs=in_specs,
        out_specs=out_specs,
        compiler_params=_params(1),
        name="in_proj_even_latent" if latent else "in_proj_even_ctx",
    )(*args)


def _retention_kernel(*refs, n, nb, has_state, write_state):
    lg_ref, q_ref, k_ref, v_ref, gr_ref, gn_ref = refs[:6]
    refs = refs[6:]
    if has_state:
        s0f_ref, s0b_ref = refs[:2]
        refs = refs[2:]
    o_ref = refs[0]
    if write_state:
        sf_ref, sb_ref = refs[1:3]
    c = RET_CHUNK
    assert c == LANES and 2 * RET_DK == LANES
    nc = n // c
    p = pl.program_id(0)
    lgf = [lg_ref[0, 2 * p + hh] for hh in range(2)]
    lgb = [lg_ref[1, 2 * p + hh] for hh in range(2)]
    row = lax.broadcasted_iota(jnp.int32, (c, c), 0)
    col = lax.broadcasted_iota(jnp.int32, (c, c), 1)
    diff = (row - col).astype(F32)
    pos = row.astype(F32)
    lane_lo = col < RET_DK

    def both_scans(f, b):
        return (jnp.where(diff >= 0, jnp.exp(f * jnp.maximum(diff, 0.0)), 0.0)
                + jnp.where(diff <= 0, jnp.exp(b * jnp.maximum(-diff, 0.0)), 0.0))

    decay2 = jnp.concatenate([both_scans(lgf[0], lgb[0]), both_scans(lgf[1], lgb[1])], axis=1)
    lgf_lane = jnp.where(lane_lo, lgf[0], lgf[1])
    lgb_lane = jnp.where(lane_lo, lgb[0], lgb[1])
    qd_f = jnp.exp(lgf_lane * (pos + 1.0))
    kd_f = jnp.exp(lgf_lane * (c - 1.0 - pos))
    qd_b = jnp.exp(lgb_lane * (c - pos))
    kd_b = jnp.exp(lgb_lane * pos)
    srow = lax.broadcasted_iota(jnp.int32, (c, 2 * RET_DV), 0)
    scol = lax.broadcasted_iota(jnp.int32, (c, 2 * RET_DV), 1)
    row_a = srow < RET_DK
    col_a = scol < RET_DV
    own = row_a == col_a
    cd_f = jnp.exp(jnp.where(row_a, lgf[0], lgf[1]) * float(c))
    cd_b = jnp.exp(jnp.where(row_a, lgb[0], lgb[1]) * float(c))
    zeros_half = jnp.zeros((RET_DK, RET_DV), F32)
    gn = gn_ref[...]

    def place(s0_ref, bi):
        top = jnp.concatenate([s0_ref[bi, 0, 0], zeros_half], axis=1)
        bot = jnp.concatenate([zeros_half, s0_ref[bi, 0, 1]], axis=1)
        return jnp.concatenate([top, bot], axis=0)

    for bi in range(nb):
        rows = [slice(bi * n + i * c, bi * n + (i + 1) * c) for i in range(nc)]
        if has_state:
            s_f = place(s0f_ref, bi)
            s_b = place(s0b_ref, bi)
        else:
            s_f = jnp.zeros((c, 2 * RET_DV), F32)
            s_b = jnp.zeros((c, 2 * RET_DV), F32)
        before_f = []
        for i in range(nc):
            before_f.append(s_f)
            kv = _tn((k_ref[rows[i], :] * kd_f).astype(BF16), v_ref[rows[i], :])
            s_f = s_f * cd_f + jnp.where(own, kv, 0.0)
        before_b = [None] * nc
        for i in reversed(range(nc)):
            before_b[i] = s_b
            kv = _tn((k_ref[rows[i], :] * kd_b).astype(BF16), v_ref[rows[i], :])
            s_b = s_b * cd_b + jnp.where(own, kv, 0.0)
        if write_state:
            for hh in range(2):
                blk = (slice(hh * RET_DK, (hh + 1) * RET_DK), slice(hh * RET_DV, (hh + 1) * RET_DV))
                sf_ref[bi, 0, hh] = s_f[blk]
                sb_ref[bi, 0, hh] = s_b[blk]
        for i in range(nc):
            qc = q_ref[rows[i], :]
            kc = k_ref[rows[i], :]
            vc = v_ref[rows[i], :]
            k_cat = jnp.concatenate([jnp.where(lane_lo, kc, 0.0), jnp.where(lane_lo, 0.0, kc)], axis=0)
            v_blk = jnp.concatenate([jnp.where(col_a, vc, jnp.zeros_like(vc)),
                                     jnp.where(col_a, jnp.zeros_like(vc), vc)], axis=0)
            scores = _nt(qc.astype(BF16), k_cat.astype(BF16)) * decay2
            q_cat = jnp.concatenate([(qc * qd_f).astype(BF16), (qc * qd_b).astype(BF16)], axis=1)
            s_cat = jnp.concatenate([before_f[i], before_b[i]], axis=0).astype(BF16)
            o = _nn(scores.astype(BF16), v_blk) + _nn(q_cat, s_cat)
            for hh in range(2):
                vcols = slice(hh * RET_DV, (hh + 1) * RET_DV)
                oh = o[:, vcols]
                mu = jnp.mean(oh, axis=-1, keepdims=True)
                d = oh - mu
                var = jnp.mean(d * d, axis=-1, keepdims=True)
                y = d * lax.rsqrt(var + GN_EPS) * gn[:, vcols] * gr_ref[rows[i], vcols]
                o_ref[rows[i], vcols] = y.astype(BF16)


def _retention(log_g, qr, kr, vr, gr, gn, state_f, state_b, *, n_batch, seq_len, write_state):
    n = qr.shape[0]
    pairs = RET_HEADS // 2
    has_state = state_f is not None
    nb = max(1, 1024 // seq_len)
    rows = nb * seq_len
    tok = lambda p, g: (g, p)
    in_specs = [
        pl.BlockSpec(memory_space=pltpu.SMEM),
        pl.BlockSpec((rows, LANES), tok),
        pl.BlockSpec((rows, LANES), tok),
        pl.BlockSpec((rows, 2 * RET_DV), tok),
        pl.BlockSpec((rows, 2 * RET_DV), tok),
        pl.BlockSpec((1, 2 * RET_DV), lambda p, g: (0, p)),
    ]
    args = [log_g, qr, kr, vr, gr, gn]
    state_spec = pl.BlockSpec((nb, 1, 2, RET_DK, RET_DV), lambda p, g: (g, 0, p, 0, 0))
    if has_state:
        in_specs += [state_spec, state_spec]
        args += [state_f, state_b]
    out_shape = [jax.ShapeDtypeStruct((n, RET_HEADS * RET_DV), BF16)]
    out_specs = [pl.BlockSpec((rows, 2 * RET_DV), tok)]
    if write_state:
        st = jax.ShapeDtypeStruct((n_batch, 1, RET_HEADS, RET_DK, RET_DV), F32)
        out_shape += [st, st]
        out_specs += [state_spec, state_spec]
    return pl.pallas_call(
        functools.partial(_retention_kernel, n=seq_len, nb=nb, has_state=has_state, write_state=write_state),
        out_shape=out_shape,
        grid=(pairs, n_batch // nb),
        in_specs=in_specs,
        out_specs=out_specs,
        compiler_params=_params(2),
        name="retention_latent" if has_state else "retention_ctx",
    )(*args)


def _gqa_kernel(*refs, n_src, tq):
    q_ref = refs[0]
    k_refs = refs[1:1 + 2 * n_src:2]
    v_refs = refs[2:2 + 2 * n_src:2]
    o_ref = refs[1 + 2 * n_src]
    units = [(g, half) for g in range(GQA_KV_HEADS) for half in range(2)]
    outs = {}

    def scores_of(unit):
        g, half = unit
        base = g * 2 * LANES
        q = jnp.concatenate([q_ref[:, base:base + LANES], q_ref[:, base + LANES:base + 2 * LANES]], axis=0)
        return [_nt(q, _masked_half(k_ref[:, g * LANES:(g + 1) * LANES], half)) for k_ref in k_refs]

    def finish(unit, scores):
        g, half = unit
        outs[half] = _softmax_apply(scores, [v_ref[:, g * LANES:(g + 1) * LANES] for v_ref in v_refs])
        if half == 1:
            base = g * 2 * LANES
            o = jnp.where(_lane_lo(2 * tq), outs[0], outs[1]).astype(BF16)
            o_ref[:, base:base + LANES] = o[:tq]
            o_ref[:, base + LANES:base + 2 * LANES] = o[tq:]

    _run_pipelined(units, scores_of, finish)


def _gqa_attention(qa, kd, vd, ctx_kd, ctx_vd, *, n_batch, seq_len, tq):
    n = qa.shape[0]
    tiles = seq_len // tq
    n_src = 1 if ctx_kd is None else 2
    qmap = lambda b, t: (b * tiles + t, 0)
    kmap = lambda b, t: (b, 0)
    in_specs = [pl.BlockSpec((tq, GQA_HEADS * HEAD_DIM), qmap),
                pl.BlockSpec((seq_len, 2 * LANES), kmap),
                pl.BlockSpec((seq_len, 2 * LANES), kmap)]
    args = [qa, kd, vd]
    if n_src == 2:
        past = ctx_kd.shape[1]
        cmap = lambda b, t: (b, 0, 0)
        in_specs += [pl.BlockSpec((None, past, 2 * LANES), cmap)] * 2
        args += [ctx_kd, ctx_vd]
    return pl.pallas_call(
        functools.partial(_gqa_kernel, n_src=n_src, tq=tq),
        out_shape=jax.ShapeDtypeStruct((n, GQA_HEADS * HEAD_DIM), BF16),
        grid=(n_batch, tiles),
        in_specs=in_specs,
        out_specs=pl.BlockSpec((tq, GQA_HEADS * HEAD_DIM), qmap),
        compiler_params=_params(2),
        name="gqa_latent" if n_src == 2 else "gqa_ctx",
    )(*args)


def _zero_rows(arr, rows):
    pieces, cur = [], 0
    sub = lax.broadcasted_iota(jnp.int32, (8, arr.shape[1]), 0)
    for r in sorted(rows):
        g0 = (r // 8) * 8
        if g0 > cur:
            pieces.append(arr[cur:g0])
        pieces.append(jnp.where(sub == r - g0, 0.0, arr[g0:g0 + 8]))
        cur = g0 + 8
    if cur < arr.shape[0]:
        pieces.append(arr[cur:])
    return jnp.concatenate(pieces, axis=0)


def _mix_ffn_kernel(*refs, tm, seq_len, final, n_mix):
    x_ref, gate_mix_ref, g_ref, scale_ref, shift_ref, gate_ref = refs[:6]
    m_refs = refs[6:6 + n_mix]
    w_refs = refs[6 + n_mix:6 + 2 * n_mix]
    wup_ref, cw_ref, cb_ref, wd_ref = refs[6 + 2 * n_mix:10 + 2 * n_mix]
    refs = refs[10 + 2 * n_mix:]
    if final:
        gfin_ref = refs[0]
        refs = refs[1:]
    o_ref, act_ref, hb_ref = refs
    halves = [slice(rb * (tm // 2), (rb + 1) * (tm // 2)) for rb in range(2)]
    for rows in halves:
        acc = None
        for m_ref, w_ref in zip(m_refs, w_refs):
            t = _nn(m_ref[rows, :], w_ref[...])
            acc = t if acc is None else acc + t
        y = x_ref[rows, :] + gate_mix_ref[...] * acc
        o_ref[rows, :] = y
        hb_ref[rows, :] = _modulated_norm(y, g_ref[...], scale_ref[...], shift_ref[...]).astype(BF16)
    seq_starts = list(range(0, tm, seq_len))
    seq_ends = [s + seq_len - 1 for s in seq_starts]
    for j in range(N_FF_CHUNKS):
        u = _nn(hb_ref[...], wup_ref[j])
        cw = cw_ref[j]
        prev = _zero_rows(pltpu.roll(u, 1, 0), seq_starts)
        nxt = _zero_rows(pltpu.roll(u, tm - 1, 0), seq_ends)
        cv = prev * cw[0:1] + u * cw[1:2] + nxt * cw[2:3] + cb_ref[j]
        a = cv[:, :FF_CHUNK]
        act_ref[:, j * FF_CHUNK:(j + 1) * FF_CHUNK] = (a * _sigmoid(a) * cv[:, FF_CHUNK:]).astype(BF16)
    for rows in halves:
        y = o_ref[rows, :] + gate_ref[...] * _nn(act_ref[rows, :], wd_ref[...])
        if final:
            ms = jnp.mean(y * y, axis=-1, keepdims=True)
            y = y * lax.rsqrt(ms + EPS) * gfin_ref[...]
        o_ref[rows, :] = y


def _mix_ffn(x2d, mods, layer, mixes, weights, ffn_norm_g, wup_c, cw_c, cb_c, wd_c, final_g, *, seq_len, latent):
    n = x2d.shape[0]
    tm = 1024
    bidx = _batch_index_fn(latent, tm, seq_len)
    row = lambda i: (i, 0)
    const2 = lambda i: (0, 0)
    once = pl.Buffered(1)
    resident = lambda a: pl.BlockSpec((None,) + a.shape[1:], lambda i: (layer,) + (0,) * (a.ndim - 1),
                                      pipeline_mode=once)
    in_specs = [pl.BlockSpec((tm, D_MODEL), row), _mod_spec(layer, 2, bidx), pl.BlockSpec((1, D_MODEL), const2),
                _mod_spec(layer, 4, bidx), _mod_spec(layer, 3, bidx), _mod_spec(layer, 5, bidx)]
    in_specs += [pl.BlockSpec((tm, m.shape[1]), row) for m in mixes]
    in_specs += [pl.BlockSpec(w.shape, const2, pipeline_mode=once) for w in weights]
    in_specs += [resident(wup_c), resident(cw_c), resident(cb_c), resident(wd_c)]
    args = [x2d, mods, ffn_norm_g, mods, mods, mods, *mixes, *weights, wup_c, cw_c, cb_c, wd_c]
    final = final_g is not None
    if final:
        in_specs.append(pl.BlockSpec((1, D_MODEL), const2))
        args.append(final_g)
    return pl.pallas_call(
        functools.partial(_mix_ffn_kernel, tm=tm, seq_len=seq_len, final=final, n_mix=len(mixes)),
        out_shape=jax.ShapeDtypeStruct((n, D_MODEL), F32),
        grid=(n // tm,),
        in_specs=in_specs,
        out_specs=pl.BlockSpec((tm, D_MODEL), row),
        scratch_shapes=[pltpu.VMEM((tm, D_FF), BF16), pltpu.VMEM((tm, D_MODEL), BF16)],
        compiler_params=_params(1),
        name="mix_ffn_latent" if latent else "mix_ffn_ctx",
    )(*args)


def _regroup_up_kernel(a_ref, g_ref, o_ref):
    o_ref[:, :FF_CHUNK] = a_ref[...].astype(BF16)
    o_ref[:, FF_CHUNK:] = g_ref[...].astype(BF16)


def _regroup_w_up(w_up):
    depth = w_up.shape[0]
    return pl.pallas_call(
        _regroup_up_kernel,
        out_shape=jax.ShapeDtypeStruct((depth, N_FF_CHUNKS, D_MODEL, 2 * FF_CHUNK), BF16),
        grid=(depth, N_FF_CHUNKS),
        in_specs=[pl.BlockSpec((None, D_MODEL, FF_CHUNK), lambda l, j: (l, 0, j)),
                  pl.BlockSpec((None, D_MODEL, FF_CHUNK), lambda l, j: (l, 0, N_FF_CHUNKS + j))],
        out_specs=pl.BlockSpec((None, None, D_MODEL, 2 * FF_CHUNK), lambda l, j: (l, j, 0, 0)),
        compiler_params=_params(2),
        name="regroup_w_up",
    )(w_up, w_up)


def _in_odd_kernel(*refs, write_cache, tm):
    x_ref, shift_ref, scale_ref, g_ref, w_ref, q_ref, k_ref, v_ref = refs[:8]
    hb = _modulated_norm(x_ref[...], g_ref[...], scale_ref[...], shift_ref[...]).astype(BF16)
    width = NA_HEADS * HEAD_DIM
    q_scale = HEAD_DIM ** -0.5 * LOG2E
    for which, dst in ((1, k_ref), (2, v_ref)):
        for c in range(2):
            r = _nn(hb, w_ref[:, which * width + c * 512:which * width + (c + 1) * 512])
            dst[:, c * 512:(c + 1) * 512] = r.astype(BF16)
            if write_cache:
                cache_ref = refs[8 + which - 1]
                seq = cache_ref.shape[4]
                for bb in range(tm // seq):
                    for blk in range(512 // LANES):
                        t = r[bb * seq:(bb + 1) * seq, blk * LANES:(blk + 1) * LANES].T
                        for hh in range(2):
                            head = c * (512 // HEAD_DIM) + 2 * blk + hh
                            cache_ref[bb, 0, head] = t[hh * HEAD_DIM:(hh + 1) * HEAD_DIM]
    for c in range(2):
        cols = slice(c * 512, (c + 1) * 512)
        q_ref[:, cols] = (_nn(hb, w_ref[:, c * 512:(c + 1) * 512]) * q_scale).astype(BF16)


def _in_proj_odd(x2d, mods, layer, norm_g, w_bf, *, n_batch, seq_len, latent):
    n = x2d.shape[0]
    tm = 512
    width = NA_HEADS * HEAD_DIM
    bidx = _batch_index_fn(latent, tm, seq_len)
    row = lambda i: (i, 0)
    const = lambda i: (0, 0)
    in_specs = [pl.BlockSpec((tm, D_MODEL), row), _mod_spec(layer, 0, bidx), _mod_spec(layer, 1, bidx),
                pl.BlockSpec((1, D_MODEL), const), pl.BlockSpec(w_bf.shape, const)]
    out_shape = [jax.ShapeDtypeStruct((n, width), BF16)] * 3
    out_specs = [pl.BlockSpec((tm, width), row)] * 3
    write_cache = not latent
    if write_cache:
        cache = jax.ShapeDtypeStruct((n_batch, 1, NA_HEADS, HEAD_DIM, seq_len), F32)
        out_shape += [cache, cache]
        out_specs += [pl.BlockSpec((tm // seq_len, 1, NA_HEADS, HEAD_DIM, seq_len),
                                   lambda i: (i, 0, 0, 0, 0))] * 2
    return pl.pallas_call(
        functools.partial(_in_odd_kernel, write_cache=write_cache, tm=tm),
        out_shape=out_shape,
        grid=(n // tm,),
        in_specs=in_specs,
        out_specs=out_specs,
        compiler_params=_params(1),
        name="in_proj_odd_latent" if latent else "in_proj_odd_ctx",
    )(x2d, mods, mods, norm_g, w_bf)


def _dense_pairs_kernel(q_ref, k_ref, v_ref, o_ref, *, seq_len):
    n_seq = q_ref.shape[0] // seq_len
    units = [(b, p, half) for b in range(n_seq) for p in range(NA_HEADS // 2) for half in range(2)]
    outs = {}

    def scores_of(unit):
        b, p, half = unit
        rows, cols = slice(b * seq_len, (b + 1) * seq_len), slice(p * LANES, (p + 1) * LANES)
        return [_nt(q_ref[rows, cols], _masked_half(k_ref[rows, cols], half))]

    def finish(unit, scores):
        b, p, half = unit
        rows, cols = slice(b * seq_len, (b + 1) * seq_len), slice(p * LANES, (p + 1) * LANES)
        outs[half] = _softmax_apply(scores, [v_ref[rows, cols]])
        if half == 1:
            o_ref[rows, cols] = jnp.where(_lane_lo(seq_len), outs[0], outs[1]).astype(BF16)

    _run_pipelined(units, scores_of, finish)


def _dense_attention_ctx(q, k, v, *, n_batch, seq_len):
    width = NA_HEADS * HEAD_DIM
    n_seq = 2
    spec = pl.BlockSpec((n_seq * seq_len, width), lambda b: (b, 0))
    return pl.pallas_call(
        functools.partial(_dense_pairs_kernel, seq_len=seq_len),
        out_shape=jax.ShapeDtypeStruct(q.shape, BF16),
        grid=(n_batch // n_seq,),
        in_specs=[spec, spec, spec],
        out_specs=spec,
        compiler_params=_params(1),
        name="dense_attention_ctx",
    )(q, k, v)


NA_Q_ROWS = 8
NA_KEY_ROWS = 12


def _na_window_start(r):
    return min(max(r - NA_KH // 2, 0), (1024 // GRID_W) - NA_KH)


def _na_bias_tile(bias_ref, half, tile, n_rows):
    r0 = tile * NA_Q_ROWS
    kr0 = min(max(r0 - NA_KH // 2, 0), n_rows - NA_KEY_ROWS)
    lo = _lane_lo(GRID_W)
    neg_block = jnp.full((GRID_W, LANES), NEG, F32)
    left_off = jnp.where(lo, NEG, 0.0)
    right_off = jnp.where(lo, 0.0, NEG)
    rows = []
    for rq in range(NA_Q_ROWS):
        r = r0 + rq
        rs = _na_window_start(r)
        blocks = []
        for kk in range(NA_KEY_ROWS // 2):
            ka = kr0 + 2 * kk
            va = rs <= ka < rs + NA_KH
            vb = rs <= ka + 1 < rs + NA_KH
            if not (va or vb):
                blocks.append(neg_block)
                continue
            blk = bias_ref[half, ka - r + NA_KH]
            if not va:
                blk = blk + left_off
            if not vb:
                blk = blk + right_off
            blocks.append(blk)
        rows.append(jnp.concatenate(blocks, axis=1))
    return jnp.concatenate(rows, axis=0), kr0


def _na_kernel(q_ref, k_ref, v_ref, ck_ref, cv_ref, bias_ref, o_ref, *, n_rows):
    tq = NA_Q_ROWS * GRID_W
    span = NA_KEY_ROWS * GRID_W
    units = [(tile, half) for tile in range(n_rows // NA_Q_ROWS) for half in range(2)]
    outs = {}

    def window(tile):
        kr0 = min(max(tile * NA_Q_ROWS - NA_KH // 2, 0), n_rows - NA_KEY_ROWS)
        return slice(kr0 * GRID_W, kr0 * GRID_W + span)

    def scores_of(unit):
        tile, half = unit
        q = q_ref[tile * tq:(tile + 1) * tq, :]
        bias, _ = _na_bias_tile(bias_ref, half, tile, n_rows)
        ck_t = jnp.concatenate([ck_ref[0], ck_ref[1]], axis=0)
        own = (lax.broadcasted_iota(jnp.int32, ck_t.shape, 0) < HEAD_DIM) == (half == 0)
        return [_nt(q, _masked_half(k_ref[window(tile), :], half)) + bias,
                _nn(q, jnp.where(own, ck_t, 0.0).astype(BF16))]

    def finish(unit, scores):
        tile, half = unit
        cv_t = jnp.concatenate([cv_ref[0], cv_ref[1]], axis=0).astype(BF16)
        outs[half] = _softmax_apply(scores, [v_ref[window(tile), :], ("t", cv_t)])
        if half == 1:
            o_ref[tile * tq:(tile + 1) * tq, :] = jnp.where(_lane_lo(tq), outs[0], outs[1]).astype(BF16)

    _run_pipelined(units, scores_of, finish)


def _na_bias_table(rpb):
    cidx = np.arange(GRID_W)
    cs = np.clip(cidx - NA_KW // 2, 0, GRID_W - NA_KW)
    kc = np.arange(GRID_W)
    inside = (kc[None, :] >= cs[:, None]) & (kc[None, :] < cs[:, None] + NA_KW)
    rel = kc[None, :] - cidx[:, None] + NA_KW - 1
    onehot = (rel[None] == np.arange(2 * NA_KW - 1)[:, None, None]) & inside[None]
    m = jnp.einsum("hdj,jck->hdck", rpb * LOG2E, jnp.asarray(onehot, F32), precision=lax.Precision.HIGHEST)
    m = jnp.where(jnp.asarray(inside)[None, None], m, NEG)
    neg = jnp.full((rpb.shape[0], 1, GRID_W, GRID_W), NEG, F32)
    left = jnp.concatenate([neg, m], axis=1)
    right = jnp.concatenate([m, neg], axis=1)
    return jnp.concatenate([left, right], axis=-1)


def _na_attention(q, k, v, ctx_k, ctx_v, bias_tab, *, n_batch, seq_len):
    pairs = NA_HEADS // 2
    tok = lambda p, b: (b, p)
    ctx = lambda p, b: (b, p, 0, 0)
    past = ctx_k.shape[-1]
    return pl.pallas_call(
        functools.partial(_na_kernel, n_rows=seq_len // GRID_W),
        out_shape=jax.ShapeDtypeStruct(q.shape, BF16),
        grid=(pairs, n_batch),
        in_specs=[pl.BlockSpec((seq_len, LANES), tok)] * 3
        + [pl.BlockSpec((None, 2, HEAD_DIM, past), ctx)] * 2
        + [pl.BlockSpec((2, 2 * NA_KH, GRID_W, LANES), lambda p, b: (p, 0, 0, 0))],
        out_specs=pl.BlockSpec((seq_len, LANES), tok),
        compiler_params=_params(2),
        name="neighbourhood_attention",
    )(q, k, v, ctx_k, ctx_v, bias_tab)


def _rope_tables(n):
    t = np.arange(n)
    row = (t // GRID_W).astype(np.float64)
    col = (t % GRID_W).astype(np.float64)
    half = HEAD_DIM // 2
    inv = ROPE_BASE ** (-np.arange(0, half, 2, dtype=np.float64) / half)
    ang_r = row[:, None] * inv
    ang_c = col[:, None] * inv
    cos_h = np.concatenate([np.cos(ang_r)] * 2 + [np.cos(ang_c)] * 2, axis=-1)
    sin_h = np.concatenate([-np.sin(ang_r), np.sin(ang_r), -np.sin(ang_c), np.sin(ang_c)], axis=-1)
    return (jnp.asarray(np.concatenate([cos_h, cos_h], axis=-1), F32),
            jnp.asarray(np.concatenate([sin_h, sin_h], axis=-1), F32))


def _ffn_weights(w_up, conv_w, conv_b, w_down):
    depth = w_up.shape[0]

    def regroup(a):
        lead = a.shape[1:-1]
        a = a.reshape(depth, *lead, 2, N_FF_CHUNKS, FF_CHUNK)
        a = jnp.moveaxis(a, -2, 1)
        return a.reshape(depth, N_FF_CHUNKS, *lead, 2 * FF_CHUNK)
    return (_regroup_w_up(w_up), regroup(conv_w), regroup(conv_b[:, None, :]),
            w_down.astype(BF16))


def _token_major_dup(cache):
    b, kv, t, d = cache.shape
    c = jnp.transpose(cache, (0, 2, 1, 3))[:, :, :, None, :]
    return jnp.broadcast_to(c, (b, t, kv, 2, d)).reshape(b, t, kv * 2 * d).astype(BF16)


def _head_transposed(cache):
    return jnp.swapaxes(cache, -1, -2)


def kernel(x_prompt, x_sample, state_ret_fwd, state_ret_bwd, cache_gqa_k, cache_gqa_v, cache_na_k, cache_na_v,
           c, c_ctx, ada_w, ada_b, norm_mix, norm_ffn, norm_final, even_w_in, even_w_out, ret_decay_fwd,
           ret_decay_bwd, ret_gn, gqa_q_norm, gqa_k_norm, odd_w_in, odd_w_out, na_rpb, ffn_w_up, ffn_conv_w,
           ffn_conv_b, ffn_w_down):
    nb_c, len_c, _ = x_prompt.shape
    nb_s, len_s, _ = x_sample.shape
    depth = ada_w.shape[0]
    streams = {
        False: dict(n_batch=nb_c, seq_len=len_c),
        True: dict(n_batch=nb_s, seq_len=len_s),
    }
    xs = {False: x_prompt.reshape(nb_c * len_c, D_MODEL), True: x_sample.reshape(nb_s * len_s, D_MODEL)}

    rows = 8 * (-(-(1 + nb_s) // 8))
    cond = jnp.zeros((rows, D_MODEL), F32).at[0].set(c_ctx).at[1:1 + nb_s].set(c)
    mods = _ada_params(cond, ada_w, ada_b).reshape(depth, rows, 6, 1, D_MODEL)
    rope_tabs = _rope_tables(len_s)
    ffn_w = _ffn_weights(ffn_w_up, ffn_conv_w, ffn_conv_b, ffn_w_down)
    outs = {}
    mixed = {}

    for l in range(depth):
        g_mix = norm_mix[l][None, :]
        g_ffn = norm_ffn[l][None, :]
        if l % 2 == 0:
            e = l // 2
            w_in = even_w_in[e].astype(BF16)
            w_out = even_w_out[e].astype(BF16)
            w_out_parts = [w_out[:RET_HEADS * RET_DV], w_out[RET_HEADS * RET_DV:]]
            log_g = jnp.stack([jax.nn.log_sigmoid(ret_decay_fwd[e].astype(F32)),
                               jax.nn.log_sigmoid(ret_decay_bwd[e].astype(F32))])
            gn = ret_gn[e][None, :]
            qg2 = jnp.tile(gqa_q_norm[e], 2)[None, :]
            kg2 = jnp.tile(gqa_k_norm[e], 2)[None, :]
            for latent in (False, True):
                st = streams[latent]
                res = _in_proj_even(xs[latent], mods, l, g_mix, w_in, qg2, kg2, rope_tabs, latent=latent, **st)
                qr, kr, vr, gr, qa, kd, vd = res[:7]
                if latent:
                    ret = _retention(log_g, qr, kr, vr, gr, gn, state_ret_fwd[:, e:e + 1],
                                     state_ret_bwd[:, e:e + 1], write_state=False, **st)[0]
                    att = _gqa_attention(qa, kd, vd, _token_major_dup(cache_gqa_k[:, e]),
                                         _token_major_dup(cache_gqa_v[:, e]), tq=512, **st)
                else:
                    outs.setdefault("gk", []).append(res[7])
                    outs.setdefault("gv", []).append(res[8])
                    ret, s_f, s_b = _retention(log_g, qr, kr, vr, gr, gn, None, None, write_state=True, **st)
                    outs.setdefault("sf", []).append(s_f)
                    outs.setdefault("sb", []).append(s_b)
                    att = _gqa_attention(qa, kd, vd, None, None, tq=st["seq_len"], **st)
                mixed[latent] = ([ret, att], w_out_parts)
        else:
            o = l // 2
            w_in = odd_w_in[o].astype(BF16)
            w_out = odd_w_out[o].astype(BF16)
            for latent in (False, True):
                st = streams[latent]
                res = _in_proj_odd(xs[latent], mods, l, g_mix, w_in, latent=latent, **st)
                q, k, v = res[:3]
                if latent:
                    att = _na_attention(q, k, v, _head_transposed(cache_na_k[:, o]), _head_transposed(cache_na_v[:, o]),
                                        _na_bias_table(na_rpb[o]), **st)
                else:
                    outs.setdefault("nk", []).append(res[3])
                    outs.setdefault("nv", []).append(res[4])
                    att = _dense_attention_ctx(q, k, v, **st)
                mixed[latent] = ([att], [w_out])
        final_g = norm_final[None, :] if l == depth - 1 else None
        for latent in (False, True):
            xs[latent] = _mix_ffn(xs[latent], mods, l, *mixed[latent], g_ffn, *ffn_w, final_g,
                                   seq_len=streams[latent]["seq_len"], latent=latent)

    tr = lambda a: jnp.swapaxes(a, -1, -2)
    cat = lambda name: outs[name][0] if len(outs[name]) == 1 else jnp.concatenate(outs[name], axis=1)
    return (xs[False].reshape(nb_c, len_c, D_MODEL), xs[True].reshape(nb_s, len_s, D_MODEL),
            cat("sf"), cat("sb"), tr(cat("gk")), tr(cat("gv")), tr(cat("nk")), tr(cat("nv")))
```

```python
import functools

import numpy as np
import jax
import jax.numpy as jnp
from jax import lax
from jax.experimental import pallas as pl
from jax.experimental.pallas import tpu as pltpu

F32 = jnp.float32
BF16 = jnp.bfloat16

D_MODEL = 1024
GRID_W = 64
HEAD_DIM = 64
ROPE_BASE = 10000.0
EPS = 1e-6
GN_EPS = 1e-5
RET_HEADS = 8
RET_DK = 64
RET_DV = 128
RET_CHUNK = 128
GQA_HEADS = 8
GQA_KV_HEADS = 2
NA_HEADS = 16
NA_KH = 8
NA_KW = 16
D_FF = 2816
LANES = 128
MXU_DIM = 256
FF_CHUNK = MXU_DIM
N_FF_CHUNKS = D_FF // FF_CHUNK
NEG = -1e30
LOG2E = 1.4426950408889634
VMEM_LIMIT = 56 * 1024 * 1024


def _nn(a, b):
    return jnp.dot(a, b, preferred_element_type=F32)


def _nt(a, b):
    return lax.dot_general(a, b, (((1,), (1,)), ((), ())), preferred_element_type=F32)


def _tn(a, b):
    return lax.dot_general(a, b, (((0,), (0,)), ((), ())), preferred_element_type=F32)


def _sigmoid(x):
    return 1.0 / (1.0 + jnp.exp(-x))


def _params(n_axes):
    return pltpu.CompilerParams(dimension_semantics=("arbitrary",) * n_axes, vmem_limit_bytes=VMEM_LIMIT)


def _modulated_norm(x, g, scale, shift):
    ms = jnp.mean(x * x, axis=-1, keepdims=True)
    return (x * lax.rsqrt(ms + EPS) * g) * (1.0 + scale) + shift


def _lane_lo(rows):
    return lax.broadcasted_iota(jnp.int32, (rows, LANES), 1) < HEAD_DIM


def _head_rms_norm(xb, gain, lo):
    sq = xb * xb
    s_lo = jnp.sum(jnp.where(lo, sq, 0.0), axis=-1, keepdims=True)
    s_hi = jnp.sum(jnp.where(lo, 0.0, sq), axis=-1, keepdims=True)
    r = jnp.where(lo, lax.rsqrt(s_lo * (1.0 / HEAD_DIM) + EPS), lax.rsqrt(s_hi * (1.0 / HEAD_DIM) + EPS))
    return xb * r * gain


def _rope(xb, cos, sin_signed, first16):
    partner = jnp.where(first16, pltpu.roll(xb, LANES - 16, 1), pltpu.roll(xb, 16, 1))
    return xb * cos + partner * sin_signed


def _softmax_apply(scores, values):
    m = functools.reduce(jnp.maximum, [jnp.max(s, axis=-1, keepdims=True) for s in scores])
    es = [jnp.exp2(s - m) for s in scores]
    l = functools.reduce(jnp.add, [jnp.sum(e, axis=-1, keepdims=True) for e in es])
    o = None
    for e, v in zip(es, values):
        t = _nt(e.astype(BF16), v[1]) if isinstance(v, tuple) else _nn(e.astype(BF16), v)
        o = t if o is None else o + t
    return o * (1.0 / l)


def _run_pipelined(units, scores_of, finish):
    pending = scores_of(units[0])
    for n, unit in enumerate(units):
        nxt = scores_of(units[n + 1]) if n + 1 < len(units) else None
        finish(unit, pending)
        pending = nxt


def _masked_half(kp, half):
    lo = _lane_lo(kp.shape[0])
    return jnp.where(lo if half == 0 else jnp.logical_not(lo), kp, jnp.zeros_like(kp))


def _ada_kernel(c_ref, w_ref, b_ref, o_ref):
    c = c_ref[...]
    a = (c * _sigmoid(c)).astype(BF16)
    o_ref[...] = _nn(a, w_ref[...].astype(BF16)) + b_ref[...]


def _ada_params(cond, ada_w, ada_b):
    depth = ada_w.shape[0]
    rows = cond.shape[0]
    tn = 2048
    return pl.pallas_call(
        _ada_kernel,
        out_shape=jax.ShapeDtypeStruct((depth, rows, 6 * D_MODEL), F32),
        grid=(depth, 6 * D_MODEL // tn),
        in_specs=[
            pl.BlockSpec((rows, D_MODEL), lambda l, j: (0, 0)),
            pl.BlockSpec((None, D_MODEL, tn), lambda l, j: (l, 0, j)),
            pl.BlockSpec((None, 1, tn), lambda l, j: (l, 0, j)),
        ],
        out_specs=pl.BlockSpec((None, rows, tn), lambda l, j: (l, 0, j)),
        compiler_params=_params(2),
        name="ada_params",
    )(cond, ada_w, ada_b.reshape(depth, 1, 6 * D_MODEL))


def _mod_spec(layer, which, bidx):
    return pl.BlockSpec((None, None, None, 1, D_MODEL), lambda i: (layer, bidx(i), which, 0, 0))


def _batch_index_fn(latent, rows_per_tile, seq_len):
    if not latent:
        return lambda i: 0
    return lambda i: 1 + (i * rows_per_tile) // seq_len


def _in_even_kernel(*refs, latent, tm):
    if latent:
        (x_ref, shift_ref, scale_ref, g_ref, w_ref, qg_ref, kg_ref, cos_ref, sin_ref,
         qr_ref, kr_ref, vr_ref, gr_ref, qa_ref, kd_ref, vd_ref) = refs
    else:
        (x_ref, shift_ref, scale_ref, g_ref, w_ref, qg_ref, kg_ref,
         qr_ref, kr_ref, vr_ref, gr_ref, qa_ref, kd_ref, vd_ref, ck_ref, cv_ref) = refs
    hb = _modulated_norm(x_ref[...], g_ref[...], scale_ref[...], shift_ref[...]).astype(BF16)
    lane = lax.broadcasted_iota(jnp.int32, (tm, LANES), 1)
    lo = lane < HEAD_DIM
    if latent:
        cos = cos_ref[...]
        sin = sin_ref[...]
        first16 = (lane % 32) < 16
        rope = lambda v: _rope(v, cos, sin, first16)
    else:
        rope = lambda v: v
    dk_scale = RET_DK ** -0.5
    q_scale = HEAD_DIM ** -0.5 * LOG2E

    r = _nn(hb, w_ref[:, 3072:3584])
    qg = qg_ref[...]
    for b in range(4):
        blk = rope(_head_rms_norm(r[:, b * LANES:(b + 1) * LANES], qg, lo)) * q_scale
        qa_ref[:, b * LANES:(b + 1) * LANES] = blk.astype(BF16)
    r = _nn(hb, w_ref[:, 3584:3840])
    kn = _head_rms_norm(r[:, 0:LANES], kg_ref[...], lo)
    vn = r[:, LANES:2 * LANES]
    if not latent:
        seq = ck_ref.shape[4]
        for bb in range(tm // seq):
            for src, dst in ((kn, ck_ref), (vn, cv_ref)):
                t = src[bb * seq:(bb + 1) * seq, :].T
                for kv in range(GQA_KV_HEADS):
                    dst[bb, 0, kv] = t[kv * HEAD_DIM:(kv + 1) * HEAD_DIM]
    kn = rope(kn)
    for src, dst in ((kn, kd_ref), (vn, vd_ref)):
        sw = pltpu.roll(src, HEAD_DIM, 1)
        dst[:, 0:LANES] = jnp.where(lo, src, sw).astype(BF16)
        dst[:, LANES:2 * LANES] = jnp.where(lo, sw, src).astype(BF16)
    r = _nn(hb, w_ref[:, 0:512])
    for b in range(4):
        qr_ref[:, b * LANES:(b + 1) * LANES] = rope(r[:, b * LANES:(b + 1) * LANES])
    r = _nn(hb, w_ref[:, 512:1024]) * dk_scale
    for b in range(4):
        kr_ref[:, b * LANES:(b + 1) * LANES] = rope(r[:, b * LANES:(b + 1) * LANES])
    for c in range(2):
        r = _nn(hb, w_ref[:, 2048 + c * 512:2560 + c * 512])
        gr_ref[:, c * 512:(c + 1) * 512] = r * _sigmoid(r)
    for c in range(2):
        vr_ref[:, c * 512:(c + 1) * 512] = _nn(hb, w_ref[:, 1024 + c * 512:1536 + c * 512]).astype(BF16)


def _in_proj_even(x2d, mods, layer, norm_g, w_bf, q_gain2, k_gain2, rope_tabs, *, n_batch, seq_len, latent):
    n = x2d.shape[0]
    tm = 512
    bidx = _batch_index_fn(latent, tm, seq_len)
    row = lambda i: (i, 0)
    const = lambda i: (0, 0)
    in_specs = [
        pl.BlockSpec((tm, D_MODEL), row),
        _mod_spec(layer, 0, bidx),
        _mod_spec(layer, 1, bidx),
        pl.BlockSpec((1, D_MODEL), const),
        pl.BlockSpec(w_bf.shape, const),
        pl.BlockSpec((1, LANES), const),
        pl.BlockSpec((1, LANES), const),
    ]
    args = [x2d, mods, mods, norm_g, w_bf, q_gain2, k_gain2]
    if latent:
        tiles_per_seq = seq_len // tm
        in_specs += [pl.BlockSpec((tm, LANES), lambda i: (i % tiles_per_seq, 0))] * 2
        args += list(rope_tabs)
    out_shape = [
        jax.ShapeDtypeStruct((n, 512), F32),
        jax.ShapeDtypeStruct((n, 512), F32),
        jax.ShapeDtypeStruct((n, 1024), BF16),
        jax.ShapeDtypeStruct((n, 1024), F32),
        jax.ShapeDtypeStruct((n, 512), BF16),
        jax.ShapeDtypeStruct((n, 256), BF16),
        jax.ShapeDtypeStruct((n, 256), BF16),
    ]
    out_specs = [pl.BlockSpec((tm, s.shape[1]), row) for s in out_shape]
    if not latent:
        cache = jax.ShapeDtypeStruct((n_batch, 1, GQA_KV_HEADS, HEAD_DIM, seq_len), F32)
        out_shape += [cache, cache]
        out_specs += [pl.BlockSpec((tm // seq_len, 1, GQA_KV_HEADS, HEAD_DIM, seq_len),
                                   lambda i: (i, 0, 0, 0, 0))] * 2
    return pl.pallas_call(
        functools.partial(_in_even_kernel, latent=latent, tm=tm),
        out_shape=out_shape,
        grid=(n // tm,),
        in_specs=in_specs,
        out_specs=out_specs,
        compiler_params=_params(1),
        name="in_proj_even_latent" if latent else "in_proj_even_ctx",
    )(*args)


def _retention_kernel(*refs, n, nb, has_state, write_state):
    lg_ref, q_ref, k_ref, v_ref, gr_ref, gn_ref = refs[:6]
    refs = refs[6:]
    if has_state:
        s0f_ref, s0b_ref = refs[:2]
        refs = refs[2:]
    o_ref = refs[0]
    if write_state:
        sf_ref, sb_ref = refs[1:3]
    c = RET_CHUNK
    assert c == LANES and 2 * RET_DK == LANES
    nc = n // c
    p = pl.program_id(0)
    lgf = [lg_ref[0, 2 * p + hh] for hh in range(2)]
    lgb = [lg_ref[1, 2 * p + hh] for hh in range(2)]
    row = lax.broadcasted_iota(jnp.int32, (c, c), 0)
    col = lax.broadcasted_iota(jnp.int32, (c, c), 1)
    diff = (row - col).astype(F32)
    pos = row.astype(F32)
    lane_lo = col < RET_DK

    def both_scans(f, b):
        return (jnp.where(diff >= 0, jnp.exp(f * jnp.maximum(diff, 0.0)), 0.0)
                + jnp.where(diff <= 0, jnp.exp(b * jnp.maximum(-diff, 0.0)), 0.0))

    decay2 = jnp.concatenate([both_scans(lgf[0], lgb[0]), both_scans(lgf[1], lgb[1])], axis=1)
    lgf_lane = jnp.where(lane_lo, lgf[0], lgf[1])
    lgb_lane = jnp.where(lane_lo, lgb[0], lgb[1])
    qd_f = jnp.exp(lgf_lane * (pos + 1.0))
    kd_f = jnp.exp(lgf_lane * (c - 1.0 - pos))
    qd_b = jnp.exp(lgb_lane * (c - pos))
    kd_b = jnp.exp(lgb_lane * pos)
    srow = lax.broadcasted_iota(jnp.int32, (c, 2 * RET_DV), 0)
    scol = lax.broadcasted_iota(jnp.int32, (c, 2 * RET_DV), 1)
    row_a = srow < RET_DK
    col_a = scol < RET_DV
    own = row_a == col_a
    cd_f = jnp.exp(jnp.where(row_a, lgf[0], lgf[1]) * float(c))
    cd_b = jnp.exp(jnp.where(row_a, lgb[0], lgb[1]) * float(c))
    zeros_half = jnp.zeros((RET_DK, RET_DV), F32)
    gn = gn_ref[...]

    def place(s0_ref, bi):
        top = jnp.concatenate([s0_ref[bi, 0, 0], zeros_half], axis=1)
        bot = jnp.concatenate([zeros_half, s0_ref[bi, 0, 1]], axis=1)
        return jnp.concatenate([top, bot], axis=0)

    for bi in range(nb):
        rows = [slice(bi * n + i * c, bi * n + (i + 1) * c) for i in range(nc)]
        if has_state:
            s_f = place(s0f_ref, bi)
            s_b = place(s0b_ref, bi)
        else:
            s_f = jnp.zeros((c, 2 * RET_DV), F32)
            s_b = jnp.zeros((c, 2 * RET_DV), F32)
        before_f = []
        for i in range(nc):
            before_f.append(s_f)
            kv = _tn((k_ref[rows[i], :] * kd_f).astype(BF16), v_ref[rows[i], :])
            s_f = s_f * cd_f + jnp.where(own, kv, 0.0)
        before_b = [None] * nc
        for i in reversed(range(nc)):
            before_b[i] = s_b
            kv = _tn((k_ref[rows[i], :] * kd_b).astype(BF16), v_ref[rows[i], :])
            s_b = s_b * cd_b + jnp.where(own, kv, 0.0)
        if write_state:
            for hh in range(2):
                blk = (slice(hh * RET_DK, (hh + 1) * RET_DK), slice(hh * RET_DV, (hh + 1) * RET_DV))
                sf_ref[bi, 0, hh] = s_f[blk]
                sb_ref[bi, 0, hh] = s_b[blk]
        for i in range(nc):
            qc = q_ref[rows[i], :]
            kc = k_ref[rows[i], :]
            vc = v_ref[rows[i], :]
            k_cat = jnp.concatenate([jnp.where(lane_lo, kc, 0.0), jnp.where(lane_lo, 0.0, kc)], axis=0)
            v_blk = jnp.concatenate([jnp.where(col_a, vc, jnp.zeros_like(vc)),
                                     jnp.where(col_a, jnp.zeros_like(vc), vc)], axis=0)
            scores = _nt(qc.astype(BF16), k_cat.astype(BF16)) * decay2
            q_cat = jnp.concatenate([(qc * qd_f).astype(BF16), (qc * qd_b).astype(BF16)], axis=1)
            s_cat = jnp.concatenate([before_f[i], before_b[i]], axis=0).astype(BF16)
            o = _nn(scores.astype(BF16), v_blk) + _nn(q_cat, s_cat)
            for hh in range(2):
                vcols = slice(hh * RET_DV, (hh + 1) * RET_DV)
                oh = o[:, vcols]
                mu = jnp.mean(oh, axis=-1, keepdims=True)
                d = oh - mu
                var = jnp.mean(d * d, axis=-1, keepdims=True)
                y = d * lax.rsqrt(var + GN_EPS) * gn[:, vcols] * gr_ref[rows[i], vcols]
                o_ref[rows[i], vcols] = y.astype(BF16)


def _retention(log_g, qr, kr, vr, gr, gn, state_f, state_b, *, n_batch, seq_len, write_state):
    n = qr.shape[0]
    pairs = RET_HEADS // 2
    has_state = state_f is not None
    nb = max(1, 1024 // seq_len)
    rows = nb * seq_len
    tok = lambda p, g: (g, p)
    in_specs = [
        pl.BlockSpec(memory_space=pltpu.SMEM),
        pl.BlockSpec((rows, LANES), tok),
        pl.BlockSpec((rows, LANES), tok),
        pl.BlockSpec((rows, 2 * RET_DV), tok),
        pl.BlockSpec((rows, 2 * RET_DV), tok),
        pl.BlockSpec((1, 2 * RET_DV), lambda p, g: (0, p)),
    ]
    args = [log_g, qr, kr, vr, gr, gn]
    state_spec = pl.BlockSpec((nb, 1, 2, RET_DK, RET_DV), lambda p, g: (g, 0, p, 0, 0))
    if has_state:
        in_specs += [state_spec, state_spec]
        args += [state_f, state_b]
    out_shape = [jax.ShapeDtypeStruct((n, RET_HEADS * RET_DV), BF16)]
    out_specs = [pl.BlockSpec((rows, 2 * RET_DV), tok)]
    if write_state:
        st = jax.ShapeDtypeStruct((n_batch, 1, RET_HEADS, RET_DK, RET_DV), F32)
        out_shape += [st, st]
        out_specs += [state_spec, state_spec]
    return pl.pallas_call(
        functools.partial(_retention_kernel, n=seq_len, nb=nb, has_state=has_state, write_state=write_state),
        out_shape=out_shape,
        grid=(pairs, n_batch // nb),
        in_specs=in_specs,
        out_specs=out_specs,
        compiler_params=_params(2),
        name="retention_latent" if has_state else "retention_ctx",
    )(*args)


def _gqa_kernel(*refs, n_src, tq, n_seq):
    q_ref = refs[0]
    k_refs = refs[1:1 + 2 * n_src:2]
    v_refs = refs[2:2 + 2 * n_src:2]
    o_ref = refs[1 + 2 * n_src]
    units = [(b, g, half) for b in range(n_seq) for g in range(GQA_KV_HEADS) for half in range(2)]
    outs = {}

    def kv_rows(ref, b):
        n_keys = ref.shape[0] // n_seq
        return slice(b * n_keys, (b + 1) * n_keys)

    def scores_of(unit):
        b, g, half = unit
        rows, base = slice(b * tq, (b + 1) * tq), g * 2 * LANES
        q = jnp.concatenate([q_ref[rows, base:base + LANES], q_ref[rows, base + LANES:base + 2 * LANES]], axis=0)
        return [_nt(q, _masked_half(k_ref[kv_rows(k_ref, b), g * LANES:(g + 1) * LANES], half)) for k_ref in k_refs]

    def finish(unit, scores):
        b, g, half = unit
        outs[half] = _softmax_apply(scores, [v_ref[kv_rows(v_ref, b), g * LANES:(g + 1) * LANES] for v_ref in v_refs])
        if half == 1:
            rows, base = slice(b * tq, (b + 1) * tq), g * 2 * LANES
            o = jnp.where(_lane_lo(2 * tq), outs[0], outs[1]).astype(BF16)
            o_ref[rows, base:base + LANES] = o[:tq]
            o_ref[rows, base + LANES:base + 2 * LANES] = o[tq:]

    _run_pipelined(units, scores_of, finish)


def _gqa_attention(qa, kd, vd, ctx_kd, ctx_vd, *, n_batch, seq_len, tq):
    n = qa.shape[0]
    tiles = seq_len // tq
    n_src = 1 if ctx_kd is None else 2
    n_seq = 2 if (tiles == 1 and n_src == 1) else 1
    qmap = lambda b, t: (b * tiles + t, 0)
    kmap = lambda b, t: (b, 0)
    in_specs = [pl.BlockSpec((n_seq * tq, GQA_HEADS * HEAD_DIM), qmap),
                pl.BlockSpec((n_seq * seq_len, 2 * LANES), kmap),
                pl.BlockSpec((n_seq * seq_len, 2 * LANES), kmap)]
    args = [qa, kd, vd]
    if n_src == 2:
        past = ctx_kd.shape[1]
        cmap = lambda b, t: (b, 0, 0)
        in_specs += [pl.BlockSpec((None, past, 2 * LANES), cmap)] * 2
        args += [ctx_kd, ctx_vd]
    return pl.pallas_call(
        functools.partial(_gqa_kernel, n_src=n_src, tq=tq, n_seq=n_seq),
        out_shape=jax.ShapeDtypeStruct((n, GQA_HEADS * HEAD_DIM), BF16),
        grid=(n_batch // n_seq, tiles),
        in_specs=in_specs,
        out_specs=pl.BlockSpec((n_seq * tq, GQA_HEADS * HEAD_DIM), qmap),
        compiler_params=_params(2),
        name="gqa_latent" if n_src == 2 else "gqa_ctx",
    )(*args)


def _zero_rows(arr, rows):
    pieces, cur = [], 0
    sub = lax.broadcasted_iota(jnp.int32, (8, arr.shape[1]), 0)
    for r in sorted(rows):
        g0 = (r // 8) * 8
        if g0 > cur:
            pieces.append(arr[cur:g0])
        pieces.append(jnp.where(sub == r - g0, 0.0, arr[g0:g0 + 8]))
        cur = g0 + 8
    if cur < arr.shape[0]:
        pieces.append(arr[cur:])
    return jnp.concatenate(pieces, axis=0)


def _mix_ffn_kernel(*refs, tm, seq_len, final, n_mix):
    x_ref, gate_mix_ref, g_ref, scale_ref, shift_ref, gate_ref = refs[:6]
    m_refs = refs[6:6 + n_mix]
    w_refs = refs[6 + n_mix:6 + 2 * n_mix]
    wup_ref, cw_ref, cb_ref, wd_ref = refs[6 + 2 * n_mix:10 + 2 * n_mix]
    refs = refs[10 + 2 * n_mix:]
    if final:
        gfin_ref = refs[0]
        refs = refs[1:]
    o_ref, act_ref, hb_ref = refs
    halves = [slice(rb * (tm // 2), (rb + 1) * (tm // 2)) for rb in range(2)]
    for rows in halves:
        acc = None
        for m_ref, w_ref in zip(m_refs, w_refs):
            t = _nn(m_ref[rows, :], w_ref[...])
            acc = t if acc is None else acc + t
        y = x_ref[rows, :] + gate_mix_ref[...] * acc
        o_ref[rows, :] = y
        hb_ref[rows, :] = _modulated_norm(y, g_ref[...], scale_ref[...], shift_ref[...]).astype(BF16)
    seq_starts = list(range(0, tm, seq_len))
    seq_ends = [s + seq_len - 1 for s in seq_starts]
    for j in range(N_FF_CHUNKS):
        parts = []
        for off in (0, D_FF):
            cols = slice(off + j * FF_CHUNK, off + (j + 1) * FF_CHUNK)
            u = _nn(hb_ref[...], wup_ref[:, cols])
            cw = cw_ref[:, cols]
            prev = _zero_rows(pltpu.roll(u, 1, 0), seq_starts)
            nxt = _zero_rows(pltpu.roll(u, tm - 1, 0), seq_ends)
            parts.append(prev * cw[0:1] + u * cw[1:2] + nxt * cw[2:3] + cb_ref[:, cols])
        a, g = parts
        act_ref[:, j * FF_CHUNK:(j + 1) * FF_CHUNK] = (a * _sigmoid(a) * g).astype(BF16)
    for rows in halves:
        y = o_ref[rows, :] + gate_ref[...] * _nn(act_ref[rows, :], wd_ref[...])
        if final:
            ms = jnp.mean(y * y, axis=-1, keepdims=True)
            y = y * lax.rsqrt(ms + EPS) * gfin_ref[...]
        o_ref[rows, :] = y


def _mix_ffn(x2d, mods, layer, mixes, weights, ffn_norm_g, wup_c, cw_c, cb_c, wd_c, final_g, *, seq_len, latent):
    n = x2d.shape[0]
    tm = 1024
    bidx = _batch_index_fn(latent, tm, seq_len)
    row = lambda i: (i, 0)
    const2 = lambda i: (0, 0)
    once = pl.Buffered(1)
    resident = lambda a: pl.BlockSpec((None,) + a.shape[1:], lambda i: (layer,) + (0,) * (a.ndim - 1),
                                      pipeline_mode=once)
    in_specs = [pl.BlockSpec((tm, D_MODEL), row), _mod_spec(layer, 2, bidx), pl.BlockSpec((1, D_MODEL), const2),
                _mod_spec(layer, 4, bidx), _mod_spec(layer, 3, bidx), _mod_spec(layer, 5, bidx)]
    in_specs += [pl.BlockSpec((tm, m.shape[1]), row) for m in mixes]
    in_specs += [pl.BlockSpec(w.shape, const2, pipeline_mode=once) for w in weights]
    in_specs += [resident(wup_c), resident(cw_c), resident(cb_c), resident(wd_c)]
    args = [x2d, mods, ffn_norm_g, mods, mods, mods, *mixes, *weights, wup_c, cw_c, cb_c, wd_c]
    final = final_g is not None
    if final:
        in_specs.append(pl.BlockSpec((1, D_MODEL), const2))
        args.append(final_g)
    return pl.pallas_call(
        functools.partial(_mix_ffn_kernel, tm=tm, seq_len=seq_len, final=final, n_mix=len(mixes)),
        out_shape=jax.ShapeDtypeStruct((n, D_MODEL), F32),
        grid=(n // tm,),
        in_specs=in_specs,
        out_specs=pl.BlockSpec((tm, D_MODEL), row),
        scratch_shapes=[pltpu.VMEM((tm, D_FF), BF16), pltpu.VMEM((tm, D_MODEL), BF16)],
        compiler_params=_params(1),
        name="mix_ffn_latent" if latent else "mix_ffn_ctx",
    )(*args)


def _in_odd_kernel(*refs, write_cache, tm):
    x_ref, shift_ref, scale_ref, g_ref, w_ref, q_ref, k_ref, v_ref = refs[:8]
    hb = _modulated_norm(x_ref[...], g_ref[...], scale_ref[...], shift_ref[...]).astype(BF16)
    width = NA_HEADS * HEAD_DIM
    q_scale = HEAD_DIM ** -0.5 * LOG2E
    for which, dst in ((1, k_ref), (2, v_ref)):
        for c in range(2):
            r = _nn(hb, w_ref[:, which * width + c * 512:which * width + (c + 1) * 512])
            dst[:, c * 512:(c + 1) * 512] = r.astype(BF16)
            if write_cache:
                cache_ref = refs[8 + which - 1]
                seq = cache_ref.shape[4]
                for bb in range(tm // seq):
                    for blk in range(512 // LANES):
                        t = r[bb * seq:(bb + 1) * seq, blk * LANES:(blk + 1) * LANES].T
                        for hh in range(2):
                            head = c * (512 // HEAD_DIM) + 2 * blk + hh
                            cache_ref[bb, 0, head] = t[hh * HEAD_DIM:(hh + 1) * HEAD_DIM]
    for c in range(2):
        cols = slice(c * 512, (c + 1) * 512)
        q_ref[:, cols] = (_nn(hb, w_ref[:, c * 512:(c + 1) * 512]) * q_scale).astype(BF16)


def _in_proj_odd(x2d, mods, layer, norm_g, w_bf, *, n_batch, seq_len, latent):
    n = x2d.shape[0]
    tm = 512
    width = NA_HEADS * HEAD_DIM
    bidx = _batch_index_fn(latent, tm, seq_len)
    row = lambda i: (i, 0)
    const = lambda i: (0, 0)
    in_specs = [pl.BlockSpec((tm, D_MODEL), row), _mod_spec(layer, 0, bidx), _mod_spec(layer, 1, bidx),
                pl.BlockSpec((1, D_MODEL), const), pl.BlockSpec(w_bf.shape, const)]
    out_shape = [jax.ShapeDtypeStruct((n, width), BF16)] * 3
    out_specs = [pl.BlockSpec((tm, width), row)] * 3
    write_cache = not latent
    if write_cache:
        cache = jax.ShapeDtypeStruct((n_batch, 1, NA_HEADS, HEAD_DIM, seq_len), F32)
        out_shape += [cache, cache]
        out_specs += [pl.BlockSpec((tm // seq_len, 1, NA_HEADS, HEAD_DIM, seq_len),
                                   lambda i: (i, 0, 0, 0, 0))] * 2
    return pl.pallas_call(
        functools.partial(_in_odd_kernel, write_cache=write_cache, tm=tm),
        out_shape=out_shape,
        grid=(n // tm,),
        in_specs=in_specs,
        out_specs=out_specs,
        compiler_params=_params(1),
        name="in_proj_odd_latent" if latent else "in_proj_odd_ctx",
    )(x2d, mods, mods, norm_g, w_bf)


def _dense_pairs_kernel(q_ref, k_ref, v_ref, o_ref, *, seq_len):
    n_seq = q_ref.shape[0] // seq_len
    units = [(b, p, half) for b in range(n_seq) for p in range(NA_HEADS // 2) for half in range(2)]
    outs = {}

    def scores_of(unit):
        b, p, half = unit
        rows, cols = slice(b * seq_len, (b + 1) * seq_len), slice(p * LANES, (p + 1) * LANES)
        return [_nt(q_ref[rows, cols], _masked_half(k_ref[rows, cols], half))]

    def finish(unit, scores):
        b, p, half = unit
        rows, cols = slice(b * seq_len, (b + 1) * seq_len), slice(p * LANES, (p + 1) * LANES)
        outs[half] = _softmax_apply(scores, [v_ref[rows, cols]])
        if half == 1:
            o_ref[rows, cols] = jnp.where(_lane_lo(seq_len), outs[0], outs[1]).astype(BF16)

    _run_pipelined(units, scores_of, finish)


def _dense_attention_ctx(q, k, v, *, n_batch, seq_len):
    width = NA_HEADS * HEAD_DIM
    n_seq = 4
    spec = pl.BlockSpec((n_seq * seq_len, width), lambda b: (b, 0))
    return pl.pallas_call(
        functools.partial(_dense_pairs_kernel, seq_len=seq_len),
        out_shape=jax.ShapeDtypeStruct(q.shape, BF16),
        grid=(n_batch // n_seq,),
        in_specs=[spec, spec, spec],
        out_specs=spec,
        compiler_params=_params(1),
        name="dense_attention_ctx",
    )(q, k, v)


NA_Q_ROWS = 8
NA_KEY_ROWS = 12


def _na_window_start(r):
    return min(max(r - NA_KH // 2, 0), (1024 // GRID_W) - NA_KH)


def _na_bias_tile(bias_ref, half, tile, n_rows):
    r0 = tile * NA_Q_ROWS
    kr0 = min(max(r0 - NA_KH // 2, 0), n_rows - NA_KEY_ROWS)
    lo = _lane_lo(GRID_W)
    neg_block = jnp.full((GRID_W, LANES), NEG, F32)
    left_off = jnp.where(lo, NEG, 0.0)
    right_off = jnp.where(lo, 0.0, NEG)
    rows = []
    for rq in range(NA_Q_ROWS):
        r = r0 + rq
        rs = _na_window_start(r)
        blocks = []
        for kk in range(NA_KEY_ROWS // 2):
            ka = kr0 + 2 * kk
            va = rs <= ka < rs + NA_KH
            vb = rs <= ka + 1 < rs + NA_KH
            if not (va or vb):
                blocks.append(neg_block)
                continue
            blk = bias_ref[half, ka - r + NA_KH]
            if not va:
                blk = blk + left_off
            if not vb:
                blk = blk + right_off
            blocks.append(blk)
        rows.append(jnp.concatenate(blocks, axis=1))
    return jnp.concatenate(rows, axis=0), kr0


def _na_kernel(q_ref, k_ref, v_ref, ck_ref, cv_ref, bias_ref, o_ref, *, n_rows):
    tq = NA_Q_ROWS * GRID_W
    span = NA_KEY_ROWS * GRID_W
    units = [(tile, half) for tile in range(n_rows // NA_Q_ROWS) for half in range(2)]
    outs = {}

    def window(tile):
        kr0 = min(max(tile * NA_Q_ROWS - NA_KH // 2, 0), n_rows - NA_KEY_ROWS)
        return slice(kr0 * GRID_W, kr0 * GRID_W + span)

    def scores_of(unit):
        tile, half = unit
        q = q_ref[tile * tq:(tile + 1) * tq, :]
        bias, _ = _na_bias_tile(bias_ref, half, tile, n_rows)
        ck_t = jnp.concatenate([ck_ref[0], ck_ref[1]], axis=0)
        own = (lax.broadcasted_iota(jnp.int32, ck_t.shape, 0) < HEAD_DIM) == (half == 0)
        return [_nt(q, _masked_half(k_ref[window(tile), :], half)) + bias,
                _nn(q, jnp.where(own, ck_t, 0.0).astype(BF16))]

    def finish(unit, scores):
        tile, half = unit
        cv_t = jnp.concatenate([cv_ref[0], cv_ref[1]], axis=0).astype(BF16)
        outs[half] = _softmax_apply(scores, [v_ref[window(tile), :], ("t", cv_t)])
        if half == 1:
            o_ref[tile * tq:(tile + 1) * tq, :] = jnp.where(_lane_lo(tq), outs[0], outs[1]).astype(BF16)

    _run_pipelined(units, scores_of, finish)


def _na_bias_table(rpb):
    cidx = np.arange(GRID_W)
    cs = np.clip(cidx - NA_KW // 2, 0, GRID_W - NA_KW)
    kc = np.arange(GRID_W)
    inside = (kc[None, :] >= cs[:, None]) & (kc[None, :] < cs[:, None] + NA_KW)
    rel = kc[None, :] - cidx[:, None] + NA_KW - 1
    onehot = (rel[None] == np.arange(2 * NA_KW - 1)[:, None, None]) & inside[None]
    m = jnp.einsum("hdj,jck->hdck", rpb * LOG2E, jnp.asarray(onehot, F32), precision=lax.Precision.HIGHEST)
    m = jnp.where(jnp.asarray(inside)[None, None], m, NEG)
    neg = jnp.full((rpb.shape[0], 1, GRID_W, GRID_W), NEG, F32)
    left = jnp.concatenate([neg, m], axis=1)
    right = jnp.concatenate([m, neg], axis=1)
    return jnp.concatenate([left, right], axis=-1)


def _na_attention(q, k, v, ctx_k, ctx_v, bias_tab, *, n_batch, seq_len):
    pairs = NA_HEADS // 2
    tok = lambda p, b: (b, p)
    ctx = lambda p, b: (b, p, 0, 0)
    past = ctx_k.shape[-1]
    return pl.pallas_call(
        functools.partial(_na_kernel, n_rows=seq_len // GRID_W),
        out_shape=jax.ShapeDtypeStruct(q.shape, BF16),
        grid=(pairs, n_batch),
        in_specs=[pl.BlockSpec((seq_len, LANES), tok)] * 3
        + [pl.BlockSpec((None, 2, HEAD_DIM, past), ctx)] * 2
        + [pl.BlockSpec((2, 2 * NA_KH, GRID_W, LANES), lambda p, b: (p, 0, 0, 0))],
        out_specs=pl.BlockSpec((seq_len, LANES), tok),
        compiler_params=_params(2),
        name="neighbourhood_attention",
    )(q, k, v, ctx_k, ctx_v, bias_tab)


def _rope_tables(n):
    t = np.arange(n)
    row = (t // GRID_W).astype(np.float64)
    col = (t % GRID_W).astype(np.float64)
    half = HEAD_DIM // 2
    inv = ROPE_BASE ** (-np.arange(0, half, 2, dtype=np.float64) / half)
    ang_r = row[:, None] * inv
    ang_c = col[:, None] * inv
    cos_h = np.concatenate([np.cos(ang_r)] * 2 + [np.cos(ang_c)] * 2, axis=-1)
    sin_h = np.concatenate([-np.sin(ang_r), np.sin(ang_r), -np.sin(ang_c), np.sin(ang_c)], axis=-1)
    return (jnp.asarray(np.concatenate([cos_h, cos_h], axis=-1), F32),
            jnp.asarray(np.concatenate([sin_h, sin_h], axis=-1), F32))


def _ffn_weights(w_up, conv_w, conv_b, w_down):
    return w_up.astype(BF16), conv_w, conv_b[:, None, :], w_down.astype(BF16)


def _token_major_dup(cache):
    b, kv, t, d = cache.shape
    c = jnp.transpose(cache, (0, 2, 1, 3))[:, :, :, None, :]
    return jnp.broadcast_to(c, (b, t, kv, 2, d)).reshape(b, t, kv * 2 * d).astype(BF16)


def _head_transposed(cache):
    return jnp.swapaxes(cache, -1, -2)


def kernel(x_prompt, x_sample, state_ret_fwd, state_ret_bwd, cache_gqa_k, cache_gqa_v, cache_na_k, cache_na_v,
           c, c_ctx, ada_w, ada_b, norm_mix, norm_ffn, norm_final, even_w_in, even_w_out, ret_decay_fwd,
           ret_decay_bwd, ret_gn, gqa_q_norm, gqa_k_norm, odd_w_in, odd_w_out, na_rpb, ffn_w_up, ffn_conv_w,
           ffn_conv_b, ffn_w_down):
    nb_c, len_c, _ = x_prompt.shape
    nb_s, len_s, _ = x_sample.shape
    depth = ada_w.shape[0]
    streams = {
        False: dict(n_batch=nb_c, seq_len=len_c),
        True: dict(n_batch=nb_s, seq_len=len_s),
    }
    xs = {False: x_prompt.reshape(nb_c * len_c, D_MODEL), True: x_sample.reshape(nb_s * len_s, D_MODEL)}

    rows = 8 * (-(-(1 + nb_s) // 8))
    cond = jnp.zeros((rows, D_MODEL), F32).at[0].set(c_ctx).at[1:1 + nb_s].set(c)
    mods = _ada_params(cond, ada_w, ada_b).reshape(depth, rows, 6, 1, D_MODEL)
    rope_tabs = _rope_tables(len_s)
    ffn_w = _ffn_weights(ffn_w_up, ffn_conv_w, ffn_conv_b, ffn_w_down)
    outs = {}
    mixed = {}

    for l in range(depth):
        g_mix = norm_mix[l][None, :]
        g_ffn = norm_ffn[l][None, :]
        if l % 2 == 0:
            e = l // 2
            w_in = even_w_in[e].astype(BF16)
            w_out = even_w_out[e].astype(BF16)
            w_out_parts = [w_out[:RET_HEADS * RET_DV], w_out[RET_HEADS * RET_DV:]]
            log_g = jnp.stack([jax.nn.log_sigmoid(ret_decay_fwd[e].astype(F32)),
                               jax.nn.log_sigmoid(ret_decay_bwd[e].astype(F32))])
            gn = ret_gn[e][None, :]
            qg2 = jnp.tile(gqa_q_norm[e], 2)[None, :]
            kg2 = jnp.tile(gqa_k_norm[e], 2)[None, :]
            for latent in (False, True):
                st = streams[latent]
                res = _in_proj_even(xs[latent], mods, l, g_mix, w_in, qg2, kg2, rope_tabs, latent=latent, **st)
                qr, kr, vr, gr, qa, kd, vd = res[:7]
                if latent:
                    ret = _retention(log_g, qr, kr, vr, gr, gn, state_ret_fwd[:, e:e + 1],
                                     state_ret_bwd[:, e:e + 1], write_state=False, **st)[0]
                    att = _gqa_attention(qa, kd, vd, _token_major_dup(cache_gqa_k[:, e]),
                                         _token_major_dup(cache_gqa_v[:, e]), tq=512, **st)
                else:
                    outs.setdefault("gk", []).append(res[7])
                    outs.setdefault("gv", []).append(res[8])
                    ret, s_f, s_b = _retention(log_g, qr, kr, vr, gr, gn, None, None, write_state=True, **st)
                    outs.setdefault("sf", []).append(s_f)
                    outs.setdefault("sb", []).append(s_b)
                    att = _gqa_attention(qa, kd, vd, None, None, tq=st["seq_len"], **st)
                mixed[latent] = ([ret, att], w_out_parts)
        else:
            o = l // 2
            w_in = odd_w_in[o].astype(BF16)
            w_out = odd_w_out[o].astype(BF16)
            for latent in (False, True):
                st = streams[latent]
                res = _in_proj_odd(xs[latent], mods, l, g_mix, w_in, latent=latent, **st)
                q, k, v = res[:3]
                if latent:
                    att = _na_attention(q, k, v, _head_transposed(cache_na_k[:, o]), _head_transposed(cache_na_v[:, o]),
                                        _na_bias_table(na_rpb[o]), **st)
                else:
                    outs.setdefault("nk", []).append(res[3])
                    outs.setdefault("nv", []).append(res[4])
                    att = _dense_attention_ctx(q, k, v, **st)
                mixed[latent] = ([att], [w_out])
        final_g = norm_final[None, :] if l == depth - 1 else None
        for latent in (False, True):
            xs[latent] = _mix_ffn(xs[latent], mods, l, *mixed[latent], g_ffn, *ffn_w, final_g,
                                   seq_len=streams[latent]["seq_len"], latent=latent)

    tr = lambda a: jnp.swapaxes(a, -1, -2)
    cat = lambda name: outs[name][0] if len(outs[name]) == 1 else jnp.concatenate(outs[name], axis=1)
    return (xs[False].reshape(nb_c, len_c, D_MODEL), xs[True].reshape(nb_s, len_s, D_MODEL),
            cat("sf"), cat("sb"), tr(cat("gk")), tr(cat("gv")), tr(cat("nk")), tr(cat("nv")))
```

```python
import functools

import numpy as np
import jax
import jax.numpy as jnp
from jax import lax
from jax.experimental import pallas as pl
from jax.experimental.pallas import tpu as pltpu

F32 = jnp.float32
BF16 = jnp.bfloat16

D_MODEL = 1024
GRID_W = 64
HEAD_DIM = 64
ROPE_BASE = 10000.0
EPS = 1e-6
GN_EPS = 1e-5
RET_HEADS = 8
RET_DK = 64
RET_DV = 128
RET_CHUNK = 128
GQA_HEADS = 8
GQA_KV_HEADS = 2
NA_HEADS = 16
NA_KH = 8
NA_KW = 16
D_FF = 2816
LANES = 128
MXU_DIM = 256
FF_CHUNK = MXU_DIM
N_FF_CHUNKS = D_FF // FF_CHUNK
PROJ_CHUNK = 2 * MXU_DIM
RET_QK_W = RET_HEADS * RET_DK
RET_V_W = RET_HEADS * RET_DV
GQA_Q_W = GQA_HEADS * HEAD_DIM
GQA_KV_W = GQA_KV_HEADS * HEAD_DIM
NA_W = NA_HEADS * HEAD_DIM
OFF_QR, OFF_KR, OFF_VR, OFF_GR, OFF_QA, OFF_KA, OFF_VA, EVEN_IN_W = (
    int(v) for v in np.cumsum([0, RET_QK_W, RET_QK_W, RET_V_W, RET_V_W, GQA_Q_W, GQA_KV_W, GQA_KV_W]))
PROJ_ROWS = 512
FFN_ROWS = 1024
RET_ROWS = 1024
GQA_Q_ROWS = 512
GQA_CTX_SEQS = 2
DENSE_CTX_SEQS = 4
ADA_COLS = 2048
NEG = -1e30
LOG2E = 1.4426950408889634
VMEM_LIMIT = 56 * 1024 * 1024


def _nn(a, b):
    return jnp.dot(a, b, preferred_element_type=F32)


def _nt(a, b):
    return lax.dot_general(a, b, (((1,), (1,)), ((), ())), preferred_element_type=F32)


def _tn(a, b):
    return lax.dot_general(a, b, (((0,), (0,)), ((), ())), preferred_element_type=F32)


def _sigmoid(x):
    return 1.0 / (1.0 + jnp.exp(-x))


def _params(n_axes):
    return pltpu.CompilerParams(dimension_semantics=("arbitrary",) * n_axes, vmem_limit_bytes=VMEM_LIMIT)


def _modulated_norm(x, g, scale, shift):
    ms = jnp.mean(x * x, axis=-1, keepdims=True)
    return (x * lax.rsqrt(ms + EPS) * g) * (1.0 + scale) + shift


def _lane_lo(rows):
    return lax.broadcasted_iota(jnp.int32, (rows, LANES), 1) < HEAD_DIM


def _head_rms_norm(xb, gain, lo):
    sq = xb * xb
    s_lo = jnp.sum(jnp.where(lo, sq, 0.0), axis=-1, keepdims=True)
    s_hi = jnp.sum(jnp.where(lo, 0.0, sq), axis=-1, keepdims=True)
    r = jnp.where(lo, lax.rsqrt(s_lo * (1.0 / HEAD_DIM) + EPS), lax.rsqrt(s_hi * (1.0 / HEAD_DIM) + EPS))
    return xb * r * gain


def _rope(xb, cos, sin_signed, first16):
    partner = jnp.where(first16, pltpu.roll(xb, LANES - 16, 1), pltpu.roll(xb, 16, 1))
    return xb * cos + partner * sin_signed


def _softmax_apply(scores, values):
    m = functools.reduce(jnp.maximum, [jnp.max(s, axis=-1, keepdims=True) for s in scores])
    es = [jnp.exp2(s - m) for s in scores]
    l = functools.reduce(jnp.add, [jnp.sum(e, axis=-1, keepdims=True) for e in es])
    o = None
    for e, v in zip(es, values):
        t = _nt(e.astype(BF16), v[1]) if isinstance(v, tuple) else _nn(e.astype(BF16), v)
        o = t if o is None else o + t
    return o * (1.0 / l)


def _run_pipelined(units, scores_of, finish):
    pending = scores_of(units[0])
    for n, unit in enumerate(units):
        nxt = scores_of(units[n + 1]) if n + 1 < len(units) else None
        finish(unit, pending)
        pending = nxt


def _masked_half(kp, half):
    lo = _lane_lo(kp.shape[0])
    return jnp.where(lo if half == 0 else jnp.logical_not(lo), kp, jnp.zeros_like(kp))


def _ada_kernel(c_ref, w_ref, b_ref, o_ref):
    c = c_ref[...]
    a = (c * _sigmoid(c)).astype(BF16)
    o_ref[...] = _nn(a, w_ref[...].astype(BF16)) + b_ref[...]


def _ada_params(cond, ada_w, ada_b):
    depth = ada_w.shape[0]
    rows = cond.shape[0]
    tn = ADA_COLS
    return pl.pallas_call(
        _ada_kernel,
        out_shape=jax.ShapeDtypeStruct((depth, rows, 6 * D_MODEL), F32),
        grid=(depth, 6 * D_MODEL // tn),
        in_specs=[
            pl.BlockSpec((rows, D_MODEL), lambda l, j: (0, 0)),
            pl.BlockSpec((None, D_MODEL, tn), lambda l, j: (l, 0, j)),
            pl.BlockSpec((None, 1, tn), lambda l, j: (l, 0, j)),
        ],
        out_specs=pl.BlockSpec((None, rows, tn), lambda l, j: (l, 0, j)),
        compiler_params=_params(2),
        name="ada_params",
    )(cond, ada_w, ada_b.reshape(depth, 1, 6 * D_MODEL))


def _mod_spec(layer, which, bidx):
    return pl.BlockSpec((None, None, None, 1, D_MODEL), lambda i: (layer, bidx(i), which, 0, 0))


def _batch_index_fn(latent, rows_per_tile, seq_len):
    if not latent:
        return lambda i: 0
    return lambda i: 1 + (i * rows_per_tile) // seq_len


def _in_even_kernel(*refs, latent, tm):
    if latent:
        (x_ref, shift_ref, scale_ref, g_ref, w_ref, qg_ref, kg_ref, cos_ref, sin_ref,
         qr_ref, kr_ref, vr_ref, gr_ref, qa_ref, kd_ref, vd_ref) = refs
    else:
        (x_ref, shift_ref, scale_ref, g_ref, w_ref, qg_ref, kg_ref,
         qr_ref, kr_ref, vr_ref, gr_ref, qa_ref, kd_ref, vd_ref, ck_ref, cv_ref) = refs
    hb = _modulated_norm(x_ref[...], g_ref[...], scale_ref[...], shift_ref[...]).astype(BF16)
    lane = lax.broadcasted_iota(jnp.int32, (tm, LANES), 1)
    lo = lane < HEAD_DIM
    if latent:
        cos = cos_ref[...]
        sin = sin_ref[...]
        first16 = (lane % 32) < 16
        rope = lambda v: _rope(v, cos, sin, first16)
    else:
        rope = lambda v: v
    dk_scale = RET_DK ** -0.5
    q_scale = HEAD_DIM ** -0.5 * LOG2E

    r = _nn(hb, w_ref[:, OFF_QA:OFF_QA + GQA_Q_W])
    qg = qg_ref[...]
    for b in range(GQA_Q_W // LANES):
        blk = rope(_head_rms_norm(r[:, b * LANES:(b + 1) * LANES], qg, lo)) * q_scale
        qa_ref[:, b * LANES:(b + 1) * LANES] = blk.astype(BF16)
    assert GQA_KV_W == LANES
    r = _nn(hb, w_ref[:, OFF_KA:OFF_VA + GQA_KV_W])
    kn = _head_rms_norm(r[:, 0:LANES], kg_ref[...], lo)
    vn = r[:, LANES:2 * LANES]
    if not latent:
        seq = ck_ref.shape[4]
        for bb in range(tm // seq):
            for src, dst in ((kn, ck_ref), (vn, cv_ref)):
                t = src[bb * seq:(bb + 1) * seq, :].T
                for kv in range(GQA_KV_HEADS):
                    dst[bb, 0, kv] = t[kv * HEAD_DIM:(kv + 1) * HEAD_DIM]
    kn = rope(kn)
    for src, dst in ((kn, kd_ref), (vn, vd_ref)):
        sw = pltpu.roll(src, HEAD_DIM, 1)
        dst[:, 0:LANES] = jnp.where(lo, src, sw).astype(BF16)
        dst[:, LANES:2 * LANES] = jnp.where(lo, sw, src).astype(BF16)
    r = _nn(hb, w_ref[:, OFF_QR:OFF_QR + RET_QK_W])
    for b in range(RET_QK_W // LANES):
        qr_ref[:, b * LANES:(b + 1) * LANES] = rope(r[:, b * LANES:(b + 1) * LANES])
    r = _nn(hb, w_ref[:, OFF_KR:OFF_KR + RET_QK_W]) * dk_scale
    for b in range(RET_QK_W // LANES):
        kr_ref[:, b * LANES:(b + 1) * LANES] = rope(r[:, b * LANES:(b + 1) * LANES])
    for c in range(RET_V_W // PROJ_CHUNK):
        cols = slice(c * PROJ_CHUNK, (c + 1) * PROJ_CHUNK)
        r = _nn(hb, w_ref[:, OFF_GR + c * PROJ_CHUNK:OFF_GR + (c + 1) * PROJ_CHUNK])
        gr_ref[:, cols] = r * _sigmoid(r)
    for c in range(RET_V_W // PROJ_CHUNK):
        cols = slice(c * PROJ_CHUNK, (c + 1) * PROJ_CHUNK)
        vr_ref[:, cols] = _nn(hb, w_ref[:, OFF_VR + c * PROJ_CHUNK:OFF_VR + (c + 1) * PROJ_CHUNK]).astype(BF16)


def _in_proj_even(x2d, mods, layer, norm_g, w_bf, q_gain2, k_gain2, rope_tabs, *, n_batch, seq_len, latent):
    n = x2d.shape[0]
    tm = PROJ_ROWS
    assert w_bf.shape == (D_MODEL, EVEN_IN_W)
    bidx = _batch_index_fn(latent, tm, seq_len)
    row = lambda i: (i, 0)
    const = lambda i: (0, 0)
    in_specs = [
        pl.BlockSpec((tm, D_MODEL), row),
        _mod_spec(layer, 0, bidx),
        _mod_spec(layer, 1, bidx),
        pl.BlockSpec((1, D_MODEL), const),
        pl.BlockSpec(w_bf.shape, const),
        pl.BlockSpec((1, LANES), const),
        pl.BlockSpec((1, LANES), const),
    ]
    args = [x2d, mods, mods, norm_g, w_bf, q_gain2, k_gain2]
    if latent:
        tiles_per_seq = seq_len // tm
        in_specs += [pl.BlockSpec((tm, LANES), lambda i: (i % tiles_per_seq, 0))] * 2
        args += list(rope_tabs)
    out_shape = [
        jax.ShapeDtypeStruct((n, RET_QK_W), F32),
        jax.ShapeDtypeStruct((n, RET_QK_W), F32),
        jax.ShapeDtypeStruct((n, RET_V_W), BF16),
        jax.ShapeDtypeStruct((n, RET_V_W), F32),
        jax.ShapeDtypeStruct((n, GQA_Q_W), BF16),
        jax.ShapeDtypeStruct((n, 2 * GQA_KV_W), BF16),
        jax.ShapeDtypeStruct((n, 2 * GQA_KV_W), BF16),
    ]
    out_specs = [pl.BlockSpec((tm, s.shape[1]), row) for s in out_shape]
    if not latent:
        cache = jax.ShapeDtypeStruct((n_batch, 1, GQA_KV_HEADS, HEAD_DIM, seq_len), F32)
        out_shape += [cache, cache]
        out_specs += [pl.BlockSpec((tm // seq_len, 1, GQA_KV_HEADS, HEAD_DIM, seq_len),
                                   lambda i: (i, 0, 0, 0, 0))] * 2
    return pl.pallas_call(
        functools.partial(_in_even_kernel, latent=latent, tm=tm),
        out_shape=out_shape,
        grid=(n // tm,),
        in_specs=in_specs,
        out_specs=out_specs,
        compiler_params=_params(1),
        name="in_proj_even_latent" if latent else "in_proj_even_ctx",
    )(*args)


def _retention_kernel(*refs, n, nb, has_state, write_state):
    lg_ref, q_ref, k_ref, v_ref, gr_ref, gn_ref = refs[:6]
    refs = refs[6:]
    if has_state:
        s0f_ref, s0b_ref = refs[:2]
        refs = refs[2:]
    o_ref = refs[0]
    if write_state:
        sf_ref, sb_ref = refs[1:3]
    c = RET_CHUNK
    assert c == LANES and 2 * RET_DK == LANES
    nc = n // c
    p = pl.program_id(0)
    lgf = [lg_ref[0, 2 * p + hh] for hh in range(2)]
    lgb = [lg_ref[1, 2 * p + hh] for hh in range(2)]
    row = lax.broadcasted_iota(jnp.int32, (c, c), 0)
    col = lax.broadcasted_iota(jnp.int32, (c, c), 1)
    diff = (row - col).astype(F32)
    pos = row.astype(F32)
    lane_lo = col < RET_DK

    def both_scans(f, b):
        return (jnp.where(diff >= 0, jnp.exp(f * jnp.maximum(diff, 0.0)), 0.0)
                + jnp.where(diff <= 0, jnp.exp(b * jnp.maximum(-diff, 0.0)), 0.0))

    decay2 = jnp.concatenate([both_scans(lgf[0], lgb[0]), both_scans(lgf[1], lgb[1])], axis=1)
    lgf_lane = jnp.where(lane_lo, lgf[0], lgf[1])
    lgb_lane = jnp.where(lane_lo, lgb[0], lgb[1])
    qd_f = jnp.exp(lgf_lane * (pos + 1.0))
    kd_f = jnp.exp(lgf_lane * (c - 1.0 - pos))
    qd_b = jnp.exp(lgb_lane * (c - pos))
    kd_b = jnp.exp(lgb_lane * pos)
    srow = lax.broadcasted_iota(jnp.int32, (c, 2 * RET_DV), 0)
    scol = lax.broadcasted_iota(jnp.int32, (c, 2 * RET_DV), 1)
    row_a = srow < RET_DK
    col_a = scol < RET_DV
    own = row_a == col_a
    cd_f = jnp.exp(jnp.where(row_a, lgf[0], lgf[1]) * float(c))
    cd_b = jnp.exp(jnp.where(row_a, lgb[0], lgb[1]) * float(c))
    zeros_half = jnp.zeros((RET_DK, RET_DV), F32)
    gn = gn_ref[...]

    def place(s0_ref, bi):
        top = jnp.concatenate([s0_ref[bi, 0, 0], zeros_half], axis=1)
        bot = jnp.concatenate([zeros_half, s0_ref[bi, 0, 1]], axis=1)
        return jnp.concatenate([top, bot], axis=0)

    for bi in range(nb):
        rows = [slice(bi * n + i * c, bi * n + (i + 1) * c) for i in range(nc)]
        if has_state:
            s_f = place(s0f_ref, bi)
            s_b = place(s0b_ref, bi)
        else:
            s_f = jnp.zeros((c, 2 * RET_DV), F32)
            s_b = jnp.zeros((c, 2 * RET_DV), F32)
        before_f = []
        for i in range(nc):
            before_f.append(s_f)
            kv = _tn((k_ref[rows[i], :] * kd_f).astype(BF16), v_ref[rows[i], :])
            s_f = s_f * cd_f + jnp.where(own, kv, 0.0)
        before_b = [None] * nc
        for i in reversed(range(nc)):
            before_b[i] = s_b
            kv = _tn((k_ref[rows[i], :] * kd_b).astype(BF16), v_ref[rows[i], :])
            s_b = s_b * cd_b + jnp.where(own, kv, 0.0)
        if write_state:
            for hh in range(2):
                blk = (slice(hh * RET_DK, (hh + 1) * RET_DK), slice(hh * RET_DV, (hh + 1) * RET_DV))
                sf_ref[bi, 0, hh] = s_f[blk]
                sb_ref[bi, 0, hh] = s_b[blk]
        for i in range(nc):
            qc = q_ref[rows[i], :]
            kc = k_ref[rows[i], :]
            vc = v_ref[rows[i], :]
            k_cat = jnp.concatenate([jnp.where(lane_lo, kc, 0.0), jnp.where(lane_lo, 0.0, kc)], axis=0)
            v_blk = jnp.concatenate([jnp.where(col_a, vc, jnp.zeros_like(vc)),
                                     jnp.where(col_a, jnp.zeros_like(vc), vc)], axis=0)
            scores = _nt(qc.astype(BF16), k_cat.astype(BF16)) * decay2
            q_cat = jnp.concatenate([(qc * qd_f).astype(BF16), (qc * qd_b).astype(BF16)], axis=1)
            s_cat = jnp.concatenate([before_f[i], before_b[i]], axis=0).astype(BF16)
            o = _nn(scores.astype(BF16), v_blk) + _nn(q_cat, s_cat)
            for hh in range(2):
                vcols = slice(hh * RET_DV, (hh + 1) * RET_DV)
                oh = o[:, vcols]
                mu = jnp.mean(oh, axis=-1, keepdims=True)
                d = oh - mu
                var = jnp.mean(d * d, axis=-1, keepdims=True)
                y = d * lax.rsqrt(var + GN_EPS) * gn[:, vcols] * gr_ref[rows[i], vcols]
                o_ref[rows[i], vcols] = y.astype(BF16)


def _retention(log_g, qr, kr, vr, gr, gn, state_f, state_b, *, n_batch, seq_len, write_state):
    n = qr.shape[0]
    pairs = RET_HEADS // 2
    has_state = state_f is not None
    nb = max(1, RET_ROWS // seq_len)
    rows = nb * seq_len
    tok = lambda p, g: (g, p)
    in_specs = [
        pl.BlockSpec(memory_space=pltpu.SMEM),
        pl.BlockSpec((rows, LANES), tok),
        pl.BlockSpec((rows, LANES), tok),
        pl.BlockSpec((rows, 2 * RET_DV), tok),
        pl.BlockSpec((rows, 2 * RET_DV), tok),
        pl.BlockSpec((1, 2 * RET_DV), lambda p, g: (0, p)),
    ]
    args = [log_g, qr, kr, vr, gr, gn]
    state_spec = pl.BlockSpec((nb, 1, 2, RET_DK, RET_DV), lambda p, g: (g, 0, p, 0, 0))
    if has_state:
        in_specs += [state_spec, state_spec]
        args += [state_f, state_b]
    out_shape = [jax.ShapeDtypeStruct((n, RET_HEADS * RET_DV), BF16)]
    out_specs = [pl.BlockSpec((rows, 2 * RET_DV), tok)]
    if write_state:
        st = jax.ShapeDtypeStruct((n_batch, 1, RET_HEADS, RET_DK, RET_DV), F32)
        out_shape += [st, st]
        out_specs += [state_spec, state_spec]
    return pl.pallas_call(
        functools.partial(_retention_kernel, n=seq_len, nb=nb, has_state=has_state, write_state=write_state),
        out_shape=out_shape,
        grid=(pairs, n_batch // nb),
        in_specs=in_specs,
        out_specs=out_specs,
        compiler_params=_params(2),
        name="retention_latent" if has_state else "retention_ctx",
    )(*args)


def _gqa_kernel(*refs, n_src, tq, n_seq):
    q_ref = refs[0]
    k_refs = refs[1:1 + 2 * n_src:2]
    v_refs = refs[2:2 + 2 * n_src:2]
    o_ref = refs[1 + 2 * n_src]
    units = [(b, g, half) for b in range(n_seq) for g in range(GQA_KV_HEADS) for half in range(2)]
    outs = {}

    def kv_rows(ref, b):
        n_keys = ref.shape[0] // n_seq
        return slice(b * n_keys, (b + 1) * n_keys)

    def scores_of(unit):
        b, g, half = unit
        rows, base = slice(b * tq, (b + 1) * tq), g * 2 * LANES
        q = jnp.concatenate([q_ref[rows, base:base + LANES], q_ref[rows, base + LANES:base + 2 * LANES]], axis=0)
        return [_nt(q, _masked_half(k_ref[kv_rows(k_ref, b), g * LANES:(g + 1) * LANES], half)) for k_ref in k_refs]

    def finish(unit, scores):
        b, g, half = unit
        outs[half] = _softmax_apply(scores, [v_ref[kv_rows(v_ref, b), g * LANES:(g + 1) * LANES] for v_ref in v_refs])
        if half == 1:
            rows, base = slice(b * tq, (b + 1) * tq), g * 2 * LANES
            o = jnp.where(_lane_lo(2 * tq), outs[0], outs[1]).astype(BF16)
            o_ref[rows, base:base + LANES] = o[:tq]
            o_ref[rows, base + LANES:base + 2 * LANES] = o[tq:]

    _run_pipelined(units, scores_of, finish)


def _gqa_attention(qa, kd, vd, ctx_kd, ctx_vd, *, n_batch, seq_len, tq):
    n = qa.shape[0]
    tiles = seq_len // tq
    n_src = 1 if ctx_kd is None else 2
    n_seq = GQA_CTX_SEQS if (tiles == 1 and n_src == 1) else 1
    qmap = lambda b, t: (b * tiles + t, 0)
    kmap = lambda b, t: (b, 0)
    in_specs = [pl.BlockSpec((n_seq * tq, GQA_Q_W), qmap),
                pl.BlockSpec((n_seq * seq_len, 2 * LANES), kmap),
                pl.BlockSpec((n_seq * seq_len, 2 * LANES), kmap)]
    args = [qa, kd, vd]
    if n_src == 2:
        past = ctx_kd.shape[1]
        cmap = lambda b, t: (b, 0, 0)
        in_specs += [pl.BlockSpec((None, past, 2 * LANES), cmap)] * 2
        args += [ctx_kd, ctx_vd]
    return pl.pallas_call(
        functools.partial(_gqa_kernel, n_src=n_src, tq=tq, n_seq=n_seq),
        out_shape=jax.ShapeDtypeStruct((n, GQA_Q_W), BF16),
        grid=(n_batch // n_seq, tiles),
        in_specs=in_specs,
        out_specs=pl.BlockSpec((n_seq * tq, GQA_Q_W), qmap),
        compiler_params=_params(2),
        name="gqa_latent" if n_src == 2 else "gqa_ctx",
    )(*args)


def _zero_rows(arr, rows):
    pieces, cur = [], 0
    sub = lax.broadcasted_iota(jnp.int32, (8, arr.shape[1]), 0)
    for r in sorted(rows):
        g0 = (r // 8) * 8
        if g0 > cur:
            pieces.append(arr[cur:g0])
        pieces.append(jnp.where(sub == r - g0, 0.0, arr[g0:g0 + 8]))
        cur = g0 + 8
    if cur < arr.shape[0]:
        pieces.append(arr[cur:])
    return jnp.concatenate(pieces, axis=0)


def _mix_ffn_kernel(*refs, tm, seq_len, final, n_mix):
    x_ref, gate_mix_ref, g_ref, scale_ref, shift_ref, gate_ref = refs[:6]
    m_refs = refs[6:6 + n_mix]
    w_refs = refs[6 + n_mix:6 + 2 * n_mix]
    wup_ref, cw_ref, cb_ref, wd_ref = refs[6 + 2 * n_mix:10 + 2 * n_mix]
    refs = refs[10 + 2 * n_mix:]
    if final:
        gfin_ref = refs[0]
        refs = refs[1:]
    o_ref, act_ref, hb_ref = refs
    halves = [slice(rb * (tm // 2), (rb + 1) * (tm // 2)) for rb in range(2)]
    for rows in halves:
        acc = None
        for m_ref, w_ref in zip(m_refs, w_refs):
            t = _nn(m_ref[rows, :], w_ref[...])
            acc = t if acc is None else acc + t
        y = x_ref[rows, :] + gate_mix_ref[...] * acc
        o_ref[rows, :] = y
        hb_ref[rows, :] = _modulated_norm(y, g_ref[...], scale_ref[...], shift_ref[...]).astype(BF16)
    seq_starts = list(range(0, tm, seq_len))
    seq_ends = [s + seq_len - 1 for s in seq_starts]
    for j in range(N_FF_CHUNKS):
        parts = []
        for off in (0, D_FF):
            cols = slice(off + j * FF_CHUNK, off + (j + 1) * FF_CHUNK)
            u = _nn(hb_ref[...], wup_ref[:, cols])
            cw = cw_ref[:, cols]
            prev = _zero_rows(pltpu.roll(u, 1, 0), seq_starts)
            nxt = _zero_rows(pltpu.roll(u, tm - 1, 0), seq_ends)
            parts.append(prev * cw[0:1] + u * cw[1:2] + nxt * cw[2:3] + cb_ref[:, cols])
        a, g = parts
        act_ref[:, j * FF_CHUNK:(j + 1) * FF_CHUNK] = (a * _sigmoid(a) * g).astype(BF16)
    for rows in halves:
        y = o_ref[rows, :] + gate_ref[...] * _nn(act_ref[rows, :], wd_ref[...])
        if final:
            ms = jnp.mean(y * y, axis=-1, keepdims=True)
            y = y * lax.rsqrt(ms + EPS) * gfin_ref[...]
        o_ref[rows, :] = y


def _mix_ffn(x2d, mods, layer, mixes, weights, ffn_norm_g, wup_c, cw_c, cb_c, wd_c, final_g, *, seq_len, latent):
    n = x2d.shape[0]
    tm = FFN_ROWS
    bidx = _batch_index_fn(latent, tm, seq_len)
    row = lambda i: (i, 0)
    const2 = lambda i: (0, 0)
    once = pl.Buffered(1)
    resident = lambda a: pl.BlockSpec((None,) + a.shape[1:], lambda i: (layer,) + (0,) * (a.ndim - 1),
                                      pipeline_mode=once)
    in_specs = [pl.BlockSpec((tm, D_MODEL), row), _mod_spec(layer, 2, bidx), pl.BlockSpec((1, D_MODEL), const2),
                _mod_spec(layer, 4, bidx), _mod_spec(layer, 3, bidx), _mod_spec(layer, 5, bidx)]
    in_specs += [pl.BlockSpec((tm, m.shape[1]), row) for m in mixes]
    in_specs += [pl.BlockSpec(w.shape, const2, pipeline_mode=once) for w in weights]
    in_specs += [resident(wup_c), resident(cw_c), resident(cb_c), resident(wd_c)]
    args = [x2d, mods, ffn_norm_g, mods, mods, mods, *mixes, *weights, wup_c, cw_c, cb_c, wd_c]
    final = final_g is not None
    if final:
        in_specs.append(pl.BlockSpec((1, D_MODEL), const2))
        args.append(final_g)
    return pl.pallas_call(
        functools.partial(_mix_ffn_kernel, tm=tm, seq_len=seq_len, final=final, n_mix=len(mixes)),
        out_shape=jax.ShapeDtypeStruct((n, D_MODEL), F32),
        grid=(n // tm,),
        in_specs=in_specs,
        out_specs=pl.BlockSpec((tm, D_MODEL), row),
        scratch_shapes=[pltpu.VMEM((tm, D_FF), BF16), pltpu.VMEM((tm, D_MODEL), BF16)],
        compiler_params=_params(1),
        name="mix_ffn_latent" if latent else "mix_ffn_ctx",
    )(*args)


def _in_odd_kernel(*refs, write_cache, tm):
    x_ref, shift_ref, scale_ref, g_ref, w_ref, q_ref, k_ref, v_ref = refs[:8]
    hb = _modulated_norm(x_ref[...], g_ref[...], scale_ref[...], shift_ref[...]).astype(BF16)
    q_scale = HEAD_DIM ** -0.5 * LOG2E
    chunks = NA_W // PROJ_CHUNK
    for which, dst in ((1, k_ref), (2, v_ref)):
        for c in range(chunks):
            r = _nn(hb, w_ref[:, which * NA_W + c * PROJ_CHUNK:which * NA_W + (c + 1) * PROJ_CHUNK])
            dst[:, c * PROJ_CHUNK:(c + 1) * PROJ_CHUNK] = r.astype(BF16)
            if write_cache:
                cache_ref = refs[8 + which - 1]
                seq = cache_ref.shape[4]
                for bb in range(tm // seq):
                    for blk in range(PROJ_CHUNK // LANES):
                        t = r[bb * seq:(bb + 1) * seq, blk * LANES:(blk + 1) * LANES].T
                        for hh in range(2):
                            head = c * (PROJ_CHUNK // HEAD_DIM) + 2 * blk + hh
                            cache_ref[bb, 0, head] = t[hh * HEAD_DIM:(hh + 1) * HEAD_DIM]
    for c in range(chunks):
        cols = slice(c * PROJ_CHUNK, (c + 1) * PROJ_CHUNK)
        q_ref[:, cols] = (_nn(hb, w_ref[:, cols]) * q_scale).astype(BF16)


def _in_proj_odd(x2d, mods, layer, norm_g, w_bf, *, n_batch, seq_len, latent):
    n = x2d.shape[0]
    tm = PROJ_ROWS
    width = NA_W
    bidx = _batch_index_fn(latent, tm, seq_len)
    row = lambda i: (i, 0)
    const = lambda i: (0, 0)
    in_specs = [pl.BlockSpec((tm, D_MODEL), row), _mod_spec(layer, 0, bidx), _mod_spec(layer, 1, bidx),
                pl.BlockSpec((1, D_MODEL), const), pl.BlockSpec(w_bf.shape, const)]
    out_shape = [jax.ShapeDtypeStruct((n, width), BF16)] * 3
    out_specs = [pl.BlockSpec((tm, width), row)] * 3
    write_cache = not latent
    if write_cache:
        cache = jax.ShapeDtypeStruct((n_batch, 1, NA_HEADS, HEAD_DIM, seq_len), F32)
        out_shape += [cache, cache]
        out_specs += [pl.BlockSpec((tm // seq_len, 1, NA_HEADS, HEAD_DIM, seq_len),
                                   lambda i: (i, 0, 0, 0, 0))] * 2
    return pl.pallas_call(
        functools.partial(_in_odd_kernel, write_cache=write_cache, tm=tm),
        out_shape=out_shape,
        grid=(n // tm,),
        in_specs=in_specs,
        out_specs=out_specs,
        compiler_params=_params(1),
        name="in_proj_odd_latent" if latent else "in_proj_odd_ctx",
    )(x2d, mods, mods, norm_g, w_bf)


def _dense_pairs_kernel(q_ref, k_ref, v_ref, o_ref, *, seq_len):
    n_seq = q_ref.shape[0] // seq_len
    units = [(b, p, half) for b in range(n_seq) for p in range(NA_HEADS // 2) for half in range(2)]
    outs = {}

    def scores_of(unit):
        b, p, half = unit
        rows, cols = slice(b * seq_len, (b + 1) * seq_len), slice(p * LANES, (p + 1) * LANES)
        return [_nt(q_ref[rows, cols], _masked_half(k_ref[rows, cols], half))]

    def finish(unit, scores):
        b, p, half = unit
        rows, cols = slice(b * seq_len, (b + 1) * seq_len), slice(p * LANES, (p + 1) * LANES)
        outs[half] = _softmax_apply(scores, [v_ref[rows, cols]])
        if half == 1:
            o_ref[rows, cols] = jnp.where(_lane_lo(seq_len), outs[0], outs[1]).astype(BF16)

    _run_pipelined(units, scores_of, finish)


def _dense_attention_ctx(q, k, v, *, n_batch, seq_len):
    width = NA_W
    n_seq = DENSE_CTX_SEQS
    spec = pl.BlockSpec((n_seq * seq_len, width), lambda b: (b, 0))
    return pl.pallas_call(
        functools.partial(_dense_pairs_kernel, seq_len=seq_len),
        out_shape=jax.ShapeDtypeStruct(q.shape, BF16),
        grid=(n_batch // n_seq,),
        in_specs=[spec, spec, spec],
        out_specs=spec,
        compiler_params=_params(1),
        name="dense_attention_ctx",
    )(q, k, v)


NA_Q_ROWS = 8
NA_KEY_ROWS = 12


def _na_window_start(r, n_rows):
    return min(max(r - NA_KH // 2, 0), n_rows - NA_KH)


def _na_bias_tile(bias_ref, half, tile, n_rows):
    r0 = tile * NA_Q_ROWS
    kr0 = min(max(r0 - NA_KH // 2, 0), n_rows - NA_KEY_ROWS)
    lo = _lane_lo(GRID_W)
    neg_block = jnp.full((GRID_W, LANES), NEG, F32)
    left_off = jnp.where(lo, NEG, 0.0)
    right_off = jnp.where(lo, 0.0, NEG)
    rows = []
    for rq in range(NA_Q_ROWS):
        r = r0 + rq
        rs = _na_window_start(r, n_rows)
        blocks = []
        for kk in range(NA_KEY_ROWS // 2):
            ka = kr0 + 2 * kk
            va = rs <= ka < rs + NA_KH
            vb = rs <= ka + 1 < rs + NA_KH
            if not (va or vb):
                blocks.append(neg_block)
                continue
            blk = bias_ref[half, ka - r + NA_KH]
            if not va:
                blk = blk + left_off
            if not vb:
                blk = blk + right_off
            blocks.append(blk)
        rows.append(jnp.concatenate(blocks, axis=1))
    return jnp.concatenate(rows, axis=0), kr0


def _na_kernel(q_ref, k_ref, v_ref, ck_ref, cv_ref, bias_ref, o_ref, *, n_rows):
    tq = NA_Q_ROWS * GRID_W
    span = NA_KEY_ROWS * GRID_W
    units = [(tile, half) for tile in range(n_rows // NA_Q_ROWS) for half in range(2)]
    outs = {}

    def window(tile):
        kr0 = min(max(tile * NA_Q_ROWS - NA_KH // 2, 0), n_rows - NA_KEY_ROWS)
        return slice(kr0 * GRID_W, kr0 * GRID_W + span)

    def scores_of(unit):
        tile, half = unit
        q = q_ref[tile * tq:(tile + 1) * tq, :]
        bias, _ = _na_bias_tile(bias_ref, half, tile, n_rows)
        ck_t = jnp.concatenate([ck_ref[0], ck_ref[1]], axis=0)
        own = (lax.broadcasted_iota(jnp.int32, ck_t.shape, 0) < HEAD_DIM) == (half == 0)
        return [_nt(q, _masked_half(k_ref[window(tile), :], half)) + bias,
                _nn(q, jnp.where(own, ck_t, 0.0).astype(BF16))]

    def finish(unit, scores):
        tile, half = unit
        cv_t = jnp.concatenate([cv_ref[0], cv_ref[1]], axis=0).astype(BF16)
        outs[half] = _softmax_apply(scores, [v_ref[window(tile), :], ("t", cv_t)])
        if half == 1:
            o_ref[tile * tq:(tile + 1) * tq, :] = jnp.where(_lane_lo(tq), outs[0], outs[1]).astype(BF16)

    _run_pipelined(units, scores_of, finish)


def _na_bias_table(rpb):
    cidx = np.arange(GRID_W)
    cs = np.clip(cidx - NA_KW // 2, 0, GRID_W - NA_KW)
    kc = np.arange(GRID_W)
    inside = (kc[None, :] >= cs[:, None]) & (kc[None, :] < cs[:, None] + NA_KW)
    rel = kc[None, :] - cidx[:, None] + NA_KW - 1
    onehot = (rel[None] == np.arange(2 * NA_KW - 1)[:, None, None]) & inside[None]
    m = jnp.einsum("hdj,jck->hdck", rpb * LOG2E, jnp.asarray(onehot, F32), precision=lax.Precision.HIGHEST)
    m = jnp.where(jnp.asarray(inside)[None, None], m, NEG)
    neg = jnp.full((rpb.shape[0], 1, GRID_W, GRID_W), NEG, F32)
    left = jnp.concatenate([neg, m], axis=1)
    right = jnp.concatenate([m, neg], axis=1)
    return jnp.concatenate([left, right], axis=-1)


def _na_attention(q, k, v, ctx_k, ctx_v, bias_tab, *, n_batch, seq_len):
    pairs = NA_HEADS // 2
    tok = lambda p, b: (b, p)
    ctx = lambda p, b: (b, p, 0, 0)
    past = ctx_k.shape[-1]
    return pl.pallas_call(
        functools.partial(_na_kernel, n_rows=seq_len // GRID_W),
        out_shape=jax.ShapeDtypeStruct(q.shape, BF16),
        grid=(pairs, n_batch),
        in_specs=[pl.BlockSpec((seq_len, LANES), tok)] * 3
        + [pl.BlockSpec((None, 2, HEAD_DIM, past), ctx)] * 2
        + [pl.BlockSpec((2, 2 * NA_KH, GRID_W, LANES), lambda p, b: (p, 0, 0, 0))],
        out_specs=pl.BlockSpec((seq_len, LANES), tok),
        compiler_params=_params(2),
        name="neighbourhood_attention",
    )(q, k, v, ctx_k, ctx_v, bias_tab)


def _rope_tables(n):
    t = np.arange(n)
    row = (t // GRID_W).astype(np.float64)
    col = (t % GRID_W).astype(np.float64)
    half = HEAD_DIM // 2
    inv = ROPE_BASE ** (-np.arange(0, half, 2, dtype=np.float64) / half)
    ang_r = row[:, None] * inv
    ang_c = col[:, None] * inv
    cos_h = np.concatenate([np.cos(ang_r)] * 2 + [np.cos(ang_c)] * 2, axis=-1)
    sin_h = np.concatenate([-np.sin(ang_r), np.sin(ang_r), -np.sin(ang_c), np.sin(ang_c)], axis=-1)
    return (jnp.asarray(np.concatenate([cos_h, cos_h], axis=-1), F32),
            jnp.asarray(np.concatenate([sin_h, sin_h], axis=-1), F32))


def _ffn_weights(w_up, conv_w, conv_b, w_down):
    return w_up.astype(BF16), conv_w, conv_b[:, None, :], w_down.astype(BF16)


def _token_major_dup(cache):
    b, kv, t, d = cache.shape
    c = jnp.transpose(cache, (0, 2, 1, 3))[:, :, :, None, :]
    return jnp.broadcast_to(c, (b, t, kv, 2, d)).reshape(b, t, kv * 2 * d).astype(BF16)


def _head_transposed(cache):
    return jnp.swapaxes(cache, -1, -2)


def kernel(x_prompt, x_sample, state_ret_fwd, state_ret_bwd, cache_gqa_k, cache_gqa_v, cache_na_k, cache_na_v,
           c, c_ctx, ada_w, ada_b, norm_mix, norm_ffn, norm_final, even_w_in, even_w_out, ret_decay_fwd,
           ret_decay_bwd, ret_gn, gqa_q_norm, gqa_k_norm, odd_w_in, odd_w_out, na_rpb, ffn_w_up, ffn_conv_w,
           ffn_conv_b, ffn_w_down):
    nb_c, len_c, _ = x_prompt.shape
    nb_s, len_s, _ = x_sample.shape
    depth = ada_w.shape[0]
    streams = {
        False: dict(n_batch=nb_c, seq_len=len_c),
        True: dict(n_batch=nb_s, seq_len=len_s),
    }
    xs = {False: x_prompt.reshape(nb_c * len_c, D_MODEL), True: x_sample.reshape(nb_s * len_s, D_MODEL)}

    rows = 8 * (-(-(1 + nb_s) // 8))
    cond = jnp.zeros((rows, D_MODEL), F32).at[0].set(c_ctx).at[1:1 + nb_s].set(c)
    mods = _ada_params(cond, ada_w, ada_b).reshape(depth, rows, 6, 1, D_MODEL)
    rope_tabs = _rope_tables(len_s)
    ffn_w = _ffn_weights(ffn_w_up, ffn_conv_w, ffn_conv_b, ffn_w_down)
    outs = {}
    mixed = {}

    for l in range(depth):
        g_mix = norm_mix[l][None, :]
        g_ffn = norm_ffn[l][None, :]
        if l % 2 == 0:
            e = l // 2
            w_in = even_w_in[e].astype(BF16)
            w_out = even_w_out[e].astype(BF16)
            w_out_parts = [w_out[:RET_V_W], w_out[RET_V_W:]]
            log_g = jnp.stack([jax.nn.log_sigmoid(ret_decay_fwd[e].astype(F32)),
                               jax.nn.log_sigmoid(ret_decay_bwd[e].astype(F32))])
            gn = ret_gn[e][None, :]
            qg2 = jnp.tile(gqa_q_norm[e], 2)[None, :]
            kg2 = jnp.tile(gqa_k_norm[e], 2)[None, :]
            for latent in (False, True):
                st = streams[latent]
                res = _in_proj_even(xs[latent], mods, l, g_mix, w_in, qg2, kg2, rope_tabs, latent=latent, **st)
                qr, kr, vr, gr, qa, kd, vd = res[:7]
                if latent:
                    ret = _retention(log_g, qr, kr, vr, gr, gn, state_ret_fwd[:, e:e + 1],
                                     state_ret_bwd[:, e:e + 1], write_state=False, **st)[0]
                    att = _gqa_attention(qa, kd, vd, _token_major_dup(cache_gqa_k[:, e]),
                                         _token_major_dup(cache_gqa_v[:, e]), tq=GQA_Q_ROWS, **st)
                else:
                    outs.setdefault("gk", []).append(res[7])
                    outs.setdefault("gv", []).append(res[8])
                    ret, s_f, s_b = _retention(log_g, qr, kr, vr, gr, gn, None, None, write_state=True, **st)
                    outs.setdefault("sf", []).append(s_f)
                    outs.setdefault("sb", []).append(s_b)
                    att = _gqa_attention(qa, kd, vd, None, None, tq=st["seq_len"], **st)
                mixed[latent] = ([ret, att], w_out_parts)
        else:
            o = l // 2
            w_in = odd_w_in[o].astype(BF16)
            w_out = odd_w_out[o].astype(BF16)
            for latent in (False, True):
                st = streams[latent]
                res = _in_proj_odd(xs[latent], mods, l, g_mix, w_in, latent=latent, **st)
                q, k, v = res[:3]
                if latent:
                    att = _na_attention(q, k, v, _head_transposed(cache_na_k[:, o]), _head_transposed(cache_na_v[:, o]),
                                        _na_bias_table(na_rpb[o]), **st)
                else:
                    outs.setdefault("nk", []).append(res[3])
                    outs.setdefault("nv", []).append(res[4])
                    att = _dense_attention_ctx(q, k, v, **st)
                mixed[latent] = ([att], [w_out])
        final_g = norm_final[None, :] if l == depth - 1 else None
        for latent in (False, True):
            xs[latent] = _mix_ffn(xs[latent], mods, l, *mixed[latent], g_ffn, *ffn_w, final_g,
                                   seq_len=streams[latent]["seq_len"], latent=latent)

    tr = lambda a: jnp.swapaxes(a, -1, -2)
    cat = lambda name: outs[name][0] if len(outs[name]) == 1 else jnp.concatenate(outs[name], axis=1)
    return (xs[False].reshape(nb_c, len_c, D_MODEL), xs[True].reshape(nb_s, len_s, D_MODEL),
            cat("sf"), cat("sb"), tr(cat("gk")), tr(cat("gv")), tr(cat("nk")), tr(cat("nv")))
```

```python
import functools

import numpy as np
import jax
import jax.numpy as jnp
from jax import lax
from jax.experimental import pallas as pl
from jax.experimental.pallas import tpu as pltpu

F32 = jnp.float32
BF16 = jnp.bfloat16

D_MODEL = 1024
GRID_W = 64
HEAD_DIM = 64
ROPE_BASE = 10000.0
EPS = 1e-6
GN_EPS = 1e-5
RET_HEADS = 8
RET_DK = 64
RET_DV = 128
RET_CHUNK = 128
GQA_HEADS = 8
GQA_KV_HEADS = 2
NA_HEADS = 16
NA_KH = 8
NA_KW = 16
D_FF = 2816
LANES = 128
MXU_DIM = 256
FF_CHUNK = MXU_DIM
N_FF_CHUNKS = D_FF // FF_CHUNK
PROJ_CHUNK = 2 * MXU_DIM
RET_QK_W = RET_HEADS * RET_DK
RET_V_W = RET_HEADS * RET_DV
GQA_Q_W = GQA_HEADS * HEAD_DIM
GQA_KV_W = GQA_KV_HEADS * HEAD_DIM
NA_W = NA_HEADS * HEAD_DIM
OFF_QR, OFF_KR, OFF_VR, OFF_GR, OFF_QA, OFF_KA, OFF_VA, EVEN_IN_W = (
    int(v) for v in np.cumsum([0, RET_QK_W, RET_QK_W, RET_V_W, RET_V_W, GQA_Q_W, GQA_KV_W, GQA_KV_W]))
PROJ_ROWS = 512
FFN_ROWS = 1024
RET_ROWS = 2048
GQA_Q_ROWS = 512
GQA_CTX_SEQS = 2
DENSE_CTX_SEQS = 4
ADA_COLS = 2048
NEG = -1e30
LOG2E = 1.4426950408889634
VMEM_LIMIT = 56 * 1024 * 1024


def _nn(a, b):
    return jnp.dot(a, b, preferred_element_type=F32)


def _nt(a, b):
    return lax.dot_general(a, b, (((1,), (1,)), ((), ())), preferred_element_type=F32)


def _tn(a, b):
    return lax.dot_general(a, b, (((0,), (0,)), ((), ())), preferred_element_type=F32)


def _sigmoid(x):
    return 1.0 / (1.0 + jnp.exp(-x))


def _params(n_axes):
    return pltpu.CompilerParams(dimension_semantics=("arbitrary",) * n_axes, vmem_limit_bytes=VMEM_LIMIT)


def _modulated_norm(x, g, scale, shift):
    ms = jnp.mean(x * x, axis=-1, keepdims=True)
    return (x * lax.rsqrt(ms + EPS) * g) * (1.0 + scale) + shift


def _lane_lo(rows):
    return lax.broadcasted_iota(jnp.int32, (rows, LANES), 1) < HEAD_DIM


def _head_rms_norm(xb, gain, lo):
    sq = xb * xb
    s_lo = jnp.sum(jnp.where(lo, sq, 0.0), axis=-1, keepdims=True)
    s_hi = jnp.sum(jnp.where(lo, 0.0, sq), axis=-1, keepdims=True)
    r = jnp.where(lo, lax.rsqrt(s_lo * (1.0 / HEAD_DIM) + EPS), lax.rsqrt(s_hi * (1.0 / HEAD_DIM) + EPS))
    return xb * r * gain


def _rope(xb, cos, sin_signed, first16):
    partner = jnp.where(first16, pltpu.roll(xb, LANES - 16, 1), pltpu.roll(xb, 16, 1))
    return xb * cos + partner * sin_signed


def _softmax_apply(scores, values):
    m = functools.reduce(jnp.maximum, [jnp.max(s, axis=-1, keepdims=True) for s in scores])
    es = [jnp.exp2(s - m) for s in scores]
    l = functools.reduce(jnp.add, [jnp.sum(e, axis=-1, keepdims=True) for e in es])
    o = None
    for e, v in zip(es, values):
        t = _nt(e.astype(BF16), v[1]) if isinstance(v, tuple) else _nn(e.astype(BF16), v)
        o = t if o is None else o + t
    return o * (1.0 / l)


def _run_pipelined(units, scores_of, finish):
    pending = scores_of(units[0])
    for n, unit in enumerate(units):
        nxt = scores_of(units[n + 1]) if n + 1 < len(units) else None
        finish(unit, pending)
        pending = nxt


def _masked_half(kp, half):
    lo = _lane_lo(kp.shape[0])
    return jnp.where(lo if half == 0 else jnp.logical_not(lo), kp, jnp.zeros_like(kp))


def _ada_kernel(c_ref, w_ref, b_ref, o_ref):
    c = c_ref[...]
    a = (c * _sigmoid(c)).astype(BF16)
    o_ref[...] = _nn(a, w_ref[...].astype(BF16)) + b_ref[...]


def _ada_params(cond, ada_w, ada_b):
    depth = ada_w.shape[0]
    rows = cond.shape[0]
    tn = ADA_COLS
    return pl.pallas_call(
        _ada_kernel,
        out_shape=jax.ShapeDtypeStruct((depth, rows, 6 * D_MODEL), F32),
        grid=(depth, 6 * D_MODEL // tn),
        in_specs=[
            pl.BlockSpec((rows, D_MODEL), lambda l, j: (0, 0)),
            pl.BlockSpec((None, D_MODEL, tn), lambda l, j: (l, 0, j)),
            pl.BlockSpec((None, 1, tn), lambda l, j: (l, 0, j)),
        ],
        out_specs=pl.BlockSpec((None, rows, tn), lambda l, j: (l, 0, j)),
        compiler_params=_params(2),
        name="ada_params",
    )(cond, ada_w, ada_b.reshape(depth, 1, 6 * D_MODEL))


def _mod_spec(layer, which, bidx):
    return pl.BlockSpec((None, None, None, 1, D_MODEL), lambda i: (layer, bidx(i), which, 0, 0))


def _batch_index_fn(latent, rows_per_tile, seq_len):
    if not latent:
        return lambda i: 0
    return lambda i: 1 + (i * rows_per_tile) // seq_len


def _in_even_kernel(*refs, latent, tm):
    if latent:
        (x_ref, shift_ref, scale_ref, g_ref, w_ref, qg_ref, kg_ref, cos_ref, sin_ref,
         qr_ref, kr_ref, vr_ref, gr_ref, qa_ref, kd_ref, vd_ref) = refs
    else:
        (x_ref, shift_ref, scale_ref, g_ref, w_ref, qg_ref, kg_ref,
         qr_ref, kr_ref, vr_ref, gr_ref, qa_ref, kd_ref, vd_ref, ck_ref, cv_ref) = refs
    hb = _modulated_norm(x_ref[...], g_ref[...], scale_ref[...], shift_ref[...]).astype(BF16)
    lane = lax.broadcasted_iota(jnp.int32, (tm, LANES), 1)
    lo = lane < HEAD_DIM
    if latent:
        cos = cos_ref[...]
        sin = sin_ref[...]
        first16 = (lane % 32) < 16
        rope = lambda v: _rope(v, cos, sin, first16)
    else:
        rope = lambda v: v
    dk_scale = RET_DK ** -0.5
    q_scale = HEAD_DIM ** -0.5 * LOG2E

    r = _nn(hb, w_ref[:, OFF_QA:OFF_QA + GQA_Q_W])
    qg = qg_ref[...]
    for b in range(GQA_Q_W // LANES):
        blk = rope(_head_rms_norm(r[:, b * LANES:(b + 1) * LANES], qg, lo)) * q_scale
        qa_ref[:, b * LANES:(b + 1) * LANES] = blk.astype(BF16)
    assert GQA_KV_W == LANES
    r = _nn(hb, w_ref[:, OFF_KA:OFF_VA + GQA_KV_W])
    kn = _head_rms_norm(r[:, 0:LANES], kg_ref[...], lo)
    vn = r[:, LANES:2 * LANES]
    if not latent:
        seq = ck_ref.shape[4]
        for bb in range(tm // seq):
            for src, dst in ((kn, ck_ref), (vn, cv_ref)):
                t = src[bb * seq:(bb + 1) * seq, :].T
                for kv in range(GQA_KV_HEADS):
                    dst[bb, 0, kv] = t[kv * HEAD_DIM:(kv + 1) * HEAD_DIM]
    kn = rope(kn)
    for src, dst in ((kn, kd_ref), (vn, vd_ref)):
        sw = pltpu.roll(src, HEAD_DIM, 1)
        dst[:, 0:LANES] = jnp.where(lo, src, sw).astype(BF16)
        dst[:, LANES:2 * LANES] = jnp.where(lo, sw, src).astype(BF16)
    r = _nn(hb, w_ref[:, OFF_QR:OFF_QR + RET_QK_W])
    for b in range(RET_QK_W // LANES):
        qr_ref[:, b * LANES:(b + 1) * LANES] = rope(r[:, b * LANES:(b + 1) * LANES])
    r = _nn(hb, w_ref[:, OFF_KR:OFF_KR + RET_QK_W]) * dk_scale
    for b in range(RET_QK_W // LANES):
        kr_ref[:, b * LANES:(b + 1) * LANES] = rope(r[:, b * LANES:(b + 1) * LANES])
    for c in range(RET_V_W // PROJ_CHUNK):
        cols = slice(c * PROJ_CHUNK, (c + 1) * PROJ_CHUNK)
        r = _nn(hb, w_ref[:, OFF_GR + c * PROJ_CHUNK:OFF_GR + (c + 1) * PROJ_CHUNK])
        gr_ref[:, cols] = r * _sigmoid(r)
    for c in range(RET_V_W // PROJ_CHUNK):
        cols = slice(c * PROJ_CHUNK, (c + 1) * PROJ_CHUNK)
        vr_ref[:, cols] = _nn(hb, w_ref[:, OFF_VR + c * PROJ_CHUNK:OFF_VR + (c + 1) * PROJ_CHUNK]).astype(BF16)


def _in_proj_even(x2d, mods, layer, norm_g, w_bf, q_gain2, k_gain2, rope_tabs, *, n_batch, seq_len, latent):
    n = x2d.shape[0]
    tm = PROJ_ROWS
    assert w_bf.shape == (D_MODEL, EVEN_IN_W)
    bidx = _batch_index_fn(latent, tm, seq_len)
    row = lambda i: (i, 0)
    const = lambda i: (0, 0)
    in_specs = [
        pl.BlockSpec((tm, D_MODEL), row),
        _mod_spec(layer, 0, bidx),
        _mod_spec(layer, 1, bidx),
        pl.BlockSpec((1, D_MODEL), const),
        pl.BlockSpec(w_bf.shape, const),
        pl.BlockSpec((1, LANES), const),
        pl.BlockSpec((1, LANES), const),
    ]
    args = [x2d, mods, mods, norm_g, w_bf, q_gain2, k_gain2]
    if latent:
        tiles_per_seq = seq_len // tm
        in_specs += [pl.BlockSpec((tm, LANES), lambda i: (i % tiles_per_seq, 0))] * 2
        args += list(rope_tabs)
    out_shape = [
        jax.ShapeDtypeStruct((n, RET_QK_W), F32),
        jax.ShapeDtypeStruct((n, RET_QK_W), F32),
        jax.ShapeDtypeStruct((n, RET_V_W), BF16),
        jax.ShapeDtypeStruct((n, RET_V_W), F32),
        jax.ShapeDtypeStruct((n, GQA_Q_W), BF16),
        jax.ShapeDtypeStruct((n, 2 * GQA_KV_W), BF16),
        jax.ShapeDtypeStruct((n, 2 * GQA_KV_W), BF16),
    ]
    out_specs = [pl.BlockSpec((tm, s.shape[1]), row) for s in out_shape]
    if not latent:
        cache = jax.ShapeDtypeStruct((n_batch, 1, GQA_KV_HEADS, HEAD_DIM, seq_len), F32)
        out_shape += [cache, cache]
        out_specs += [pl.BlockSpec((tm // seq_len, 1, GQA_KV_HEADS, HEAD_DIM, seq_len),
                                   lambda i: (i, 0, 0, 0, 0))] * 2
    return pl.pallas_call(
        functools.partial(_in_even_kernel, latent=latent, tm=tm),
        out_shape=out_shape,
        grid=(n // tm,),
        in_specs=in_specs,
        out_specs=out_specs,
        compiler_params=_params(1),
        name="in_proj_even_latent" if latent else "in_proj_even_ctx",
    )(*args)


def _retention_kernel(*refs, n, nb, has_state, write_state):
    lg_ref, q_ref, k_ref, v_ref, gr_ref, gn_ref = refs[:6]
    refs = refs[6:]
    if has_state:
        s0f_ref, s0b_ref = refs[:2]
        refs = refs[2:]
    o_ref = refs[0]
    if write_state:
        sf_ref, sb_ref = refs[1:3]
    c = RET_CHUNK
    assert c == LANES and 2 * RET_DK == LANES
    nc = n // c
    p = pl.program_id(0)
    lgf = [lg_ref[0, 2 * p + hh] for hh in range(2)]
    lgb = [lg_ref[1, 2 * p + hh] for hh in range(2)]
    row = lax.broadcasted_iota(jnp.int32, (c, c), 0)
    col = lax.broadcasted_iota(jnp.int32, (c, c), 1)
    diff = (row - col).astype(F32)
    pos = row.astype(F32)
    lane_lo = col < RET_DK

    def both_scans(f, b):
        return (jnp.where(diff >= 0, jnp.exp(f * jnp.maximum(diff, 0.0)), 0.0)
                + jnp.where(diff <= 0, jnp.exp(b * jnp.maximum(-diff, 0.0)), 0.0))

    decay2 = jnp.concatenate([both_scans(lgf[0], lgb[0]), both_scans(lgf[1], lgb[1])], axis=1)
    lgf_lane = jnp.where(lane_lo, lgf[0], lgf[1])
    lgb_lane = jnp.where(lane_lo, lgb[0], lgb[1])
    qd_f = jnp.exp(lgf_lane * (pos + 1.0))
    kd_f = jnp.exp(lgf_lane * (c - 1.0 - pos))
    qd_b = jnp.exp(lgb_lane * (c - pos))
    kd_b = jnp.exp(lgb_lane * pos)
    srow = lax.broadcasted_iota(jnp.int32, (c, 2 * RET_DV), 0)
    scol = lax.broadcasted_iota(jnp.int32, (c, 2 * RET_DV), 1)
    row_a = srow < RET_DK
    col_a = scol < RET_DV
    own = row_a == col_a
    cd_f = jnp.exp(jnp.where(row_a, lgf[0], lgf[1]) * float(c))
    cd_b = jnp.exp(jnp.where(row_a, lgb[0], lgb[1]) * float(c))
    zeros_half = jnp.zeros((RET_DK, RET_DV), F32)
    gn = gn_ref[...]

    def place(s0_ref, bi):
        top = jnp.concatenate([s0_ref[bi, 0, 0], zeros_half], axis=1)
        bot = jnp.concatenate([zeros_half, s0_ref[bi, 0, 1]], axis=1)
        return jnp.concatenate([top, bot], axis=0)

    for bi in range(nb):
        rows = [slice(bi * n + i * c, bi * n + (i + 1) * c) for i in range(nc)]
        if has_state:
            s_f = place(s0f_ref, bi)
            s_b = place(s0b_ref, bi)
        else:
            s_f = jnp.zeros((c, 2 * RET_DV), F32)
            s_b = jnp.zeros((c, 2 * RET_DV), F32)
        before_f = []
        for i in range(nc):
            before_f.append(s_f)
            kv = _tn((k_ref[rows[i], :] * kd_f).astype(BF16), v_ref[rows[i], :])
            s_f = s_f * cd_f + jnp.where(own, kv, 0.0)
        before_b = [None] * nc
        for i in reversed(range(nc)):
            before_b[i] = s_b
            kv = _tn((k_ref[rows[i], :] * kd_b).astype(BF16), v_ref[rows[i], :])
            s_b = s_b * cd_b + jnp.where(own, kv, 0.0)
        if write_state:
            for hh in range(2):
                blk = (slice(hh * RET_DK, (hh + 1) * RET_DK), slice(hh * RET_DV, (hh + 1) * RET_DV))
                sf_ref[bi, 0, hh] = s_f[blk]
                sb_ref[bi, 0, hh] = s_b[blk]
        for i in range(nc):
            qc = q_ref[rows[i], :]
            kc = k_ref[rows[i], :]
            vc = v_ref[rows[i], :]
            k_cat = jnp.concatenate([jnp.where(lane_lo, kc, 0.0), jnp.where(lane_lo, 0.0, kc)], axis=0)
            v_blk = jnp.concatenate([jnp.where(col_a, vc, jnp.zeros_like(vc)),
                                     jnp.where(col_a, jnp.zeros_like(vc), vc)], axis=0)
            scores = _nt(qc.astype(BF16), k_cat.astype(BF16)) * decay2
            q_cat = jnp.concatenate([(qc * qd_f).astype(BF16), (qc * qd_b).astype(BF16)], axis=1)
            s_cat = jnp.concatenate([before_f[i], before_b[i]], axis=0).astype(BF16)
            o = _nn(scores.astype(BF16), v_blk) + _nn(q_cat, s_cat)
            for hh in range(2):
                vcols = slice(hh * RET_DV, (hh + 1) * RET_DV)
                oh = o[:, vcols]
                mu = jnp.mean(oh, axis=-1, keepdims=True)
                d = oh - mu
                var = jnp.mean(d * d, axis=-1, keepdims=True)
                y = d * lax.rsqrt(var + GN_EPS) * gn[:, vcols] * gr_ref[rows[i], vcols]
                o_ref[rows[i], vcols] = y.astype(BF16)


def _retention(log_g, qr, kr, vr, gr, gn, state_f, state_b, *, n_batch, seq_len, write_state):
    n = qr.shape[0]
    pairs = RET_HEADS // 2
    has_state = state_f is not None
    nb = max(1, RET_ROWS // seq_len)
    rows = nb * seq_len
    tok = lambda p, g: (g, p)
    in_specs = [
        pl.BlockSpec(memory_space=pltpu.SMEM),
        pl.BlockSpec((rows, LANES), tok),
        pl.BlockSpec((rows, LANES), tok),
        pl.BlockSpec((rows, 2 * RET_DV), tok),
        pl.BlockSpec((rows, 2 * RET_DV), tok),
        pl.BlockSpec((1, 2 * RET_DV), lambda p, g: (0, p)),
    ]
    args = [log_g, qr, kr, vr, gr, gn]
    state_spec = pl.BlockSpec((nb, 1, 2, RET_DK, RET_DV), lambda p, g: (g, 0, p, 0, 0))
    if has_state:
        in_specs += [state_spec, state_spec]
        args += [state_f, state_b]
    out_shape = [jax.ShapeDtypeStruct((n, RET_HEADS * RET_DV), BF16)]
    out_specs = [pl.BlockSpec((rows, 2 * RET_DV), tok)]
    if write_state:
        st = jax.ShapeDtypeStruct((n_batch, 1, RET_HEADS, RET_DK, RET_DV), F32)
        out_shape += [st, st]
        out_specs += [state_spec, state_spec]
    return pl.pallas_call(
        functools.partial(_retention_kernel, n=seq_len, nb=nb, has_state=has_state, write_state=write_state),
        out_shape=out_shape,
        grid=(pairs, n_batch // nb),
        in_specs=in_specs,
        out_specs=out_specs,
        compiler_params=_params(2),
        name="retention_latent" if has_state else "retention_ctx",
    )(*args)


def _gqa_kernel(*refs, n_src, tq, n_seq):
    q_ref = refs[0]
    k_refs = refs[1:1 + 2 * n_src:2]
    v_refs = refs[2:2 + 2 * n_src:2]
    o_ref = refs[1 + 2 * n_src]
    units = [(b, g, half) for b in range(n_seq) for g in range(GQA_KV_HEADS) for half in range(2)]
    outs = {}

    def kv_rows(ref, b):
        n_keys = ref.shape[0] // n_seq
        return slice(b * n_keys, (b + 1) * n_keys)

    def scores_of(unit):
        b, g, half = unit
        rows, base = slice(b * tq, (b + 1) * tq), g * 2 * LANES
        q = jnp.concatenate([q_ref[rows, base:base + LANES], q_ref[rows, base + LANES:base + 2 * LANES]], axis=0)
        return [_nt(q, _masked_half(k_ref[kv_rows(k_ref, b), g * LANES:(g + 1) * LANES], half)) for k_ref in k_refs]

    def finish(unit, scores):
        b, g, half = unit
        outs[half] = _softmax_apply(scores, [v_ref[kv_rows(v_ref, b), g * LANES:(g + 1) * LANES] for v_ref in v_refs])
        if half == 1:
            rows, base = slice(b * tq, (b + 1) * tq), g * 2 * LANES
            o = jnp.where(_lane_lo(2 * tq), outs[0], outs[1]).astype(BF16)
            o_ref[rows, base:base + LANES] = o[:tq]
            o_ref[rows, base + LANES:base + 2 * LANES] = o[tq:]

    _run_pipelined(units, scores_of, finish)


def _gqa_attention(qa, kd, vd, ctx_kd, ctx_vd, *, n_batch, seq_len, tq):
    n = qa.shape[0]
    tiles = seq_len // tq
    n_src = 1 if ctx_kd is None else 2
    n_seq = GQA_CTX_SEQS if (tiles == 1 and n_src == 1) else 1
    qmap = lambda b, t: (b * tiles + t, 0)
    kmap = lambda b, t: (b, 0)
    in_specs = [pl.BlockSpec((n_seq * tq, GQA_Q_W), qmap),
                pl.BlockSpec((n_seq * seq_len, 2 * LANES), kmap),
                pl.BlockSpec((n_seq * seq_len, 2 * LANES), kmap)]
    args = [qa, kd, vd]
    if n_src == 2:
        past = ctx_kd.shape[1]
        cmap = lambda b, t: (b, 0, 0)
        in_specs += [pl.BlockSpec((None, past, 2 * LANES), cmap)] * 2
        args += [ctx_kd, ctx_vd]
    return pl.pallas_call(
        functools.partial(_gqa_kernel, n_src=n_src, tq=tq, n_seq=n_seq),
        out_shape=jax.ShapeDtypeStruct((n, GQA_Q_W), BF16),
        grid=(n_batch // n_seq, tiles),
        in_specs=in_specs,
        out_specs=pl.BlockSpec((n_seq * tq, GQA_Q_W), qmap),
        compiler_params=_params(2),
        name="gqa_latent" if n_src == 2 else "gqa_ctx",
    )(*args)


def _zero_rows(arr, rows):
    pieces, cur = [], 0
    sub = lax.broadcasted_iota(jnp.int32, (8, arr.shape[1]), 0)
    for r in sorted(rows):
        g0 = (r // 8) * 8
        if g0 > cur:
            pieces.append(arr[cur:g0])
        pieces.append(jnp.where(sub == r - g0, 0.0, arr[g0:g0 + 8]))
        cur = g0 + 8
    if cur < arr.shape[0]:
        pieces.append(arr[cur:])
    return jnp.concatenate(pieces, axis=0)


def _mix_ffn_kernel(*refs, tm, seq_len, final, n_mix):
    x_ref, gate_mix_ref, g_ref, scale_ref, shift_ref, gate_ref = refs[:6]
    m_refs = refs[6:6 + n_mix]
    w_refs = refs[6 + n_mix:6 + 2 * n_mix]
    wup_ref, cw_ref, cb_ref, wd_ref = refs[6 + 2 * n_mix:10 + 2 * n_mix]
    refs = refs[10 + 2 * n_mix:]
    if final:
        gfin_ref = refs[0]
        refs = refs[1:]
    o_ref, act_ref, hb_ref = refs
    halves = [slice(rb * (tm // 2), (rb + 1) * (tm // 2)) for rb in range(2)]
    for rows in halves:
        acc = None
        for m_ref, w_ref in zip(m_refs, w_refs):
            t = _nn(m_ref[rows, :], w_ref[...])
            acc = t if acc is None else acc + t
        y = x_ref[rows, :] + gate_mix_ref[...] * acc
        o_ref[rows, :] = y
        hb_ref[rows, :] = _modulated_norm(y, g_ref[...], scale_ref[...], shift_ref[...]).astype(BF16)
    seq_starts = list(range(0, tm, seq_len))
    seq_ends = [s + seq_len - 1 for s in seq_starts]
    for j in range(N_FF_CHUNKS):
        parts = []
        for off in (0, D_FF):
            cols = slice(off + j * FF_CHUNK, off + (j + 1) * FF_CHUNK)
            u = _nn(hb_ref[...], wup_ref[:, cols])
            cw = cw_ref[:, cols]
            prev = _zero_rows(pltpu.roll(u, 1, 0), seq_starts)
            nxt = _zero_rows(pltpu.roll(u, tm - 1, 0), seq_ends)
            parts.append(prev * cw[0:1] + u * cw[1:2] + nxt * cw[2:3] + cb_ref[:, cols])
        a, g = parts
        act_ref[:, j * FF_CHUNK:(j + 1) * FF_CHUNK] = (a * _sigmoid(a) * g).astype(BF16)
    for rows in halves:
        y = o_ref[rows, :] + gate_ref[...] * _nn(act_ref[rows, :], wd_ref[...])
        if final:
            ms = jnp.mean(y * y, axis=-1, keepdims=True)
            y = y * lax.rsqrt(ms + EPS) * gfin_ref[...]
        o_ref[rows, :] = y


def _mix_ffn(x2d, mods, layer, mixes, weights, ffn_norm_g, wup_c, cw_c, cb_c, wd_c, final_g, *, seq_len, latent):
    n = x2d.shape[0]
    tm = FFN_ROWS
    bidx = _batch_index_fn(latent, tm, seq_len)
    row = lambda i: (i, 0)
    const2 = lambda i: (0, 0)
    once = pl.Buffered(1)
    resident = lambda a: pl.BlockSpec((None,) + a.shape[1:], lambda i: (layer,) + (0,) * (a.ndim - 1),
                                      pipeline_mode=once)
    in_specs = [pl.BlockSpec((tm, D_MODEL), row), _mod_spec(layer, 2, bidx), pl.BlockSpec((1, D_MODEL), const2),
                _mod_spec(layer, 4, bidx), _mod_spec(layer, 3, bidx), _mod_spec(layer, 5, bidx)]
    in_specs += [pl.BlockSpec((tm, m.shape[1]), row) for m in mixes]
    in_specs += [pl.BlockSpec(w.shape, const2, pipeline_mode=once) for w in weights]
    in_specs += [resident(wup_c), resident(cw_c), resident(cb_c), resident(wd_c)]
    args = [x2d, mods, ffn_norm_g, mods, mods, mods, *mixes, *weights, wup_c, cw_c, cb_c, wd_c]
    final = final_g is not None
    if final:
        in_specs.append(pl.BlockSpec((1, D_MODEL), const2))
        args.append(final_g)
    return pl.pallas_call(
        functools.partial(_mix_ffn_kernel, tm=tm, seq_len=seq_len, final=final, n_mix=len(mixes)),
        out_shape=jax.ShapeDtypeStruct((n, D_MODEL), F32),
        grid=(n // tm,),
        in_specs=in_specs,
        out_specs=pl.BlockSpec((tm, D_MODEL), row),
        scratch_shapes=[pltpu.VMEM((tm, D_FF), BF16), pltpu.VMEM((tm, D_MODEL), BF16)],
        compiler_params=_params(1),
        name="mix_ffn_latent" if latent else "mix_ffn_ctx",
    )(*args)


def _in_odd_kernel(*refs, write_cache, tm):
    x_ref, shift_ref, scale_ref, g_ref, w_ref, q_ref, k_ref, v_ref = refs[:8]
    hb = _modulated_norm(x_ref[...], g_ref[...], scale_ref[...], shift_ref[...]).astype(BF16)
    q_scale = HEAD_DIM ** -0.5 * LOG2E
    chunks = NA_W // PROJ_CHUNK
    for which, dst in ((1, k_ref), (2, v_ref)):
        for c in range(chunks):
            r = _nn(hb, w_ref[:, which * NA_W + c * PROJ_CHUNK:which * NA_W + (c + 1) * PROJ_CHUNK])
            dst[:, c * PROJ_CHUNK:(c + 1) * PROJ_CHUNK] = r.astype(BF16)
            if write_cache:
                cache_ref = refs[8 + which - 1]
                seq = cache_ref.shape[4]
                for bb in range(tm // seq):
                    for blk in range(PROJ_CHUNK // LANES):
                        t = r[bb * seq:(bb + 1) * seq, blk * LANES:(blk + 1) * LANES].T
                        for hh in range(2):
                            head = c * (PROJ_CHUNK // HEAD_DIM) + 2 * blk + hh
                            cache_ref[bb, 0, head] = t[hh * HEAD_DIM:(hh + 1) * HEAD_DIM]
    for c in range(chunks):
        cols = slice(c * PROJ_CHUNK, (c + 1) * PROJ_CHUNK)
        q_ref[:, cols] = (_nn(hb, w_ref[:, cols]) * q_scale).astype(BF16)


def _in_proj_odd(x2d, mods, layer, norm_g, w_bf, *, n_batch, seq_len, latent):
    n = x2d.shape[0]
    tm = PROJ_ROWS
    width = NA_W
    bidx = _batch_index_fn(latent, tm, seq_len)
    row = lambda i: (i, 0)
    const = lambda i: (0, 0)
    in_specs = [pl.BlockSpec((tm, D_MODEL), row), _mod_spec(layer, 0, bidx), _mod_spec(layer, 1, bidx),
                pl.BlockSpec((1, D_MODEL), const), pl.BlockSpec(w_bf.shape, const)]
    out_shape = [jax.ShapeDtypeStruct((n, width), BF16)] * 3
    out_specs = [pl.BlockSpec((tm, width), row)] * 3
    write_cache = not latent
    if write_cache:
        cache = jax.ShapeDtypeStruct((n_batch, 1, NA_HEADS, HEAD_DIM, seq_len), F32)
        out_shape += [cache, cache]
        out_specs += [pl.BlockSpec((tm // seq_len, 1, NA_HEADS, HEAD_DIM, seq_len),
                                   lambda i: (i, 0, 0, 0, 0))] * 2
    return pl.pallas_call(
        functools.partial(_in_odd_kernel, write_cache=write_cache, tm=tm),
        out_shape=out_shape,
        grid=(n // tm,),
        in_specs=in_specs,
        out_specs=out_specs,
        compiler_params=_params(1),
        name="in_proj_odd_latent" if latent else "in_proj_odd_ctx",
    )(x2d, mods, mods, norm_g, w_bf)


def _dense_pairs_kernel(q_ref, k_ref, v_ref, o_ref, *, seq_len):
    n_seq = q_ref.shape[0] // seq_len
    units = [(b, p, half) for b in range(n_seq) for p in range(NA_HEADS // 2) for half in range(2)]
    outs = {}

    def scores_of(unit):
        b, p, half = unit
        rows, cols = slice(b * seq_len, (b + 1) * seq_len), slice(p * LANES, (p + 1) * LANES)
        return [_nt(q_ref[rows, cols], _masked_half(k_ref[rows, cols], half))]

    def finish(unit, scores):
        b, p, half = unit
        rows, cols = slice(b * seq_len, (b + 1) * seq_len), slice(p * LANES, (p + 1) * LANES)
        outs[half] = _softmax_apply(scores, [v_ref[rows, cols]])
        if half == 1:
            o_ref[rows, cols] = jnp.where(_lane_lo(seq_len), outs[0], outs[1]).astype(BF16)

    _run_pipelined(units, scores_of, finish)


def _dense_attention_ctx(q, k, v, *, n_batch, seq_len):
    width = NA_W
    n_seq = DENSE_CTX_SEQS
    spec = pl.BlockSpec((n_seq * seq_len, width), lambda b: (b, 0))
    return pl.pallas_call(
        functools.partial(_dense_pairs_kernel, seq_len=seq_len),
        out_shape=jax.ShapeDtypeStruct(q.shape, BF16),
        grid=(n_batch // n_seq,),
        in_specs=[spec, spec, spec],
        out_specs=spec,
        compiler_params=_params(1),
        name="dense_attention_ctx",
    )(q, k, v)


NA_Q_ROWS = 8
NA_KEY_ROWS = 12


def _na_window_start(r, n_rows):
    return min(max(r - NA_KH // 2, 0), n_rows - NA_KH)


def _na_bias_tile(bias_ref, half, tile, n_rows):
    r0 = tile * NA_Q_ROWS
    kr0 = min(max(r0 - NA_KH // 2, 0), n_rows - NA_KEY_ROWS)
    lo = _lane_lo(GRID_W)
    neg_block = jnp.full((GRID_W, LANES), NEG, F32)
    left_off = jnp.where(lo, NEG, 0.0)
    right_off = jnp.where(lo, 0.0, NEG)
    rows = []
    for rq in range(NA_Q_ROWS):
        r = r0 + rq
        rs = _na_window_start(r, n_rows)
        blocks = []
        for kk in range(NA_KEY_ROWS // 2):
            ka = kr0 + 2 * kk
            va = rs <= ka < rs + NA_KH
            vb = rs <= ka + 1 < rs + NA_KH
            if not (va or vb):
                blocks.append(neg_block)
                continue
            blk = bias_ref[half, ka - r + NA_KH]
            if not va:
                blk = blk + left_off
            if not vb:
                blk = blk + right_off
            blocks.append(blk)
        rows.append(jnp.concatenate(blocks, axis=1))
    return jnp.concatenate(rows, axis=0), kr0


def _na_kernel(q_ref, k_ref, v_ref, ck_ref, cv_ref, bias_ref, o_ref, *, n_rows):
    tq = NA_Q_ROWS * GRID_W
    span = NA_KEY_ROWS * GRID_W
    units = [(tile, half) for tile in range(n_rows // NA_Q_ROWS) for half in range(2)]
    outs = {}

    def window(tile):
        kr0 = min(max(tile * NA_Q_ROWS - NA_KH // 2, 0), n_rows - NA_KEY_ROWS)
        return slice(kr0 * GRID_W, kr0 * GRID_W + span)

    def scores_of(unit):
        tile, half = unit
        q = q_ref[tile * tq:(tile + 1) * tq, :]
        bias, _ = _na_bias_tile(bias_ref, half, tile, n_rows)
        ck_t = jnp.concatenate([ck_ref[0], ck_ref[1]], axis=0)
        own = (lax.broadcasted_iota(jnp.int32, ck_t.shape, 0) < HEAD_DIM) == (half == 0)
        return [_nt(q, _masked_half(k_ref[window(tile), :], half)) + bias,
                _nn(q, jnp.where(own, ck_t, 0.0).astype(BF16))]

    def finish(unit, scores):
        tile, half = unit
        cv_t = jnp.concatenate([cv_ref[0], cv_ref[1]], axis=0).astype(BF16)
        outs[half] = _softmax_apply(scores, [v_ref[window(tile), :], ("t", cv_t)])
        if half == 1:
            o_ref[tile * tq:(tile + 1) * tq, :] = jnp.where(_lane_lo(tq), outs[0], outs[1]).astype(BF16)

    _run_pipelined(units, scores_of, finish)


def _na_bias_table(rpb):
    cidx = np.arange(GRID_W)
    cs = np.clip(cidx - NA_KW // 2, 0, GRID_W - NA_KW)
    kc = np.arange(GRID_W)
    inside = (kc[None, :] >= cs[:, None]) & (kc[None, :] < cs[:, None] + NA_KW)
    rel = kc[None, :] - cidx[:, None] + NA_KW - 1
    onehot = (rel[None] == np.arange(2 * NA_KW - 1)[:, None, None]) & inside[None]
    m = jnp.einsum("hdj,jck->hdck", rpb * LOG2E, jnp.asarray(onehot, F32), precision=lax.Precision.HIGHEST)
    m = jnp.where(jnp.asarray(inside)[None, None], m, NEG)
    neg = jnp.full((rpb.shape[0], 1, GRID_W, GRID_W), NEG, F32)
    left = jnp.concatenate([neg, m], axis=1)
    right = jnp.concatenate([m, neg], axis=1)
    return jnp.concatenate([left, right], axis=-1)


def _na_attention(q, k, v, ctx_k, ctx_v, bias_tab, *, n_batch, seq_len):
    pairs = NA_HEADS // 2
    tok = lambda p, b: (b, p)
    ctx = lambda p, b: (b, p, 0, 0)
    past = ctx_k.shape[-1]
    return pl.pallas_call(
        functools.partial(_na_kernel, n_rows=seq_len // GRID_W),
        out_shape=jax.ShapeDtypeStruct(q.shape, BF16),
        grid=(pairs, n_batch),
        in_specs=[pl.BlockSpec((seq_len, LANES), tok)] * 3
        + [pl.BlockSpec((None, 2, HEAD_DIM, past), ctx)] * 2
        + [pl.BlockSpec((2, 2 * NA_KH, GRID_W, LANES), lambda p, b: (p, 0, 0, 0))],
        out_specs=pl.BlockSpec((seq_len, LANES), tok),
        compiler_params=_params(2),
        name="neighbourhood_attention",
    )(q, k, v, ctx_k, ctx_v, bias_tab)


def _rope_tables(n):
    t = np.arange(n)
    row = (t // GRID_W).astype(np.float64)
    col = (t % GRID_W).astype(np.float64)
    half = HEAD_DIM // 2
    inv = ROPE_BASE ** (-np.arange(0, half, 2, dtype=np.float64) / half)
    ang_r = row[:, None] * inv
    ang_c = col[:, None] * inv
    cos_h = np.concatenate([np.cos(ang_r)] * 2 + [np.cos(ang_c)] * 2, axis=-1)
    sin_h = np.concatenate([-np.sin(ang_r), np.sin(ang_r), -np.sin(ang_c), np.sin(ang_c)], axis=-1)
    return (jnp.asarray(np.concatenate([cos_h, cos_h], axis=-1), F32),
            jnp.asarray(np.concatenate([sin_h, sin_h], axis=-1), F32))


def _ffn_weights(w_up, conv_w, conv_b, w_down):
    return w_up.astype(BF16), conv_w, conv_b[:, None, :], w_down.astype(BF16)


def _token_major_dup(cache):
    b, kv, t, d = cache.shape
    c = jnp.transpose(cache, (0, 2, 1, 3))[:, :, :, None, :]
    return jnp.broadcast_to(c, (b, t, kv, 2, d)).reshape(b, t, kv * 2 * d).astype(BF16)


def _head_transposed(cache):
    return jnp.swapaxes(cache, -1, -2)


def kernel(x_prompt, x_sample, state_ret_fwd, state_ret_bwd, cache_gqa_k, cache_gqa_v, cache_na_k, cache_na_v,
           c, c_ctx, ada_w, ada_b, norm_mix, norm_ffn, norm_final, even_w_in, even_w_out, ret_decay_fwd,
           ret_decay_bwd, ret_gn, gqa_q_norm, gqa_k_norm, odd_w_in, odd_w_out, na_rpb, ffn_w_up, ffn_conv_w,
           ffn_conv_b, ffn_w_down):
    nb_c, len_c, _ = x_prompt.shape
    nb_s, len_s, _ = x_sample.shape
    depth = ada_w.shape[0]
    streams = {
        False: dict(n_batch=nb_c, seq_len=len_c),
        True: dict(n_batch=nb_s, seq_len=len_s),
    }
    xs = {False: x_prompt.reshape(nb_c * len_c, D_MODEL), True: x_sample.reshape(nb_s * len_s, D_MODEL)}

    rows = 8 * (-(-(1 + nb_s) // 8))
    cond = jnp.zeros((rows, D_MODEL), F32).at[0].set(c_ctx).at[1:1 + nb_s].set(c)
    mods = _ada_params(cond, ada_w, ada_b).reshape(depth, rows, 6, 1, D_MODEL)
    rope_tabs = _rope_tables(len_s)
    ffn_w = _ffn_weights(ffn_w_up, ffn_conv_w, ffn_conv_b, ffn_w_down)
    outs = {}
    mixed = {}

    for l in range(depth):
        g_mix = norm_mix[l][None, :]
        g_ffn = norm_ffn[l][None, :]
        if l % 2 == 0:
            e = l // 2
            w_in = even_w_in[e].astype(BF16)
            w_out = even_w_out[e].astype(BF16)
            w_out_parts = [w_out[:RET_V_W], w_out[RET_V_W:]]
            log_g = jnp.stack([jax.nn.log_sigmoid(ret_decay_fwd[e].astype(F32)),
                               jax.nn.log_sigmoid(ret_decay_bwd[e].astype(F32))])
            gn = ret_gn[e][None, :]
            qg2 = jnp.tile(gqa_q_norm[e], 2)[None, :]
            kg2 = jnp.tile(gqa_k_norm[e], 2)[None, :]
            for latent in (False, True):
                st = streams[latent]
                res = _in_proj_even(xs[latent], mods, l, g_mix, w_in, qg2, kg2, rope_tabs, latent=latent, **st)
                qr, kr, vr, gr, qa, kd, vd = res[:7]
                if latent:
                    ret = _retention(log_g, qr, kr, vr, gr, gn, state_ret_fwd[:, e:e + 1],
                                     state_ret_bwd[:, e:e + 1], write_state=False, **st)[0]
                    att = _gqa_attention(qa, kd, vd, _token_major_dup(cache_gqa_k[:, e]),
                                         _token_major_dup(cache_gqa_v[:, e]), tq=GQA_Q_ROWS, **st)
                else:
                    outs.setdefault("gk", []).append(res[7])
                    outs.setdefault("gv", []).append(res[8])
                    ret, s_f, s_b = _retention(log_g, qr, kr, vr, gr, gn, None, None, write_state=True, **st)
                    outs.setdefault("sf", []).append(s_f)
                    outs.setdefault("sb", []).append(s_b)
                    att = _gqa_attention(qa, kd, vd, None, None, tq=st["seq_len"], **st)
                mixed[latent] = ([ret, att], w_out_parts)
        else:
            o = l // 2
            w_in = odd_w_in[o].astype(BF16)
            w_out = odd_w_out[o].astype(BF16)
            for latent in (False, True):
                st = streams[latent]
                res = _in_proj_odd(xs[latent], mods, l, g_mix, w_in, latent=latent, **st)
                q, k, v = res[:3]
                if latent:
                    att = _na_attention(q, k, v, _head_transposed(cache_na_k[:, o]), _head_transposed(cache_na_v[:, o]),
                                        _na_bias_table(na_rpb[o]), **st)
                else:
                    outs.setdefault("nk", []).append(res[3])
                    outs.setdefault("nv", []).append(res[4])
                    att = _dense_attention_ctx(q, k, v, **st)
                mixed[latent] = ([att], [w_out])
        final_g = norm_final[None, :] if l == depth - 1 else None
        for latent in (False, True):
            xs[latent] = _mix_ffn(xs[latent], mods, l, *mixed[latent], g_ffn, *ffn_w, final_g,
                                   seq_len=streams[latent]["seq_len"], latent=latent)

    tr = lambda a: jnp.swapaxes(a, -1, -2)
    cat = lambda name: outs[name][0] if len(outs[name]) == 1 else jnp.concatenate(outs[name], axis=1)
    return (xs[False].reshape(nb_c, len_c, D_MODEL), xs[True].reshape(nb_s, len_s, D_MODEL),
            cat("sf"), cat("sb"), tr(cat("gk")), tr(cat("gv")), tr(cat("nk")), tr(cat("nv")))
```

```python
import functools

import numpy as np
import jax
import jax.numpy as jnp
from jax import lax
from jax.experimental import pallas as pl
from jax.experimental.pallas import tpu as pltpu

F32 = jnp.float32
BF16 = jnp.bfloat16

D_MODEL = 1024
GRID_W = 64
HEAD_DIM = 64
ROPE_BASE = 10000.0
EPS = 1e-6
GN_EPS = 1e-5
RET_HEADS = 8
RET_DK = 64
RET_DV = 128
RET_CHUNK = 128
GQA_HEADS = 8
GQA_KV_HEADS = 2
NA_HEADS = 16
NA_KH = 8
NA_KW = 16
D_FF = 2816
LANES = 128
MXU_DIM = 256
FF_CHUNK = MXU_DIM
N_FF_CHUNKS = D_FF // FF_CHUNK
PROJ_CHUNK = 2 * MXU_DIM
RET_QK_W = RET_HEADS * RET_DK
RET_V_W = RET_HEADS * RET_DV
GQA_Q_W = GQA_HEADS * HEAD_DIM
GQA_KV_W = GQA_KV_HEADS * HEAD_DIM
NA_W = NA_HEADS * HEAD_DIM
OFF_QR, OFF_KR, OFF_VR, OFF_GR, OFF_QA, OFF_KA, OFF_VA, EVEN_IN_W = (
    int(v) for v in np.cumsum([0, RET_QK_W, RET_QK_W, RET_V_W, RET_V_W, GQA_Q_W, GQA_KV_W, GQA_KV_W]))
PROJ_ROWS = 512
FFN_ROWS = 1024
RET_ROWS = 2048
GQA_Q_ROWS = 512
GQA_CTX_SEQS = 2
DENSE_CTX_SEQS = 4
ADA_COLS = 2048
NEG = -1e30
LOG2E = 1.4426950408889634
VMEM_LIMIT = 56 * 1024 * 1024


def _nn(a, b):
    return jnp.dot(a, b, preferred_element_type=F32)


def _nt(a, b):
    return lax.dot_general(a, b, (((1,), (1,)), ((), ())), preferred_element_type=F32)


def _tn(a, b):
    return lax.dot_general(a, b, (((0,), (0,)), ((), ())), preferred_element_type=F32)


def _sigmoid(x):
    return 1.0 / (1.0 + jnp.exp(-x))


def _params(n_axes):
    return pltpu.CompilerParams(dimension_semantics=("arbitrary",) * n_axes, vmem_limit_bytes=VMEM_LIMIT)


def _modulated_norm(x, g, scale, shift):
    ms = jnp.mean(x * x, axis=-1, keepdims=True)
    return (x * lax.rsqrt(ms + EPS) * g) * (1.0 + scale) + shift


def _lane_lo(rows):
    return lax.broadcasted_iota(jnp.int32, (rows, LANES), 1) < HEAD_DIM


def _head_rms_norm(xb, gain, lo):
    sq = xb * xb
    s_lo = jnp.sum(jnp.where(lo, sq, 0.0), axis=-1, keepdims=True)
    s_hi = jnp.sum(jnp.where(lo, 0.0, sq), axis=-1, keepdims=True)
    r = jnp.where(lo, lax.rsqrt(s_lo * (1.0 / HEAD_DIM) + EPS), lax.rsqrt(s_hi * (1.0 / HEAD_DIM) + EPS))
    return xb * r * gain


def _rope(xb, cos, sin_signed, first16):
    partner = jnp.where(first16, pltpu.roll(xb, LANES - 16, 1), pltpu.roll(xb, 16, 1))
    return xb * cos + partner * sin_signed


def _softmax_apply(scores, values):
    m = functools.reduce(jnp.maximum, [jnp.max(s, axis=-1, keepdims=True) for s in scores])
    es = [jnp.exp2(s - m) for s in scores]
    l = functools.reduce(jnp.add, [jnp.sum(e, axis=-1, keepdims=True) for e in es])
    o = None
    for e, v in zip(es, values):
        t = _nt(e.astype(BF16), v[1]) if isinstance(v, tuple) else _nn(e.astype(BF16), v)
        o = t if o is None else o + t
    return o * (1.0 / l)


def _run_pipelined(units, scores_of, finish):
    pending = scores_of(units[0])
    for n, unit in enumerate(units):
        nxt = scores_of(units[n + 1]) if n + 1 < len(units) else None
        finish(unit, pending)
        pending = nxt


def _masked_half(kp, half):
    lo = _lane_lo(kp.shape[0])
    return jnp.where(lo if half == 0 else jnp.logical_not(lo), kp, jnp.zeros_like(kp))


def _ada_kernel(c_ref, w_ref, b_ref, o_ref):
    c = c_ref[...]
    a = (c * _sigmoid(c)).astype(BF16)
    o_ref[...] = _nn(a, w_ref[...].astype(BF16)) + b_ref[...]


def _ada_params(cond, ada_w, ada_b):
    depth = ada_w.shape[0]
    rows = cond.shape[0]
    tn = ADA_COLS
    return pl.pallas_call(
        _ada_kernel,
        out_shape=jax.ShapeDtypeStruct((depth, rows, 6 * D_MODEL), F32),
        grid=(depth, 6 * D_MODEL // tn),
        in_specs=[
            pl.BlockSpec((rows, D_MODEL), lambda l, j: (0, 0)),
            pl.BlockSpec((None, D_MODEL, tn), lambda l, j: (l, 0, j)),
            pl.BlockSpec((None, 1, tn), lambda l, j: (l, 0, j)),
        ],
        out_specs=pl.BlockSpec((None, rows, tn), lambda l, j: (l, 0, j)),
        compiler_params=_params(2),
        name="ada_params",
    )(cond, ada_w, ada_b.reshape(depth, 1, 6 * D_MODEL))


def _mod_spec(layer, which, bidx):
    return pl.BlockSpec((None, None, None, 1, D_MODEL), lambda i: (layer, bidx(i), which, 0, 0))


def _batch_index_fn(latent, rows_per_tile, seq_len):
    if not latent:
        return lambda i: 0
    return lambda i: 1 + (i * rows_per_tile) // seq_len


def _in_even_kernel(*refs, latent, tm):
    if latent:
        (x_ref, shift_ref, scale_ref, g_ref, w_ref, qg_ref, kg_ref, cos_ref, sin_ref,
         qr_ref, kr_ref, vr_ref, gr_ref, qa_ref, kd_ref, vd_ref) = refs
    else:
        (x_ref, shift_ref, scale_ref, g_ref, w_ref, qg_ref, kg_ref,
         qr_ref, kr_ref, vr_ref, gr_ref, qa_ref, kd_ref, vd_ref, ck_ref, cv_ref) = refs
    hb = _modulated_norm(x_ref[...], g_ref[...], scale_ref[...], shift_ref[...]).astype(BF16)
    lane = lax.broadcasted_iota(jnp.int32, (tm, LANES), 1)
    lo = lane < HEAD_DIM
    if latent:
        cos = cos_ref[...]
        sin = sin_ref[...]
        first16 = (lane % 32) < 16
        rope = lambda v: _rope(v, cos, sin, first16)
    else:
        rope = lambda v: v
    dk_scale = RET_DK ** -0.5
    q_scale = HEAD_DIM ** -0.5 * LOG2E

    r = _nn(hb, w_ref[:, OFF_QA:OFF_QA + GQA_Q_W])
    qg = qg_ref[...]
    for b in range(GQA_Q_W // LANES):
        blk = rope(_head_rms_norm(r[:, b * LANES:(b + 1) * LANES], qg, lo)) * q_scale
        qa_ref[:, b * LANES:(b + 1) * LANES] = blk.astype(BF16)
    assert GQA_KV_W == LANES
    r = _nn(hb, w_ref[:, OFF_KA:OFF_VA + GQA_KV_W])
    kn = _head_rms_norm(r[:, 0:LANES], kg_ref[...], lo)
    vn = r[:, LANES:2 * LANES]
    if not latent:
        seq = ck_ref.shape[4]
        for bb in range(tm // seq):
            for src, dst in ((kn, ck_ref), (vn, cv_ref)):
                t = src[bb * seq:(bb + 1) * seq, :].T
                for kv in range(GQA_KV_HEADS):
                    dst[bb, 0, kv] = t[kv * HEAD_DIM:(kv + 1) * HEAD_DIM]
    kn = rope(kn)
    for src, dst in ((kn, kd_ref), (vn, vd_ref)):
        sw = pltpu.roll(src, HEAD_DIM, 1)
        dst[:, 0:LANES] = jnp.where(lo, src, sw).astype(BF16)
        dst[:, LANES:2 * LANES] = jnp.where(lo, sw, src).astype(BF16)
    r = _nn(hb, w_ref[:, OFF_QR:OFF_QR + RET_QK_W])
    for b in range(RET_QK_W // LANES):
        qr_ref[:, b * LANES:(b + 1) * LANES] = rope(r[:, b * LANES:(b + 1) * LANES])
    r = _nn(hb, w_ref[:, OFF_KR:OFF_KR + RET_QK_W]) * dk_scale
    for b in range(RET_QK_W // LANES):
        kr_ref[:, b * LANES:(b + 1) * LANES] = rope(r[:, b * LANES:(b + 1) * LANES])
    for c in range(RET_V_W // PROJ_CHUNK):
        cols = slice(c * PROJ_CHUNK, (c + 1) * PROJ_CHUNK)
        r = _nn(hb, w_ref[:, OFF_GR + c * PROJ_CHUNK:OFF_GR + (c + 1) * PROJ_CHUNK])
        gr_ref[:, cols] = r * _sigmoid(r)
    for c in range(RET_V_W // PROJ_CHUNK):
        cols = slice(c * PROJ_CHUNK, (c + 1) * PROJ_CHUNK)
        vr_ref[:, cols] = _nn(hb, w_ref[:, OFF_VR + c * PROJ_CHUNK:OFF_VR + (c + 1) * PROJ_CHUNK]).astype(BF16)


def _in_proj_even(x2d, mods, layer, norm_g, w_bf, q_gain2, k_gain2, rope_tabs, *, n_batch, seq_len, latent):
    n = x2d.shape[0]
    tm = PROJ_ROWS
    assert w_bf.shape == (D_MODEL, EVEN_IN_W)
    bidx = _batch_index_fn(latent, tm, seq_len)
    row = lambda i: (i, 0)
    const = lambda i: (0, 0)
    in_specs = [
        pl.BlockSpec((tm, D_MODEL), row),
        _mod_spec(layer, 0, bidx),
        _mod_spec(layer, 1, bidx),
        pl.BlockSpec((1, D_MODEL), const),
        pl.BlockSpec(w_bf.shape, const),
        pl.BlockSpec((1, LANES), const),
        pl.BlockSpec((1, LANES), const),
    ]
    args = [x2d, mods, mods, norm_g, w_bf, q_gain2, k_gain2]
    if latent:
        tiles_per_seq = seq_len // tm
        in_specs += [pl.BlockSpec((tm, LANES), lambda i: (i % tiles_per_seq, 0))] * 2
        args += list(rope_tabs)
    out_shape = [
        jax.ShapeDtypeStruct((n, RET_QK_W), F32),
        jax.ShapeDtypeStruct((n, RET_QK_W), F32),
        jax.ShapeDtypeStruct((n, RET_V_W), BF16),
        jax.ShapeDtypeStruct((n, RET_V_W), F32),
        jax.ShapeDtypeStruct((n, GQA_Q_W), BF16),
        jax.ShapeDtypeStruct((n, 2 * GQA_KV_W), BF16),
        jax.ShapeDtypeStruct((n, 2 * GQA_KV_W), BF16),
    ]
    out_specs = [pl.BlockSpec((tm, s.shape[1]), row) for s in out_shape]
    if not latent:
        cache = jax.ShapeDtypeStruct((n_batch, 1, GQA_KV_HEADS, HEAD_DIM, seq_len), F32)
        out_shape += [cache, cache]
        out_specs += [pl.BlockSpec((tm // seq_len, 1, GQA_KV_HEADS, HEAD_DIM, seq_len),
                                   lambda i: (i, 0, 0, 0, 0))] * 2
    return pl.pallas_call(
        functools.partial(_in_even_kernel, latent=latent, tm=tm),
        out_shape=out_shape,
        grid=(n // tm,),
        in_specs=in_specs,
        out_specs=out_specs,
        compiler_params=_params(1),
        name="in_proj_even_latent" if latent else "in_proj_even_ctx",
    )(*args)


def _retention_kernel(*refs, n, nb, has_state, write_state):
    lg_ref, q_ref, k_ref, v_ref, gr_ref, gn_ref = refs[:6]
    refs = refs[6:]
    if has_state:
        s0f_ref, s0b_ref = refs[:2]
        refs = refs[2:]
    o_ref = refs[0]
    if write_state:
        sf_ref, sb_ref = refs[1:3]
    c = RET_CHUNK
    assert c == LANES and 2 * RET_DK == LANES
    nc = n // c
    p = pl.program_id(0)
    lgf = [lg_ref[0, 2 * p + hh] for hh in range(2)]
    lgb = [lg_ref[1, 2 * p + hh] for hh in range(2)]
    row = lax.broadcasted_iota(jnp.int32, (c, c), 0)
    col = lax.broadcasted_iota(jnp.int32, (c, c), 1)
    diff = (row - col).astype(F32)
    pos = row.astype(F32)
    lane_lo = col < RET_DK

    def both_scans(f, b):
        return (jnp.where(diff >= 0, jnp.exp(f * jnp.maximum(diff, 0.0)), 0.0)
                + jnp.where(diff <= 0, jnp.exp(b * jnp.maximum(-diff, 0.0)), 0.0))

    decay2 = jnp.concatenate([both_scans(lgf[0], lgb[0]), both_scans(lgf[1], lgb[1])], axis=1)
    lgf_lane = jnp.where(lane_lo, lgf[0], lgf[1])
    lgb_lane = jnp.where(lane_lo, lgb[0], lgb[1])
    qd_f = jnp.exp(lgf_lane * (pos + 1.0))
    kd_f = jnp.exp(lgf_lane * (c - 1.0 - pos))
    qd_b = jnp.exp(lgb_lane * (c - pos))
    kd_b = jnp.exp(lgb_lane * pos)
    srow = lax.broadcasted_iota(jnp.int32, (c, 2 * RET_DV), 0)
    scol = lax.broadcasted_iota(jnp.int32, (c, 2 * RET_DV), 1)
    row_a = srow < RET_DK
    col_a = scol < RET_DV
    own = row_a == col_a
    cd_f = jnp.exp(jnp.where(row_a, lgf[0], lgf[1]) * float(c))
    cd_b = jnp.exp(jnp.where(row_a, lgb[0], lgb[1]) * float(c))
    zeros_half = jnp.zeros((RET_DK, RET_DV), F32)
    gn = gn_ref[...]

    def place(s0_ref, bi):
        top = jnp.concatenate([s0_ref[bi, 0, 0], zeros_half], axis=1)
        bot = jnp.concatenate([zeros_half, s0_ref[bi, 0, 1]], axis=1)
        return jnp.concatenate([top, bot], axis=0)

    seqs = range(nb)
    rows = {bi: [slice(bi * n + i * c, bi * n + (i + 1) * c) for i in range(nc)] for bi in seqs}
    kv_f = {(bi, i): _tn((k_ref[rows[bi][i], :] * kd_f).astype(BF16), v_ref[rows[bi][i], :])
            for bi in seqs for i in range(nc)}
    kv_b = {(bi, i): _tn((k_ref[rows[bi][i], :] * kd_b).astype(BF16), v_ref[rows[bi][i], :])
            for bi in seqs for i in range(nc)}
    before_f, before_b = {}, {}
    for bi in seqs:
        if has_state:
            s_f = place(s0f_ref, bi)
            s_b = place(s0b_ref, bi)
        else:
            s_f = jnp.zeros((c, 2 * RET_DV), F32)
            s_b = jnp.zeros((c, 2 * RET_DV), F32)
        for i in range(nc):
            before_f[(bi, i)] = s_f
            s_f = s_f * cd_f + jnp.where(own, kv_f[(bi, i)], 0.0)
        for i in reversed(range(nc)):
            before_b[(bi, i)] = s_b
            s_b = s_b * cd_b + jnp.where(own, kv_b[(bi, i)], 0.0)
        if write_state:
            for hh in range(2):
                blk = (slice(hh * RET_DK, (hh + 1) * RET_DK), slice(hh * RET_DV, (hh + 1) * RET_DV))
                sf_ref[bi, 0, hh] = s_f[blk]
                sb_ref[bi, 0, hh] = s_b[blk]
    for bi in seqs:
        for i in range(nc):
            r = rows[bi][i]
            qc = q_ref[r, :]
            kc = k_ref[r, :]
            vc = v_ref[r, :]
            k_cat = jnp.concatenate([jnp.where(lane_lo, kc, 0.0), jnp.where(lane_lo, 0.0, kc)], axis=0)
            v_blk = jnp.concatenate([jnp.where(col_a, vc, jnp.zeros_like(vc)),
                                     jnp.where(col_a, jnp.zeros_like(vc), vc)], axis=0)
            scores = _nt(qc.astype(BF16), k_cat.astype(BF16)) * decay2
            q_cat = jnp.concatenate([(qc * qd_f).astype(BF16), (qc * qd_b).astype(BF16)], axis=1)
            s_cat = jnp.concatenate([before_f[(bi, i)], before_b[(bi, i)]], axis=0).astype(BF16)
            o = _nn(scores.astype(BF16), v_blk) + _nn(q_cat, s_cat)
            for hh in range(2):
                vcols = slice(hh * RET_DV, (hh + 1) * RET_DV)
                oh = o[:, vcols]
                mu = jnp.mean(oh, axis=-1, keepdims=True)
                d = oh - mu
                var = jnp.mean(d * d, axis=-1, keepdims=True)
                y = d * lax.rsqrt(var + GN_EPS) * gn[:, vcols] * gr_ref[r, vcols]
                o_ref[r, vcols] = y.astype(BF16)


def _retention(log_g, qr, kr, vr, gr, gn, state_f, state_b, *, n_batch, seq_len, write_state):
    n = qr.shape[0]
    pairs = RET_HEADS // 2
    has_state = state_f is not None
    nb = max(1, RET_ROWS // seq_len)
    rows = nb * seq_len
    tok = lambda p, g: (g, p)
    in_specs = [
        pl.BlockSpec(memory_space=pltpu.SMEM),
        pl.BlockSpec((rows, LANES), tok),
        pl.BlockSpec((rows, LANES), tok),
        pl.BlockSpec((rows, 2 * RET_DV), tok),
        pl.BlockSpec((rows, 2 * RET_DV), tok),
        pl.BlockSpec((1, 2 * RET_DV), lambda p, g: (0, p)),
    ]
    args = [log_g, qr, kr, vr, gr, gn]
    state_spec = pl.BlockSpec((nb, 1, 2, RET_DK, RET_DV), lambda p, g: (g, 0, p, 0, 0))
    if has_state:
        in_specs += [state_spec, state_spec]
        args += [state_f, state_b]
    out_shape = [jax.ShapeDtypeStruct((n, RET_HEADS * RET_DV), BF16)]
    out_specs = [pl.BlockSpec((rows, 2 * RET_DV), tok)]
    if write_state:
        st = jax.ShapeDtypeStruct((n_batch, 1, RET_HEADS, RET_DK, RET_DV), F32)
        out_shape += [st, st]
        out_specs += [state_spec, state_spec]
    return pl.pallas_call(
        functools.partial(_retention_kernel, n=seq_len, nb=nb, has_state=has_state, write_state=write_state),
        out_shape=out_shape,
        grid=(pairs, n_batch // nb),
        in_specs=in_specs,
        out_specs=out_specs,
        compiler_params=_params(2),
        name="retention_latent" if has_state else "retention_ctx",
    )(*args)


def _gqa_kernel(*refs, n_src, tq, n_seq):
    q_ref = refs[0]
    k_refs = refs[1:1 + 2 * n_src:2]
    v_refs = refs[2:2 + 2 * n_src:2]
    o_ref = refs[1 + 2 * n_src]
    units = [(b, g, half) for b in range(n_seq) for g in range(GQA_KV_HEADS) for half in range(2)]
    outs = {}

    def kv_rows(ref, b):
        n_keys = ref.shape[0] // n_seq
        return slice(b * n_keys, (b + 1) * n_keys)

    def scores_of(unit):
        b, g, half = unit
        rows, base = slice(b * tq, (b + 1) * tq), g * 2 * LANES
        q = jnp.concatenate([q_ref[rows, base:base + LANES], q_ref[rows, base + LANES:base + 2 * LANES]], axis=0)
        return [_nt(q, _masked_half(k_ref[kv_rows(k_ref, b), g * LANES:(g + 1) * LANES], half)) for k_ref in k_refs]

    def finish(unit, scores):
        b, g, half = unit
        outs[half] = _softmax_apply(scores, [v_ref[kv_rows(v_ref, b), g * LANES:(g + 1) * LANES] for v_ref in v_refs])
        if half == 1:
            rows, base = slice(b * tq, (b + 1) * tq), g * 2 * LANES
            o = jnp.where(_lane_lo(2 * tq), outs[0], outs[1]).astype(BF16)
            o_ref[rows, base:base + LANES] = o[:tq]
            o_ref[rows, base + LANES:base + 2 * LANES] = o[tq:]

    _run_pipelined(units, scores_of, finish)


def _gqa_attention(qa, kd, vd, ctx_kd, ctx_vd, *, n_batch, seq_len, tq):
    n = qa.shape[0]
    tiles = seq_len // tq
    n_src = 1 if ctx_kd is None else 2
    n_seq = GQA_CTX_SEQS if (tiles == 1 and n_src == 1) else 1
    qmap = lambda b, t: (b * tiles + t, 0)
    kmap = lambda b, t: (b, 0)
    in_specs = [pl.BlockSpec((n_seq * tq, GQA_Q_W), qmap),
                pl.BlockSpec((n_seq * seq_len, 2 * LANES), kmap),
                pl.BlockSpec((n_seq * seq_len, 2 * LANES), kmap)]
    args = [qa, kd, vd]
    if n_src == 2:
        past = ctx_kd.shape[1]
        cmap = lambda b, t: (b, 0, 0)
        in_specs += [pl.BlockSpec((None, past, 2 * LANES), cmap)] * 2
        args += [ctx_kd, ctx_vd]
    return pl.pallas_call(
        functools.partial(_gqa_kernel, n_src=n_src, tq=tq, n_seq=n_seq),
        out_shape=jax.ShapeDtypeStruct((n, GQA_Q_W), BF16),
        grid=(n_batch // n_seq, tiles),
        in_specs=in_specs,
        out_specs=pl.BlockSpec((n_seq * tq, GQA_Q_W), qmap),
        compiler_params=_params(2),
        name="gqa_latent" if n_src == 2 else "gqa_ctx",
    )(*args)


def _zero_rows(arr, rows):
    pieces, cur = [], 0
    sub = lax.broadcasted_iota(jnp.int32, (8, arr.shape[1]), 0)
    for r in sorted(rows):
        g0 = (r // 8) * 8
        if g0 > cur:
            pieces.append(arr[cur:g0])
        pieces.append(jnp.where(sub == r - g0, 0.0, arr[g0:g0 + 8]))
        cur = g0 + 8
    if cur < arr.shape[0]:
        pieces.append(arr[cur:])
    return jnp.concatenate(pieces, axis=0)


def _mix_ffn_kernel(*refs, tm, seq_len, final, n_mix):
    x_ref, gate_mix_ref, g_ref, scale_ref, shift_ref, gate_ref = refs[:6]
    m_refs = refs[6:6 + n_mix]
    w_refs = refs[6 + n_mix:6 + 2 * n_mix]
    wup_ref, cw_ref, cb_ref, wd_ref = refs[6 + 2 * n_mix:10 + 2 * n_mix]
    refs = refs[10 + 2 * n_mix:]
    if final:
        gfin_ref = refs[0]
        refs = refs[1:]
    o_ref, act_ref, hb_ref = refs
    halves = [slice(rb * (tm // 2), (rb + 1) * (tm // 2)) for rb in range(2)]
    for rows in halves:
        acc = None
        for m_ref, w_ref in zip(m_refs, w_refs):
            t = _nn(m_ref[rows, :], w_ref[...])
            acc = t if acc is None else acc + t
        y = x_ref[rows, :] + gate_mix_ref[...] * acc
        o_ref[rows, :] = y
        hb_ref[rows, :] = _modulated_norm(y, g_ref[...], scale_ref[...], shift_ref[...]).astype(BF16)
    seq_starts = list(range(0, tm, seq_len))
    seq_ends = [s + seq_len - 1 for s in seq_starts]
    for j in range(N_FF_CHUNKS):
        parts = []
        for off in (0, D_FF):
            cols = slice(off + j * FF_CHUNK, off + (j + 1) * FF_CHUNK)
            u = _nn(hb_ref[...], wup_ref[:, cols])
            cw = cw_ref[:, cols]
            prev = _zero_rows(pltpu.roll(u, 1, 0), seq_starts)
            nxt = _zero_rows(pltpu.roll(u, tm - 1, 0), seq_ends)
            parts.append(prev * cw[0:1] + u * cw[1:2] + nxt * cw[2:3] + cb_ref[:, cols])
        a, g = parts
        act_ref[:, j * FF_CHUNK:(j + 1) * FF_CHUNK] = (a * _sigmoid(a) * g).astype(BF16)
    for rows in halves:
        y = o_ref[rows, :] + gate_ref[...] * _nn(act_ref[rows, :], wd_ref[...])
        if final:
            ms = jnp.mean(y * y, axis=-1, keepdims=True)
            y = y * lax.rsqrt(ms + EPS) * gfin_ref[...]
        o_ref[rows, :] = y


def _mix_ffn(x2d, mods, layer, mixes, weights, ffn_norm_g, wup_c, cw_c, cb_c, wd_c, final_g, *, seq_len, latent):
    n = x2d.shape[0]
    tm = FFN_ROWS
    bidx = _batch_index_fn(latent, tm, seq_len)
    row = lambda i: (i, 0)
    const2 = lambda i: (0, 0)
    once = pl.Buffered(1)
    resident = lambda a: pl.BlockSpec((None,) + a.shape[1:], lambda i: (layer,) + (0,) * (a.ndim - 1),
                                      pipeline_mode=once)
    in_specs = [pl.BlockSpec((tm, D_MODEL), row), _mod_spec(layer, 2, bidx), pl.BlockSpec((1, D_MODEL), const2),
                _mod_spec(layer, 4, bidx), _mod_spec(layer, 3, bidx), _mod_spec(layer, 5, bidx)]
    in_specs += [pl.BlockSpec((tm, m.shape[1]), row) for m in mixes]
    in_specs += [pl.BlockSpec(w.shape, const2, pipeline_mode=once) for w in weights]
    in_specs += [resident(wup_c), resident(cw_c), resident(cb_c), resident(wd_c)]
    args = [x2d, mods, ffn_norm_g, mods, mods, mods, *mixes, *weights, wup_c, cw_c, cb_c, wd_c]
    final = final_g is not None
    if final:
        in_specs.append(pl.BlockSpec((1, D_MODEL), const2))
        args.append(final_g)
    return pl.pallas_call(
        functools.partial(_mix_ffn_kernel, tm=tm, seq_len=seq_len, final=final, n_mix=len(mixes)),
        out_shape=jax.ShapeDtypeStruct((n, D_MODEL), F32),
        grid=(n // tm,),
        in_specs=in_specs,
        out_specs=pl.BlockSpec((tm, D_MODEL), row),
        scratch_shapes=[pltpu.VMEM((tm, D_FF), BF16), pltpu.VMEM((tm, D_MODEL), BF16)],
        compiler_params=_params(1),
        name="mix_ffn_latent" if latent else "mix_ffn_ctx",
    )(*args)


def _in_odd_kernel(*refs, write_cache, tm):
    x_ref, shift_ref, scale_ref, g_ref, w_ref, q_ref, k_ref, v_ref = refs[:8]
    hb = _modulated_norm(x_ref[...], g_ref[...], scale_ref[...], shift_ref[...]).astype(BF16)
    q_scale = HEAD_DIM ** -0.5 * LOG2E
    chunks = NA_W // PROJ_CHUNK
    for which, dst in ((1, k_ref), (2, v_ref)):
        for c in range(chunks):
            r = _nn(hb, w_ref[:, which * NA_W + c * PROJ_CHUNK:which * NA_W + (c + 1) * PROJ_CHUNK])
            dst[:, c * PROJ_CHUNK:(c + 1) * PROJ_CHUNK] = r.astype(BF16)
            if write_cache:
                cache_ref = refs[8 + which - 1]
                seq = cache_ref.shape[4]
                for bb in range(tm // seq):
                    for blk in range(PROJ_CHUNK // LANES):
                        t = r[bb * seq:(bb + 1) * seq, blk * LANES:(blk + 1) * LANES].T
                        for hh in range(2):
                            head = c * (PROJ_CHUNK // HEAD_DIM) + 2 * blk + hh
                            cache_ref[bb, 0, head] = t[hh * HEAD_DIM:(hh + 1) * HEAD_DIM]
    for c in range(chunks):
        cols = slice(c * PROJ_CHUNK, (c + 1) * PROJ_CHUNK)
        q_ref[:, cols] = (_nn(hb, w_ref[:, cols]) * q_scale).astype(BF16)


def _in_proj_odd(x2d, mods, layer, norm_g, w_bf, *, n_batch, seq_len, latent):
    n = x2d.shape[0]
    tm = PROJ_ROWS
    width = NA_W
    bidx = _batch_index_fn(latent, tm, seq_len)
    row = lambda i: (i, 0)
    const = lambda i: (0, 0)
    in_specs = [pl.BlockSpec((tm, D_MODEL), row), _mod_spec(layer, 0, bidx), _mod_spec(layer, 1, bidx),
                pl.BlockSpec((1, D_MODEL), const), pl.BlockSpec(w_bf.shape, const)]
    out_shape = [jax.ShapeDtypeStruct((n, width), BF16)] * 3
    out_specs = [pl.BlockSpec((tm, width), row)] * 3
    write_cache = not latent
    if write_cache:
        cache = jax.ShapeDtypeStruct((n_batch, 1, NA_HEADS, HEAD_DIM, seq_len), F32)
        out_shape += [cache, cache]
        out_specs += [pl.BlockSpec((tm // seq_len, 1, NA_HEADS, HEAD_DIM, seq_len),
                                   lambda i: (i, 0, 0, 0, 0))] * 2
    return pl.pallas_call(
        functools.partial(_in_odd_kernel, write_cache=write_cache, tm=tm),
        out_shape=out_shape,
        grid=(n // tm,),
        in_specs=in_specs,
        out_specs=out_specs,
        compiler_params=_params(1),
        name="in_proj_odd_latent" if latent else "in_proj_odd_ctx",
    )(x2d, mods, mods, norm_g, w_bf)


def _dense_pairs_kernel(q_ref, k_ref, v_ref, o_ref, *, seq_len):
    n_seq = q_ref.shape[0] // seq_len
    units = [(b, p, half) for b in range(n_seq) for p in range(NA_HEADS // 2) for half in range(2)]
    outs = {}

    def scores_of(unit):
        b, p, half = unit
        rows, cols = slice(b * seq_len, (b + 1) * seq_len), slice(p * LANES, (p + 1) * LANES)
        return [_nt(q_ref[rows, cols], _masked_half(k_ref[rows, cols], half))]

    def finish(unit, scores):
        b, p, half = unit
        rows, cols = slice(b * seq_len, (b + 1) * seq_len), slice(p * LANES, (p + 1) * LANES)
        outs[half] = _softmax_apply(scores, [v_ref[rows, cols]])
        if half == 1:
            o_ref[rows, cols] = jnp.where(_lane_lo(seq_len), outs[0], outs[1]).astype(BF16)

    _run_pipelined(units, scores_of, finish)


def _dense_attention_ctx(q, k, v, *, n_batch, seq_len):
    width = NA_W
    n_seq = DENSE_CTX_SEQS
    spec = pl.BlockSpec((n_seq * seq_len, width), lambda b: (b, 0))
    return pl.pallas_call(
        functools.partial(_dense_pairs_kernel, seq_len=seq_len),
        out_shape=jax.ShapeDtypeStruct(q.shape, BF16),
        grid=(n_batch // n_seq,),
        in_specs=[spec, spec, spec],
        out_specs=spec,
        compiler_params=_params(1),
        name="dense_attention_ctx",
    )(q, k, v)


NA_Q_ROWS = 8
NA_KEY_ROWS = 12


def _na_window_start(r, n_rows):
    return min(max(r - NA_KH // 2, 0), n_rows - NA_KH)


def _na_bias_tile(bias_ref, half, tile, n_rows):
    r0 = tile * NA_Q_ROWS
    kr0 = min(max(r0 - NA_KH // 2, 0), n_rows - NA_KEY_ROWS)
    lo = _lane_lo(GRID_W)
    neg_block = jnp.full((GRID_W, LANES), NEG, F32)
    left_off = jnp.where(lo, NEG, 0.0)
    right_off = jnp.where(lo, 0.0, NEG)
    rows = []
    for rq in range(NA_Q_ROWS):
        r = r0 + rq
        rs = _na_window_start(r, n_rows)
        blocks = []
        for kk in range(NA_KEY_ROWS // 2):
            ka = kr0 + 2 * kk
            va = rs <= ka < rs + NA_KH
            vb = rs <= ka + 1 < rs + NA_KH
            if not (va or vb):
                blocks.append(neg_block)
                continue
            blk = bias_ref[half, ka - r + NA_KH]
            if not va:
                blk = blk + left_off
            if not vb:
                blk = blk + right_off
            blocks.append(blk)
        rows.append(jnp.concatenate(blocks, axis=1))
    return jnp.concatenate(rows, axis=0), kr0


def _na_kernel(q_ref, k_ref, v_ref, ck_ref, cv_ref, bias_ref, o_ref, *, n_rows):
    tq = NA_Q_ROWS * GRID_W
    span = NA_KEY_ROWS * GRID_W
    units = [(tile, half) for tile in range(n_rows // NA_Q_ROWS) for half in range(2)]
    outs = {}

    def window(tile):
        kr0 = min(max(tile * NA_Q_ROWS - NA_KH // 2, 0), n_rows - NA_KEY_ROWS)
        return slice(kr0 * GRID_W, kr0 * GRID_W + span)

    def scores_of(unit):
        tile, half = unit
        q = q_ref[tile * tq:(tile + 1) * tq, :]
        bias, _ = _na_bias_tile(bias_ref, half, tile, n_rows)
        ck_t = jnp.concatenate([ck_ref[0], ck_ref[1]], axis=0)
        own = (lax.broadcasted_iota(jnp.int32, ck_t.shape, 0) < HEAD_DIM) == (half == 0)
        return [_nt(q, _masked_half(k_ref[window(tile), :], half)) + bias,
                _nn(q, jnp.where(own, ck_t, 0.0).astype(BF16))]

    def finish(unit, scores):
        tile, half = unit
        cv_t = jnp.concatenate([cv_ref[0], cv_ref[1]], axis=0).astype(BF16)
        outs[half] = _softmax_apply(scores, [v_ref[window(tile), :], ("t", cv_t)])
        if half == 1:
            o_ref[tile * tq:(tile + 1) * tq, :] = jnp.where(_lane_lo(tq), outs[0], outs[1]).astype(BF16)

    _run_pipelined(units, scores_of, finish)


def _na_bias_table(rpb):
    cidx = np.arange(GRID_W)
    cs = np.clip(cidx - NA_KW // 2, 0, GRID_W - NA_KW)
    kc = np.arange(GRID_W)
    inside = (kc[None, :] >= cs[:, None]) & (kc[None, :] < cs[:, None] + NA_KW)
    rel = kc[None, :] - cidx[:, None] + NA_KW - 1
    onehot = (rel[None] == np.arange(2 * NA_KW - 1)[:, None, None]) & inside[None]
    m = jnp.einsum("hdj,jck->hdck", rpb * LOG2E, jnp.asarray(onehot, F32), precision=lax.Precision.HIGHEST)
    m = jnp.where(jnp.asarray(inside)[None, None], m, NEG)
    neg = jnp.full((rpb.shape[0], 1, GRID_W, GRID_W), NEG, F32)
    left = jnp.concatenate([neg, m], axis=1)
    right = jnp.concatenate([m, neg], axis=1)
    return jnp.concatenate([left, right], axis=-1)


def _na_attention(q, k, v, ctx_k, ctx_v, bias_tab, *, n_batch, seq_len):
    pairs = NA_HEADS // 2
    tok = lambda p, b: (b, p)
    ctx = lambda p, b: (b, p, 0, 0)
    past = ctx_k.shape[-1]
    return pl.pallas_call(
        functools.partial(_na_kernel, n_rows=seq_len // GRID_W),
        out_shape=jax.ShapeDtypeStruct(q.shape, BF16),
        grid=(pairs, n_batch),
        in_specs=[pl.BlockSpec((seq_len, LANES), tok)] * 3
        + [pl.BlockSpec((None, 2, HEAD_DIM, past), ctx)] * 2
        + [pl.BlockSpec((2, 2 * NA_KH, GRID_W, LANES), lambda p, b: (p, 0, 0, 0))],
        out_specs=pl.BlockSpec((seq_len, LANES), tok),
        compiler_params=_params(2),
        name="neighbourhood_attention",
    )(q, k, v, ctx_k, ctx_v, bias_tab)


def _rope_tables(n):
    t = np.arange(n)
    row = (t // GRID_W).astype(np.float64)
    col = (t % GRID_W).astype(np.float64)
    half = HEAD_DIM // 2
    inv = ROPE_BASE ** (-np.arange(0, half, 2, dtype=np.float64) / half)
    ang_r = row[:, None] * inv
    ang_c = col[:, None] * inv
    cos_h = np.concatenate([np.cos(ang_r)] * 2 + [np.cos(ang_c)] * 2, axis=-1)
    sin_h = np.concatenate([-np.sin(ang_r), np.sin(ang_r), -np.sin(ang_c), np.sin(ang_c)], axis=-1)
    return (jnp.asarray(np.concatenate([cos_h, cos_h], axis=-1), F32),
            jnp.asarray(np.concatenate([sin_h, sin_h], axis=-1), F32))


def _ffn_weights(w_up, conv_w, conv_b, w_down):
    return w_up.astype(BF16), conv_w, conv_b[:, None, :], w_down.astype(BF16)


def _token_major_dup(cache):
    b, kv, t, d = cache.shape
    c = jnp.transpose(cache, (0, 2, 1, 3))[:, :, :, None, :]
    return jnp.broadcast_to(c, (b, t, kv, 2, d)).reshape(b, t, kv * 2 * d).astype(BF16)


def _head_transposed(cache):
    return jnp.swapaxes(cache, -1, -2)


def kernel(x_prompt, x_sample, state_ret_fwd, state_ret_bwd, cache_gqa_k, cache_gqa_v, cache_na_k, cache_na_v,
           c, c_ctx, ada_w, ada_b, norm_mix, norm_ffn, norm_final, even_w_in, even_w_out, ret_decay_fwd,
           ret_decay_bwd, ret_gn, gqa_q_norm, gqa_k_norm, odd_w_in, odd_w_out, na_rpb, ffn_w_up, ffn_conv_w,
           ffn_conv_b, ffn_w_down):
    nb_c, len_c, _ = x_prompt.shape
    nb_s, len_s, _ = x_sample.shape
    depth = ada_w.shape[0]
    streams = {
        False: dict(n_batch=nb_c, seq_len=len_c),
        True: dict(n_batch=nb_s, seq_len=len_s),
    }
    xs = {False: x_prompt.reshape(nb_c * len_c, D_MODEL), True: x_sample.reshape(nb_s * len_s, D_MODEL)}

    rows = 8 * (-(-(1 + nb_s) // 8))
    cond = jnp.zeros((rows, D_MODEL), F32).at[0].set(c_ctx).at[1:1 + nb_s].set(c)
    mods = _ada_params(cond, ada_w, ada_b).reshape(depth, rows, 6, 1, D_MODEL)
    rope_tabs = _rope_tables(len_s)
    ffn_w = _ffn_weights(ffn_w_up, ffn_conv_w, ffn_conv_b, ffn_w_down)
    outs = {}
    mixed = {}

    for l in range(depth):
        g_mix = norm_mix[l][None, :]
        g_ffn = norm_ffn[l][None, :]
        if l % 2 == 0:
            e = l // 2
            w_in = even_w_in[e].astype(BF16)
            w_out = even_w_out[e].astype(BF16)
            w_out_parts = [w_out[:RET_V_W], w_out[RET_V_W:]]
            log_g = jnp.stack([jax.nn.log_sigmoid(ret_decay_fwd[e].astype(F32)),
                               jax.nn.log_sigmoid(ret_decay_bwd[e].astype(F32))])
            gn = ret_gn[e][None, :]
            qg2 = jnp.tile(gqa_q_norm[e], 2)[None, :]
            kg2 = jnp.tile(gqa_k_norm[e], 2)[None, :]
            for latent in (False, True):
                st = streams[latent]
                res = _in_proj_even(xs[latent], mods, l, g_mix, w_in, qg2, kg2, rope_tabs, latent=latent, **st)
                qr, kr, vr, gr, qa, kd, vd = res[:7]
                if latent:
                    ret = _retention(log_g, qr, kr, vr, gr, gn, state_ret_fwd[:, e:e + 1],
                                     state_ret_bwd[:, e:e + 1], write_state=False, **st)[0]
                    att = _gqa_attention(qa, kd, vd, _token_major_dup(cache_gqa_k[:, e]),
                                         _token_major_dup(cache_gqa_v[:, e]), tq=GQA_Q_ROWS, **st)
                else:
                    outs.setdefault("gk", []).append(res[7])
                    outs.setdefault("gv", []).append(res[8])
                    ret, s_f, s_b = _retention(log_g, qr, kr, vr, gr, gn, None, None, write_state=True, **st)
                    outs.setdefault("sf", []).append(s_f)
                    outs.setdefault("sb", []).append(s_b)
                    att = _gqa_attention(qa, kd, vd, None, None, tq=st["seq_len"], **st)
                mixed[latent] = ([ret, att], w_out_parts)
        else:
            o = l // 2
            w_in = odd_w_in[o].astype(BF16)
            w_out = odd_w_out[o].astype(BF16)
            for latent in (False, True):
                st = streams[latent]
                res = _in_proj_odd(xs[latent], mods, l, g_mix, w_in, latent=latent, **st)
                q, k, v = res[:3]
                if latent:
                    att = _na_attention(q, k, v, _head_transposed(cache_na_k[:, o]), _head_transposed(cache_na_v[:, o]),
                                        _na_bias_table(na_rpb[o]), **st)
                else:
                    outs.setdefault("nk", []).append(res[3])
                    outs.setdefault("nv", []).append(res[4])
                    att = _dense_attention_ctx(q, k, v, **st)
                mixed[latent] = ([att], [w_out])
        final_g = norm_final[None, :] if l == depth - 1 else None
        for latent in (False, True):
            xs[latent] = _mix_ffn(xs[latent], mods, l, *mixed[latent], g_ffn, *ffn_w, final_g,
                                   seq_len=streams[latent]["seq_len"], latent=latent)

    tr = lambda a: jnp.swapaxes(a, -1, -2)
    cat = lambda name: outs[name][0] if len(outs[name]) == 1 else jnp.concatenate(outs[name], axis=1)
    return (xs[False].reshape(nb_c, len_c, D_MODEL), xs[True].reshape(nb_s, len_s, D_MODEL),
            cat("sf"), cat("sb"), tr(cat("gk")), tr(cat("gv")), tr(cat("nk")), tr(cat("nv")))
```

```python
import functools

import numpy as np
import jax
import jax.numpy as jnp
from jax import lax
from jax.experimental import pallas as pl
from jax.experimental.pallas import tpu as pltpu

F32 = jnp.float32
BF16 = jnp.bfloat16

D_MODEL = 1024
GRID_W = 64
HEAD_DIM = 64
ROPE_BASE = 10000.0
EPS = 1e-6
GN_EPS = 1e-5
RET_HEADS = 8
RET_DK = 64
RET_DV = 128
RET_CHUNK = 128
GQA_HEADS = 8
GQA_KV_HEADS = 2
NA_HEADS = 16
NA_KH = 8
NA_KW = 16
D_FF = 2816
LANES = 128
MXU_DIM = 256
FF_CHUNK = MXU_DIM
N_FF_CHUNKS = D_FF // FF_CHUNK
PROJ_CHUNK = 2 * MXU_DIM
RET_QK_W = RET_HEADS * RET_DK
RET_V_W = RET_HEADS * RET_DV
GQA_Q_W = GQA_HEADS * HEAD_DIM
GQA_KV_W = GQA_KV_HEADS * HEAD_DIM
NA_W = NA_HEADS * HEAD_DIM
OFF_QR, OFF_KR, OFF_VR, OFF_GR, OFF_QA, OFF_KA, OFF_VA, EVEN_IN_W = (
    int(v) for v in np.cumsum([0, RET_QK_W, RET_QK_W, RET_V_W, RET_V_W, GQA_Q_W, GQA_KV_W, GQA_KV_W]))
PROJ_ROWS = 512
FFN_ROWS = 1024
RET_ROWS = 2048
GQA_Q_ROWS = 512
GQA_CTX_SEQS = 2
DENSE_CTX_SEQS = 4
ADA_COLS = 2048
NEG = -1e30
LOG2E = 1.4426950408889634
VMEM_LIMIT = 56 * 1024 * 1024


def _nn(a, b):
    return jnp.dot(a, b, preferred_element_type=F32)


def _nt(a, b):
    return lax.dot_general(a, b, (((1,), (1,)), ((), ())), preferred_element_type=F32)


def _tn(a, b):
    return lax.dot_general(a, b, (((0,), (0,)), ((), ())), preferred_element_type=F32)


def _sigmoid(x):
    return 1.0 / (1.0 + jnp.exp(-x))


def _params(n_axes):
    return pltpu.CompilerParams(dimension_semantics=("arbitrary",) * n_axes, vmem_limit_bytes=VMEM_LIMIT)


def _modulated_norm(x, g, scale, shift):
    ms = jnp.mean(x * x, axis=-1, keepdims=True)
    return (x * lax.rsqrt(ms + EPS) * g) * (1.0 + scale) + shift


def _lane_lo(rows):
    return lax.broadcasted_iota(jnp.int32, (rows, LANES), 1) < HEAD_DIM


def _head_rms_norm(xb, gain, lo):
    sq = xb * xb
    s_lo = jnp.sum(jnp.where(lo, sq, 0.0), axis=-1, keepdims=True)
    s_hi = jnp.sum(jnp.where(lo, 0.0, sq), axis=-1, keepdims=True)
    r = jnp.where(lo, lax.rsqrt(s_lo * (1.0 / HEAD_DIM) + EPS), lax.rsqrt(s_hi * (1.0 / HEAD_DIM) + EPS))
    return xb * r * gain


def _rope(xb, cos, sin_signed, first16):
    partner = jnp.where(first16, pltpu.roll(xb, LANES - 16, 1), pltpu.roll(xb, 16, 1))
    return xb * cos + partner * sin_signed


def _softmax_apply(scores, values):
    m = functools.reduce(jnp.maximum, [jnp.max(s, axis=-1, keepdims=True) for s in scores])
    es = [jnp.exp2(s - m) for s in scores]
    l = functools.reduce(jnp.add, [jnp.sum(e, axis=-1, keepdims=True) for e in es])
    o = None
    for e, v in zip(es, values):
        t = _nt(e.astype(BF16), v[1]) if isinstance(v, tuple) else _nn(e.astype(BF16), v)
        o = t if o is None else o + t
    return o * (1.0 / l)


def _run_pipelined(units, scores_of, finish):
    pending = scores_of(units[0])
    for n, unit in enumerate(units):
        nxt = scores_of(units[n + 1]) if n + 1 < len(units) else None
        finish(unit, pending)
        pending = nxt


def _masked_half(kp, half):
    lo = _lane_lo(kp.shape[0])
    return jnp.where(lo if half == 0 else jnp.logical_not(lo), kp, jnp.zeros_like(kp))


def _ada_kernel(c_ref, w_ref, b_ref, o_ref):
    c = c_ref[...]
    a = (c * _sigmoid(c)).astype(BF16)
    o_ref[...] = _nn(a, w_ref[...].astype(BF16)) + b_ref[...]


def _ada_params(cond, ada_w, ada_b):
    depth = ada_w.shape[0]
    rows = cond.shape[0]
    tn = ADA_COLS
    return pl.pallas_call(
        _ada_kernel,
        out_shape=jax.ShapeDtypeStruct((depth, rows, 6 * D_MODEL), F32),
        grid=(depth, 6 * D_MODEL // tn),
        in_specs=[
            pl.BlockSpec((rows, D_MODEL), lambda l, j: (0, 0)),
            pl.BlockSpec((None, D_MODEL, tn), lambda l, j: (l, 0, j)),
            pl.BlockSpec((None, 1, tn), lambda l, j: (l, 0, j)),
        ],
        out_specs=pl.BlockSpec((None, rows, tn), lambda l, j: (l, 0, j)),
        compiler_params=_params(2),
        name="ada_params",
    )(cond, ada_w, ada_b.reshape(depth, 1, 6 * D_MODEL))


def _mod_spec(layer, which, bidx):
    return pl.BlockSpec((None, None, None, 1, D_MODEL), lambda i: (layer, bidx(i), which, 0, 0))


def _batch_index_fn(latent, rows_per_tile, seq_len):
    if not latent:
        return lambda i: 0
    return lambda i: 1 + (i * rows_per_tile) // seq_len


def _in_even_kernel(*refs, latent, tm):
    if latent:
        (x_ref, shift_ref, scale_ref, g_ref, w_ref, qg_ref, kg_ref, cos_ref, sin_ref,
         qr_ref, kr_ref, vr_ref, gr_ref, qa_ref, kd_ref, vd_ref) = refs
    else:
        (x_ref, shift_ref, scale_ref, g_ref, w_ref, qg_ref, kg_ref,
         qr_ref, kr_ref, vr_ref, gr_ref, qa_ref, kd_ref, vd_ref, ck_ref, cv_ref) = refs
    hb = _modulated_norm(x_ref[...], g_ref[...], scale_ref[...], shift_ref[...]).astype(BF16)
    lane = lax.broadcasted_iota(jnp.int32, (tm, LANES), 1)
    lo = lane < HEAD_DIM
    if latent:
        cos = cos_ref[...]
        sin = sin_ref[...]
        first16 = (lane % 32) < 16
        rope = lambda v: _rope(v, cos, sin, first16)
    else:
        rope = lambda v: v
    dk_scale = RET_DK ** -0.5
    q_scale = HEAD_DIM ** -0.5 * LOG2E

    r = _nn(hb, w_ref[:, OFF_QA:OFF_QA + GQA_Q_W])
    qg = qg_ref[...]
    for b in range(GQA_Q_W // LANES):
        blk = rope(_head_rms_norm(r[:, b * LANES:(b + 1) * LANES], qg, lo)) * q_scale
        qa_ref[:, b * LANES:(b + 1) * LANES] = blk.astype(BF16)
    assert GQA_KV_W == LANES
    r = _nn(hb, w_ref[:, OFF_KA:OFF_VA + GQA_KV_W])
    kn = _head_rms_norm(r[:, 0:LANES], kg_ref[...], lo)
    vn = r[:, LANES:2 * LANES]
    if not latent:
        seq = ck_ref.shape[4]
        for bb in range(tm // seq):
            for src, dst in ((kn, ck_ref), (vn, cv_ref)):
                t = src[bb * seq:(bb + 1) * seq, :].T
                for kv in range(GQA_KV_HEADS):
                    dst[bb, 0, kv] = t[kv * HEAD_DIM:(kv + 1) * HEAD_DIM]
    kn = rope(kn)
    for src, dst in ((kn, kd_ref), (vn, vd_ref)):
        sw = pltpu.roll(src, HEAD_DIM, 1)
        dst[:, 0:LANES] = jnp.where(lo, src, sw).astype(BF16)
        dst[:, LANES:2 * LANES] = jnp.where(lo, sw, src).astype(BF16)
    r = _nn(hb, w_ref[:, OFF_QR:OFF_QR + RET_QK_W])
    for b in range(RET_QK_W // LANES):
        qr_ref[b] = rope(r[:, b * LANES:(b + 1) * LANES])
    r = _nn(hb, w_ref[:, OFF_KR:OFF_KR + RET_QK_W]) * dk_scale
    for b in range(RET_QK_W // LANES):
        kr_ref[b] = rope(r[:, b * LANES:(b + 1) * LANES])
    pair_w = 2 * RET_DV
    per_chunk = PROJ_CHUNK // pair_w
    for c in range(RET_V_W // PROJ_CHUNK):
        r = _nn(hb, w_ref[:, OFF_GR + c * PROJ_CHUNK:OFF_GR + (c + 1) * PROJ_CHUNK])
        r = r * _sigmoid(r)
        for pp in range(per_chunk):
            gr_ref[c * per_chunk + pp] = r[:, pp * pair_w:(pp + 1) * pair_w]
    for c in range(RET_V_W // PROJ_CHUNK):
        r = _nn(hb, w_ref[:, OFF_VR + c * PROJ_CHUNK:OFF_VR + (c + 1) * PROJ_CHUNK]).astype(BF16)
        for pp in range(per_chunk):
            vr_ref[c * per_chunk + pp] = r[:, pp * pair_w:(pp + 1) * pair_w]


def _in_proj_even(x2d, mods, layer, norm_g, w_bf, q_gain2, k_gain2, rope_tabs, *, n_batch, seq_len, latent):
    n = x2d.shape[0]
    tm = PROJ_ROWS
    pairs = RET_HEADS // 2
    assert w_bf.shape == (D_MODEL, EVEN_IN_W)
    bidx = _batch_index_fn(latent, tm, seq_len)
    row = lambda i: (i, 0)
    const = lambda i: (0, 0)
    in_specs = [
        pl.BlockSpec((tm, D_MODEL), row),
        _mod_spec(layer, 0, bidx),
        _mod_spec(layer, 1, bidx),
        pl.BlockSpec((1, D_MODEL), const),
        pl.BlockSpec(w_bf.shape, const),
        pl.BlockSpec((1, LANES), const),
        pl.BlockSpec((1, LANES), const),
    ]
    args = [x2d, mods, mods, norm_g, w_bf, q_gain2, k_gain2]
    if latent:
        tiles_per_seq = seq_len // tm
        in_specs += [pl.BlockSpec((tm, LANES), lambda i: (i % tiles_per_seq, 0))] * 2
        args += list(rope_tabs)
    out_shape = [
        jax.ShapeDtypeStruct((pairs, n, LANES), F32),
        jax.ShapeDtypeStruct((pairs, n, LANES), F32),
        jax.ShapeDtypeStruct((pairs, n, 2 * RET_DV), BF16),
        jax.ShapeDtypeStruct((pairs, n, 2 * RET_DV), F32),
        jax.ShapeDtypeStruct((n, GQA_Q_W), BF16),
        jax.ShapeDtypeStruct((n, 2 * GQA_KV_W), BF16),
        jax.ShapeDtypeStruct((n, 2 * GQA_KV_W), BF16),
    ]
    out_specs = [pl.BlockSpec((pairs, tm, s.shape[2]), lambda i: (0, i, 0)) if len(s.shape) == 3
                 else pl.BlockSpec((tm, s.shape[1]), row) for s in out_shape]
    if not latent:
        cache = jax.ShapeDtypeStruct((n_batch, 1, GQA_KV_HEADS, HEAD_DIM, seq_len), F32)
        out_shape += [cache, cache]
        out_specs += [pl.BlockSpec((tm // seq_len, 1, GQA_KV_HEADS, HEAD_DIM, seq_len),
                                   lambda i: (i, 0, 0, 0, 0))] * 2
    return pl.pallas_call(
        functools.partial(_in_even_kernel, latent=latent, tm=tm),
        out_shape=out_shape,
        grid=(n // tm,),
        in_specs=in_specs,
        out_specs=out_specs,
        compiler_params=_params(1),
        name="in_proj_even_latent" if latent else "in_proj_even_ctx",
    )(*args)


def _retention_kernel(*refs, n, nb, has_state, write_state):
    lg_ref, q_ref, k_ref, v_ref, gr_ref, gn_ref = refs[:6]
    refs = refs[6:]
    if has_state:
        s0f_ref, s0b_ref = refs[:2]
        refs = refs[2:]
    o_ref = refs[0]
    if write_state:
        sf_ref, sb_ref = refs[1:3]
    c = RET_CHUNK
    assert c == LANES and 2 * RET_DK == LANES
    nc = n // c
    p = pl.program_id(0)
    lgf = [lg_ref[0, 2 * p + hh] for hh in range(2)]
    lgb = [lg_ref[1, 2 * p + hh] for hh in range(2)]
    row = lax.broadcasted_iota(jnp.int32, (c, c), 0)
    col = lax.broadcasted_iota(jnp.int32, (c, c), 1)
    diff = (row - col).astype(F32)
    pos = row.astype(F32)
    lane_lo = col < RET_DK

    def both_scans(f, b):
        return (jnp.where(diff >= 0, jnp.exp(f * jnp.maximum(diff, 0.0)), 0.0)
                + jnp.where(diff <= 0, jnp.exp(b * jnp.maximum(-diff, 0.0)), 0.0))

    decay2 = jnp.concatenate([both_scans(lgf[0], lgb[0]), both_scans(lgf[1], lgb[1])], axis=1)
    lgf_lane = jnp.where(lane_lo, lgf[0], lgf[1])
    lgb_lane = jnp.where(lane_lo, lgb[0], lgb[1])
    qd_f = jnp.exp(lgf_lane * (pos + 1.0))
    kd_f = jnp.exp(lgf_lane * (c - 1.0 - pos))
    qd_b = jnp.exp(lgb_lane * (c - pos))
    kd_b = jnp.exp(lgb_lane * pos)
    srow = lax.broadcasted_iota(jnp.int32, (c, 2 * RET_DV), 0)
    scol = lax.broadcasted_iota(jnp.int32, (c, 2 * RET_DV), 1)
    row_a = srow < RET_DK
    col_a = scol < RET_DV
    own = row_a == col_a
    cd_f = jnp.exp(jnp.where(row_a, lgf[0], lgf[1]) * float(c))
    cd_b = jnp.exp(jnp.where(row_a, lgb[0], lgb[1]) * float(c))
    zeros_half = jnp.zeros((RET_DK, RET_DV), F32)
    gn = gn_ref[...]

    def place(s0_ref, bi):
        top = jnp.concatenate([s0_ref[bi, 0, 0], zeros_half], axis=1)
        bot = jnp.concatenate([zeros_half, s0_ref[bi, 0, 1]], axis=1)
        return jnp.concatenate([top, bot], axis=0)

    seqs = range(nb)
    rows = {bi: [slice(bi * n + i * c, bi * n + (i + 1) * c) for i in range(nc)] for bi in seqs}
    kv_f = {(bi, i): _tn((k_ref[rows[bi][i], :] * kd_f).astype(BF16), v_ref[rows[bi][i], :])
            for bi in seqs for i in range(nc)}
    kv_b = {(bi, i): _tn((k_ref[rows[bi][i], :] * kd_b).astype(BF16), v_ref[rows[bi][i], :])
            for bi in seqs for i in range(nc)}
    before_f, before_b = {}, {}
    for bi in seqs:
        if has_state:
            s_f = place(s0f_ref, bi)
            s_b = place(s0b_ref, bi)
        else:
            s_f = jnp.zeros((c, 2 * RET_DV), F32)
            s_b = jnp.zeros((c, 2 * RET_DV), F32)
        for i in range(nc):
            before_f[(bi, i)] = s_f
            s_f = s_f * cd_f + jnp.where(own, kv_f[(bi, i)], 0.0)
        for i in reversed(range(nc)):
            before_b[(bi, i)] = s_b
            s_b = s_b * cd_b + jnp.where(own, kv_b[(bi, i)], 0.0)
        if write_state:
            for hh in range(2):
                blk = (slice(hh * RET_DK, (hh + 1) * RET_DK), slice(hh * RET_DV, (hh + 1) * RET_DV))
                sf_ref[bi, 0, hh] = s_f[blk]
                sb_ref[bi, 0, hh] = s_b[blk]
    for bi in seqs:
        for i in range(nc):
            r = rows[bi][i]
            qc = q_ref[r, :]
            kc = k_ref[r, :]
            vc = v_ref[r, :]
            k_cat = jnp.concatenate([jnp.where(lane_lo, kc, 0.0), jnp.where(lane_lo, 0.0, kc)], axis=0)
            v_blk = jnp.concatenate([jnp.where(col_a, vc, jnp.zeros_like(vc)),
                                     jnp.where(col_a, jnp.zeros_like(vc), vc)], axis=0)
            scores = _nt(qc.astype(BF16), k_cat.astype(BF16)) * decay2
            q_cat = jnp.concatenate([(qc * qd_f).astype(BF16), (qc * qd_b).astype(BF16)], axis=1)
            s_cat = jnp.concatenate([before_f[(bi, i)], before_b[(bi, i)]], axis=0).astype(BF16)
            o = _nn(scores.astype(BF16), v_blk) + _nn(q_cat, s_cat)
            for hh in range(2):
                vcols = slice(hh * RET_DV, (hh + 1) * RET_DV)
                oh = o[:, vcols]
                mu = jnp.mean(oh, axis=-1, keepdims=True)
                d = oh - mu
                var = jnp.mean(d * d, axis=-1, keepdims=True)
                y = d * lax.rsqrt(var + GN_EPS) * gn[:, vcols] * gr_ref[r, vcols]
                o_ref[r, vcols] = y.astype(BF16)


def _retention(log_g, qr, kr, vr, gr, gn, state_f, state_b, *, n_batch, seq_len, write_state):
    pairs, n, _ = qr.shape
    has_state = state_f is not None
    nb = max(1, RET_ROWS // seq_len)
    rows = nb * seq_len
    tok = lambda p, g: (g, p)
    slab = lambda p, g: (p, g, 0)
    in_specs = [
        pl.BlockSpec(memory_space=pltpu.SMEM),
        pl.BlockSpec((None, rows, LANES), slab),
        pl.BlockSpec((None, rows, LANES), slab),
        pl.BlockSpec((None, rows, 2 * RET_DV), slab),
        pl.BlockSpec((None, rows, 2 * RET_DV), slab),
        pl.BlockSpec((1, 2 * RET_DV), lambda p, g: (0, p)),
    ]
    args = [log_g, qr, kr, vr, gr, gn]
    state_spec = pl.BlockSpec((nb, 1, 2, RET_DK, RET_DV), lambda p, g: (g, 0, p, 0, 0))
    if has_state:
        in_specs += [state_spec, state_spec]
        args += [state_f, state_b]
    out_shape = [jax.ShapeDtypeStruct((n, RET_HEADS * RET_DV), BF16)]
    out_specs = [pl.BlockSpec((rows, 2 * RET_DV), tok)]
    if write_state:
        st = jax.ShapeDtypeStruct((n_batch, 1, RET_HEADS, RET_DK, RET_DV), F32)
        out_shape += [st, st]
        out_specs += [state_spec, state_spec]
    return pl.pallas_call(
        functools.partial(_retention_kernel, n=seq_len, nb=nb, has_state=has_state, write_state=write_state),
        out_shape=out_shape,
        grid=(pairs, n_batch // nb),
        in_specs=in_specs,
        out_specs=out_specs,
        compiler_params=_params(2),
        name="retention_latent" if has_state else "retention_ctx",
    )(*args)


def _gqa_kernel(*refs, n_src, tq, n_seq):
    q_ref = refs[0]
    k_refs = refs[1:1 + 2 * n_src:2]
    v_refs = refs[2:2 + 2 * n_src:2]
    o_ref = refs[1 + 2 * n_src]
    units = [(b, g, half) for b in range(n_seq) for g in range(GQA_KV_HEADS) for half in range(2)]
    outs = {}

    def kv_rows(ref, b):
        n_keys = ref.shape[0] // n_seq
        return slice(b * n_keys, (b + 1) * n_keys)

    def scores_of(unit):
        b, g, half = unit
        rows, base = slice(b * tq, (b + 1) * tq), g * 2 * LANES
        q = jnp.concatenate([q_ref[rows, base:base + LANES], q_ref[rows, base + LANES:base + 2 * LANES]], axis=0)
        return [_nt(q, _masked_half(k_ref[kv_rows(k_ref, b), g * LANES:(g + 1) * LANES], half)) for k_ref in k_refs]

    def finish(unit, scores):
        b, g, half = unit
        outs[half] = _softmax_apply(scores, [v_ref[kv_rows(v_ref, b), g * LANES:(g + 1) * LANES] for v_ref in v_refs])
        if half == 1:
            rows, base = slice(b * tq, (b + 1) * tq), g * 2 * LANES
            o = jnp.where(_lane_lo(2 * tq), outs[0], outs[1]).astype(BF16)
            o_ref[rows, base:base + LANES] = o[:tq]
            o_ref[rows, base + LANES:base + 2 * LANES] = o[tq:]

    _run_pipelined(units, scores_of, finish)


def _gqa_attention(qa, kd, vd, ctx_kd, ctx_vd, *, n_batch, seq_len, tq):
    n = qa.shape[0]
    tiles = seq_len // tq
    n_src = 1 if ctx_kd is None else 2
    n_seq = GQA_CTX_SEQS if (tiles == 1 and n_src == 1) else 1
    qmap = lambda b, t: (b * tiles + t, 0)
    kmap = lambda b, t: (b, 0)
    in_specs = [pl.BlockSpec((n_seq * tq, GQA_Q_W), qmap),
                pl.BlockSpec((n_seq * seq_len, 2 * LANES), kmap),
                pl.BlockSpec((n_seq * seq_len, 2 * LANES), kmap)]
    args = [qa, kd, vd]
    if n_src == 2:
        past = ctx_kd.shape[1]
        cmap = lambda b, t: (b, 0, 0)
        in_specs += [pl.BlockSpec((None, past, 2 * LANES), cmap)] * 2
        args += [ctx_kd, ctx_vd]
    return pl.pallas_call(
        functools.partial(_gqa_kernel, n_src=n_src, tq=tq, n_seq=n_seq),
        out_shape=jax.ShapeDtypeStruct((n, GQA_Q_W), BF16),
        grid=(n_batch // n_seq, tiles),
        in_specs=in_specs,
        out_specs=pl.BlockSpec((n_seq * tq, GQA_Q_W), qmap),
        compiler_params=_params(2),
        name="gqa_latent" if n_src == 2 else "gqa_ctx",
    )(*args)


def _zero_rows(arr, rows):
    pieces, cur = [], 0
    sub = lax.broadcasted_iota(jnp.int32, (8, arr.shape[1]), 0)
    for r in sorted(rows):
        g0 = (r // 8) * 8
        if g0 > cur:
            pieces.append(arr[cur:g0])
        pieces.append(jnp.where(sub == r - g0, 0.0, arr[g0:g0 + 8]))
        cur = g0 + 8
    if cur < arr.shape[0]:
        pieces.append(arr[cur:])
    return jnp.concatenate(pieces, axis=0)


def _mix_ffn_kernel(*refs, tm, seq_len, final, n_mix):
    x_ref, gate_mix_ref, g_ref, scale_ref, shift_ref, gate_ref = refs[:6]
    m_refs = refs[6:6 + n_mix]
    w_refs = refs[6 + n_mix:6 + 2 * n_mix]
    wup_ref, cw_ref, cb_ref, wd_ref = refs[6 + 2 * n_mix:10 + 2 * n_mix]
    refs = refs[10 + 2 * n_mix:]
    if final:
        gfin_ref = refs[0]
        refs = refs[1:]
    o_ref, act_ref, hb_ref = refs
    halves = [slice(rb * (tm // 2), (rb + 1) * (tm // 2)) for rb in range(2)]
    for rows in halves:
        acc = None
        for m_ref, w_ref in zip(m_refs, w_refs):
            t = _nn(m_ref[rows, :], w_ref[...])
            acc = t if acc is None else acc + t
        y = x_ref[rows, :] + gate_mix_ref[...] * acc
        o_ref[rows, :] = y
        hb_ref[rows, :] = _modulated_norm(y, g_ref[...], scale_ref[...], shift_ref[...]).astype(BF16)
    seq_starts = list(range(0, tm, seq_len))
    seq_ends = [s + seq_len - 1 for s in seq_starts]
    for j in range(N_FF_CHUNKS):
        parts = []
        for off in (0, D_FF):
            cols = slice(off + j * FF_CHUNK, off + (j + 1) * FF_CHUNK)
            u = _nn(hb_ref[...], wup_ref[:, cols])
            cw = cw_ref[:, cols]
            prev = _zero_rows(pltpu.roll(u, 1, 0), seq_starts)
            nxt = _zero_rows(pltpu.roll(u, tm - 1, 0), seq_ends)
            parts.append(prev * cw[0:1] + u * cw[1:2] + nxt * cw[2:3] + cb_ref[:, cols])
        a, g = parts
        act_ref[:, j * FF_CHUNK:(j + 1) * FF_CHUNK] = (a * _sigmoid(a) * g).astype(BF16)
    for rows in halves:
        y = o_ref[rows, :] + gate_ref[...] * _nn(act_ref[rows, :], wd_ref[...])
        if final:
            ms = jnp.mean(y * y, axis=-1, keepdims=True)
            y = y * lax.rsqrt(ms + EPS) * gfin_ref[...]
        o_ref[rows, :] = y


def _mix_ffn(x2d, mods, layer, mixes, weights, ffn_norm_g, wup_c, cw_c, cb_c, wd_c, final_g, *, seq_len, latent):
    n = x2d.shape[0]
    tm = FFN_ROWS
    bidx = _batch_index_fn(latent, tm, seq_len)
    row = lambda i: (i, 0)
    const2 = lambda i: (0, 0)
    once = pl.Buffered(1)
    resident = lambda a: pl.BlockSpec((None,) + a.shape[1:], lambda i: (layer,) + (0,) * (a.ndim - 1),
                                      pipeline_mode=once)
    in_specs = [pl.BlockSpec((tm, D_MODEL), row), _mod_spec(layer, 2, bidx), pl.BlockSpec((1, D_MODEL), const2),
                _mod_spec(layer, 4, bidx), _mod_spec(layer, 3, bidx), _mod_spec(layer, 5, bidx)]
    in_specs += [pl.BlockSpec((tm, m.shape[1]), row) for m in mixes]
    in_specs += [pl.BlockSpec(w.shape, const2, pipeline_mode=once) for w in weights]
    in_specs += [resident(wup_c), resident(cw_c), resident(cb_c), resident(wd_c)]
    args = [x2d, mods, ffn_norm_g, mods, mods, mods, *mixes, *weights, wup_c, cw_c, cb_c, wd_c]
    final = final_g is not None
    if final:
        in_specs.append(pl.BlockSpec((1, D_MODEL), const2))
        args.append(final_g)
    return pl.pallas_call(
        functools.partial(_mix_ffn_kernel, tm=tm, seq_len=seq_len, final=final, n_mix=len(mixes)),
        out_shape=jax.ShapeDtypeStruct((n, D_MODEL), F32),
        grid=(n // tm,),
        in_specs=in_specs,
        out_specs=pl.BlockSpec((tm, D_MODEL), row),
        scratch_shapes=[pltpu.VMEM((tm, D_FF), BF16), pltpu.VMEM((tm, D_MODEL), BF16)],
        compiler_params=_params(1),
        name="mix_ffn_latent" if latent else "mix_ffn_ctx",
    )(*args)


def _in_odd_kernel(*refs, write_cache, tm):
    x_ref, shift_ref, scale_ref, g_ref, w_ref, q_ref, k_ref, v_ref = refs[:8]
    hb = _modulated_norm(x_ref[...], g_ref[...], scale_ref[...], shift_ref[...]).astype(BF16)
    q_scale = HEAD_DIM ** -0.5 * LOG2E
    chunks = NA_W // PROJ_CHUNK
    for which, dst in ((1, k_ref), (2, v_ref)):
        for c in range(chunks):
            r = _nn(hb, w_ref[:, which * NA_W + c * PROJ_CHUNK:which * NA_W + (c + 1) * PROJ_CHUNK])
            dst[:, c * PROJ_CHUNK:(c + 1) * PROJ_CHUNK] = r.astype(BF16)
            if write_cache:
                cache_ref = refs[8 + which - 1]
                seq = cache_ref.shape[4]
                for bb in range(tm // seq):
                    for blk in range(PROJ_CHUNK // LANES):
                        t = r[bb * seq:(bb + 1) * seq, blk * LANES:(blk + 1) * LANES].T
                        for hh in range(2):
                            head = c * (PROJ_CHUNK // HEAD_DIM) + 2 * blk + hh
                            cache_ref[bb, 0, head] = t[hh * HEAD_DIM:(hh + 1) * HEAD_DIM]
    for c in range(chunks):
        cols = slice(c * PROJ_CHUNK, (c + 1) * PROJ_CHUNK)
        q_ref[:, cols] = (_nn(hb, w_ref[:, cols]) * q_scale).astype(BF16)


def _in_proj_odd(x2d, mods, layer, norm_g, w_bf, *, n_batch, seq_len, latent):
    n = x2d.shape[0]
    tm = PROJ_ROWS
    width = NA_W
    bidx = _batch_index_fn(latent, tm, seq_len)
    row = lambda i: (i, 0)
    const = lambda i: (0, 0)
    in_specs = [pl.BlockSpec((tm, D_MODEL), row), _mod_spec(layer, 0, bidx), _mod_spec(layer, 1, bidx),
                pl.BlockSpec((1, D_MODEL), const), pl.BlockSpec(w_bf.shape, const)]
    out_shape = [jax.ShapeDtypeStruct((n, width), BF16)] * 3
    out_specs = [pl.BlockSpec((tm, width), row)] * 3
    write_cache = not latent
    if write_cache:
        cache = jax.ShapeDtypeStruct((n_batch, 1, NA_HEADS, HEAD_DIM, seq_len), F32)
        out_shape += [cache, cache]
        out_specs += [pl.BlockSpec((tm // seq_len, 1, NA_HEADS, HEAD_DIM, seq_len),
                                   lambda i: (i, 0, 0, 0, 0))] * 2
    return pl.pallas_call(
        functools.partial(_in_odd_kernel, write_cache=write_cache, tm=tm),
        out_shape=out_shape,
        grid=(n // tm,),
        in_specs=in_specs,
        out_specs=out_specs,
        compiler_params=_params(1),
        name="in_proj_odd_latent" if latent else "in_proj_odd_ctx",
    )(x2d, mods, mods, norm_g, w_bf)


def _dense_pairs_kernel(q_ref, k_ref, v_ref, o_ref, *, seq_len):
    n_seq = q_ref.shape[0] // seq_len
    units = [(b, p, half) for b in range(n_seq) for p in range(NA_HEADS // 2) for half in range(2)]
    outs = {}

    def scores_of(unit):
        b, p, half = unit
        rows, cols = slice(b * seq_len, (b + 1) * seq_len), slice(p * LANES, (p + 1) * LANES)
        return [_nt(q_ref[rows, cols], _masked_half(k_ref[rows, cols], half))]

    def finish(unit, scores):
        b, p, half = unit
        rows, cols = slice(b * seq_len, (b + 1) * seq_len), slice(p * LANES, (p + 1) * LANES)
        outs[half] = _softmax_apply(scores, [v_ref[rows, cols]])
        if half == 1:
            o_ref[rows, cols] = jnp.where(_lane_lo(seq_len), outs[0], outs[1]).astype(BF16)

    _run_pipelined(units, scores_of, finish)


def _dense_attention_ctx(q, k, v, *, n_batch, seq_len):
    width = NA_W
    n_seq = DENSE_CTX_SEQS
    spec = pl.BlockSpec((n_seq * seq_len, width), lambda b: (b, 0))
    return pl.pallas_call(
        functools.partial(_dense_pairs_kernel, seq_len=seq_len),
        out_shape=jax.ShapeDtypeStruct(q.shape, BF16),
        grid=(n_batch // n_seq,),
        in_specs=[spec, spec, spec],
        out_specs=spec,
        compiler_params=_params(1),
        name="dense_attention_ctx",
    )(q, k, v)


NA_Q_ROWS = 8
NA_KEY_ROWS = 12


def _na_window_start(r, n_rows):
    return min(max(r - NA_KH // 2, 0), n_rows - NA_KH)


def _na_bias_tile(bias_ref, half, tile, n_rows):
    r0 = tile * NA_Q_ROWS
    kr0 = min(max(r0 - NA_KH // 2, 0), n_rows - NA_KEY_ROWS)
    lo = _lane_lo(GRID_W)
    neg_block = jnp.full((GRID_W, LANES), NEG, F32)
    left_off = jnp.where(lo, NEG, 0.0)
    right_off = jnp.where(lo, 0.0, NEG)
    rows = []
    for rq in range(NA_Q_ROWS):
        r = r0 + rq
        rs = _na_window_start(r, n_rows)
        blocks = []
        for kk in range(NA_KEY_ROWS // 2):
            ka = kr0 + 2 * kk
            va = rs <= ka < rs + NA_KH
            vb = rs <= ka + 1 < rs + NA_KH
            if not (va or vb):
                blocks.append(neg_block)
                continue
            blk = bias_ref[half, ka - r + NA_KH]
            if not va:
                blk = blk + left_off
            if not vb:
                blk = blk + right_off
            blocks.append(blk)
        rows.append(jnp.concatenate(blocks, axis=1))
    return jnp.concatenate(rows, axis=0), kr0


def _na_kernel(q_ref, k_ref, v_ref, ck_ref, cv_ref, bias_ref, o_ref, *, n_rows):
    tq = NA_Q_ROWS * GRID_W
    span = NA_KEY_ROWS * GRID_W
    units = [(tile, half) for tile in range(n_rows // NA_Q_ROWS) for half in range(2)]
    outs = {}

    def window(tile):
        kr0 = min(max(tile * NA_Q_ROWS - NA_KH // 2, 0), n_rows - NA_KEY_ROWS)
        return slice(kr0 * GRID_W, kr0 * GRID_W + span)

    def scores_of(unit):
        tile, half = unit
        q = q_ref[tile * tq:(tile + 1) * tq, :]
        bias, _ = _na_bias_tile(bias_ref, half, tile, n_rows)
        ck_t = jnp.concatenate([ck_ref[0], ck_ref[1]], axis=0)
        own = (lax.broadcasted_iota(jnp.int32, ck_t.shape, 0) < HEAD_DIM) == (half == 0)
        return [_nt(q, _masked_half(k_ref[window(tile), :], half)) + bias,
                _nn(q, jnp.where(own, ck_t, 0.0).astype(BF16))]

    def finish(unit, scores):
        tile, half = unit
        cv_t = jnp.concatenate([cv_ref[0], cv_ref[1]], axis=0).astype(BF16)
        outs[half] = _softmax_apply(scores, [v_ref[window(tile), :], ("t", cv_t)])
        if half == 1:
            o_ref[tile * tq:(tile + 1) * tq, :] = jnp.where(_lane_lo(tq), outs[0], outs[1]).astype(BF16)

    _run_pipelined(units, scores_of, finish)


def _na_bias_table(rpb):
    cidx = np.arange(GRID_W)
    cs = np.clip(cidx - NA_KW // 2, 0, GRID_W - NA_KW)
    kc = np.arange(GRID_W)
    inside = (kc[None, :] >= cs[:, None]) & (kc[None, :] < cs[:, None] + NA_KW)
    rel = kc[None, :] - cidx[:, None] + NA_KW - 1
    onehot = (rel[None] == np.arange(2 * NA_KW - 1)[:, None, None]) & inside[None]
    m = jnp.einsum("hdj,jck->hdck", rpb * LOG2E, jnp.asarray(onehot, F32), precision=lax.Precision.HIGHEST)
    m = jnp.where(jnp.asarray(inside)[None, None], m, NEG)
    neg = jnp.full((rpb.shape[0], 1, GRID_W, GRID_W), NEG, F32)
    left = jnp.concatenate([neg, m], axis=1)
    right = jnp.concatenate([m, neg], axis=1)
    return jnp.concatenate([left, right], axis=-1)


def _na_attention(q, k, v, ctx_k, ctx_v, bias_tab, *, n_batch, seq_len):
    pairs = NA_HEADS // 2
    tok = lambda p, b: (b, p)
    ctx = lambda p, b: (b, p, 0, 0)
    past = ctx_k.shape[-1]
    return pl.pallas_call(
        functools.partial(_na_kernel, n_rows=seq_len // GRID_W),
        out_shape=jax.ShapeDtypeStruct(q.shape, BF16),
        grid=(pairs, n_batch),
        in_specs=[pl.BlockSpec((seq_len, LANES), tok)] * 3
        + [pl.BlockSpec((None, 2, HEAD_DIM, past), ctx)] * 2
        + [pl.BlockSpec((2, 2 * NA_KH, GRID_W, LANES), lambda p, b: (p, 0, 0, 0))],
        out_specs=pl.BlockSpec((seq_len, LANES), tok),
        compiler_params=_params(2),
        name="neighbourhood_attention",
    )(q, k, v, ctx_k, ctx_v, bias_tab)


def _rope_tables(n):
    t = np.arange(n)
    row = (t // GRID_W).astype(np.float64)
    col = (t % GRID_W).astype(np.float64)
    half = HEAD_DIM // 2
    inv = ROPE_BASE ** (-np.arange(0, half, 2, dtype=np.float64) / half)
    ang_r = row[:, None] * inv
    ang_c = col[:, None] * inv
    cos_h = np.concatenate([np.cos(ang_r)] * 2 + [np.cos(ang_c)] * 2, axis=-1)
    sin_h = np.concatenate([-np.sin(ang_r), np.sin(ang_r), -np.sin(ang_c), np.sin(ang_c)], axis=-1)
    return (jnp.asarray(np.concatenate([cos_h, cos_h], axis=-1), F32),
            jnp.asarray(np.concatenate([sin_h, sin_h], axis=-1), F32))


def _ffn_weights(w_up, conv_w, conv_b, w_down):
    return w_up.astype(BF16), conv_w, conv_b[:, None, :], w_down.astype(BF16)


def _token_major_dup(cache):
    b, kv, t, d = cache.shape
    c = jnp.transpose(cache, (0, 2, 1, 3))[:, :, :, None, :]
    return jnp.broadcast_to(c, (b, t, kv, 2, d)).reshape(b, t, kv * 2 * d).astype(BF16)


def _head_transposed(cache):
    return jnp.swapaxes(cache, -1, -2)


def kernel(x_prompt, x_sample, state_ret_fwd, state_ret_bwd, cache_gqa_k, cache_gqa_v, cache_na_k, cache_na_v,
           c, c_ctx, ada_w, ada_b, norm_mix, norm_ffn, norm_final, even_w_in, even_w_out, ret_decay_fwd,
           ret_decay_bwd, ret_gn, gqa_q_norm, gqa_k_norm, odd_w_in, odd_w_out, na_rpb, ffn_w_up, ffn_conv_w,
           ffn_conv_b, ffn_w_down):
    nb_c, len_c, _ = x_prompt.shape
    nb_s, len_s, _ = x_sample.shape
    depth = ada_w.shape[0]
    streams = {
        False: dict(n_batch=nb_c, seq_len=len_c),
        True: dict(n_batch=nb_s, seq_len=len_s),
    }
    xs = {False: x_prompt.reshape(nb_c * len_c, D_MODEL), True: x_sample.reshape(nb_s * len_s, D_MODEL)}

    rows = 8 * (-(-(1 + nb_s) // 8))
    cond = jnp.zeros((rows, D_MODEL), F32).at[0].set(c_ctx).at[1:1 + nb_s].set(c)
    mods = _ada_params(cond, ada_w, ada_b).reshape(depth, rows, 6, 1, D_MODEL)
    rope_tabs = _rope_tables(len_s)
    ffn_w = _ffn_weights(ffn_w_up, ffn_conv_w, ffn_conv_b, ffn_w_down)
    outs = {}
    mixed = {}

    for l in range(depth):
        g_mix = norm_mix[l][None, :]
        g_ffn = norm_ffn[l][None, :]
        if l % 2 == 0:
            e = l // 2
            w_in = even_w_in[e].astype(BF16)
            w_out = even_w_out[e].astype(BF16)
            w_out_parts = [w_out[:RET_V_W], w_out[RET_V_W:]]
            log_g = jnp.stack([jax.nn.log_sigmoid(ret_decay_fwd[e].astype(F32)),
                               jax.nn.log_sigmoid(ret_decay_bwd[e].astype(F32))])
            gn = ret_gn[e][None, :]
            qg2 = jnp.tile(gqa_q_norm[e], 2)[None, :]
            kg2 = jnp.tile(gqa_k_norm[e], 2)[None, :]
            for latent in (False, True):
                st = streams[latent]
                res = _in_proj_even(xs[latent], mods, l, g_mix, w_in, qg2, kg2, rope_tabs, latent=latent, **st)
                qr, kr, vr, gr, qa, kd, vd = res[:7]
                if latent:
                    ret = _retention(log_g, qr, kr, vr, gr, gn, state_ret_fwd[:, e:e + 1],
                                     state_ret_bwd[:, e:e + 1], write_state=False, **st)[0]
                    att = _gqa_attention(qa, kd, vd, _token_major_dup(cache_gqa_k[:, e]),
                                         _token_major_dup(cache_gqa_v[:, e]), tq=GQA_Q_ROWS, **st)
                else:
                    outs.setdefault("gk", []).append(res[7])
                    outs.setdefault("gv", []).append(res[8])
                    ret, s_f, s_b = _retention(log_g, qr, kr, vr, gr, gn, None, None, write_state=True, **st)
                    outs.setdefault("sf", []).append(s_f)
                    outs.setdefault("sb", []).append(s_b)
                    att = _gqa_attention(qa, kd, vd, None, None, tq=st["seq_len"], **st)
                mixed[latent] = ([ret, att], w_out_parts)
        else:
            o = l // 2
            w_in = odd_w_in[o].astype(BF16)
            w_out = odd_w_out[o].astype(BF16)
            for latent in (False, True):
                st = streams[latent]
                res = _in_proj_odd(xs[latent], mods, l, g_mix, w_in, latent=latent, **st)
                q, k, v = res[:3]
                if latent:
                    att = _na_attention(q, k, v, _head_transposed(cache_na_k[:, o]), _head_transposed(cache_na_v[:, o]),
                                        _na_bias_table(na_rpb[o]), **st)
                else:
                    outs.setdefault("nk", []).append(res[3])
                    outs.setdefault("nv", []).append(res[4])
                    att = _dense_attention_ctx(q, k, v, **st)
                mixed[latent] = ([att], [w_out])
        final_g = norm_final[None, :] if l == depth - 1 else None
        for latent in (False, True):
            xs[latent] = _mix_ffn(xs[latent], mods, l, *mixed[latent], g_ffn, *ffn_w, final_g,
                                   seq_len=streams[latent]["seq_len"], latent=latent)

    tr = lambda a: jnp.swapaxes(a, -1, -2)
    cat = lambda name: outs[name][0] if len(outs[name]) == 1 else jnp.concatenate(outs[name], axis=1)
    return (xs[False].reshape(nb_c, len_c, D_MODEL), xs[True].reshape(nb_s, len_s, D_MODEL),
            cat("sf"), cat("sb"), tr(cat("gk")), tr(cat("gv")), tr(cat("nk")), tr(cat("nv")))
```

```python
import functools

import numpy as np
import jax
import jax.numpy as jnp
from jax import lax
from jax.experimental import pallas as pl
from jax.experimental.pallas import tpu as pltpu

F32 = jnp.float32
BF16 = jnp.bfloat16

D_MODEL = 1024
GRID_W = 64
HEAD_DIM = 64
ROPE_BASE = 10000.0
EPS = 1e-6
GN_EPS = 1e-5
RET_HEADS = 8
RET_DK = 64
RET_DV = 128
RET_CHUNK = 128
GQA_HEADS = 8
GQA_KV_HEADS = 2
NA_HEADS = 16
NA_KH = 8
NA_KW = 16
D_FF = 2816
LANES = 128
MXU_DIM = 256
FF_CHUNK = MXU_DIM
N_FF_CHUNKS = D_FF // FF_CHUNK
PROJ_CHUNK = 2 * MXU_DIM
RET_QK_W = RET_HEADS * RET_DK
RET_V_W = RET_HEADS * RET_DV
GQA_Q_W = GQA_HEADS * HEAD_DIM
GQA_KV_W = GQA_KV_HEADS * HEAD_DIM
NA_W = NA_HEADS * HEAD_DIM
OFF_QR, OFF_KR, OFF_VR, OFF_GR, OFF_QA, OFF_KA, OFF_VA, EVEN_IN_W = (
    int(v) for v in np.cumsum([0, RET_QK_W, RET_QK_W, RET_V_W, RET_V_W, GQA_Q_W, GQA_KV_W, GQA_KV_W]))
PROJ_ROWS = 512
FFN_ROWS = 1024
RET_ROWS = 2048
GQA_Q_ROWS = 512
GQA_CTX_SEQS = 2
DENSE_CTX_SEQS = 4
ADA_COLS = 2048
NEG = -1e30
LOG2E = 1.4426950408889634
VMEM_LIMIT = 56 * 1024 * 1024


def _nn(a, b):
    return jnp.dot(a, b, preferred_element_type=F32)


def _nt(a, b):
    return lax.dot_general(a, b, (((1,), (1,)), ((), ())), preferred_element_type=F32)


def _tn(a, b):
    return lax.dot_general(a, b, (((0,), (0,)), ((), ())), preferred_element_type=F32)


def _sigmoid(x):
    return 1.0 / (1.0 + jnp.exp(-x))


def _params(n_axes):
    return pltpu.CompilerParams(dimension_semantics=("arbitrary",) * n_axes, vmem_limit_bytes=VMEM_LIMIT)


def _modulated_norm(x, g, scale, shift):
    ms = jnp.mean(x * x, axis=-1, keepdims=True)
    return (x * lax.rsqrt(ms + EPS) * g) * (1.0 + scale) + shift


def _lane_lo(rows):
    return lax.broadcasted_iota(jnp.int32, (rows, LANES), 1) < HEAD_DIM


def _head_rms_norm(xb, gain, lo):
    sq = xb * xb
    s_lo = jnp.sum(jnp.where(lo, sq, 0.0), axis=-1, keepdims=True)
    s_hi = jnp.sum(jnp.where(lo, 0.0, sq), axis=-1, keepdims=True)
    r = jnp.where(lo, lax.rsqrt(s_lo * (1.0 / HEAD_DIM) + EPS), lax.rsqrt(s_hi * (1.0 / HEAD_DIM) + EPS))
    return xb * r * gain


def _rope(xb, cos, sin_signed, first16):
    partner = jnp.where(first16, pltpu.roll(xb, LANES - 16, 1), pltpu.roll(xb, 16, 1))
    return xb * cos + partner * sin_signed


def _softmax_apply(scores, values):
    m = functools.reduce(jnp.maximum, [jnp.max(s, axis=-1, keepdims=True) for s in scores])
    es = [jnp.exp2(s - m) for s in scores]
    l = functools.reduce(jnp.add, [jnp.sum(e, axis=-1, keepdims=True) for e in es])
    o = None
    for e, v in zip(es, values):
        t = _nt(e.astype(BF16), v[1]) if isinstance(v, tuple) else _nn(e.astype(BF16), v)
        o = t if o is None else o + t
    return o * (1.0 / l)


def _run_pipelined(units, scores_of, finish):
    pending = scores_of(units[0])
    for n, unit in enumerate(units):
        nxt = scores_of(units[n + 1]) if n + 1 < len(units) else None
        finish(unit, pending)
        pending = nxt


def _masked_half(kp, half):
    lo = _lane_lo(kp.shape[0])
    return jnp.where(lo if half == 0 else jnp.logical_not(lo), kp, jnp.zeros_like(kp))


def _ada_kernel(c_ref, w_ref, b_ref, o_ref):
    c = c_ref[...]
    a = (c * _sigmoid(c)).astype(BF16)
    o_ref[...] = _nn(a, w_ref[...].astype(BF16)) + b_ref[...]


def _ada_params(cond, ada_w, ada_b):
    depth = ada_w.shape[0]
    rows = cond.shape[0]
    tn = ADA_COLS
    return pl.pallas_call(
        _ada_kernel,
        out_shape=jax.ShapeDtypeStruct((depth, rows, 6 * D_MODEL), F32),
        grid=(depth, 6 * D_MODEL // tn),
        in_specs=[
            pl.BlockSpec((rows, D_MODEL), lambda l, j: (0, 0)),
            pl.BlockSpec((None, D_MODEL, tn), lambda l, j: (l, 0, j)),
            pl.BlockSpec((None, 1, tn), lambda l, j: (l, 0, j)),
        ],
        out_specs=pl.BlockSpec((None, rows, tn), lambda l, j: (l, 0, j)),
        compiler_params=_params(2),
        name="ada_params",
    )(cond, ada_w, ada_b.reshape(depth, 1, 6 * D_MODEL))


def _mod_spec(layer, which, bidx):
    return pl.BlockSpec((None, None, None, 1, D_MODEL), lambda i: (layer, bidx(i), which, 0, 0))


def _batch_index_fn(latent, rows_per_tile, seq_len):
    if not latent:
        return lambda i: 0
    return lambda i: 1 + (i * rows_per_tile) // seq_len


def _in_even_kernel(*refs, latent, tm):
    if latent:
        (x_ref, shift_ref, scale_ref, g_ref, w_ref, qg_ref, kg_ref, cos_ref, sin_ref,
         qr_ref, kr_ref, kt_ref, vr_ref, gr_ref, qa_ref, kd_ref, vd_ref) = refs
    else:
        (x_ref, shift_ref, scale_ref, g_ref, w_ref, qg_ref, kg_ref,
         qr_ref, kr_ref, kt_ref, vr_ref, gr_ref, qa_ref, kd_ref, vd_ref, ck_ref, cv_ref) = refs
    hb = _modulated_norm(x_ref[...], g_ref[...], scale_ref[...], shift_ref[...]).astype(BF16)
    lane = lax.broadcasted_iota(jnp.int32, (tm, LANES), 1)
    lo = lane < HEAD_DIM
    if latent:
        cos = cos_ref[...]
        sin = sin_ref[...]
        first16 = (lane % 32) < 16
        rope = lambda v: _rope(v, cos, sin, first16)
    else:
        rope = lambda v: v
    dk_scale = RET_DK ** -0.5
    q_scale = HEAD_DIM ** -0.5 * LOG2E

    r = _nn(hb, w_ref[:, OFF_QA:OFF_QA + GQA_Q_W])
    qg = qg_ref[...]
    for b in range(GQA_Q_W // LANES):
        blk = rope(_head_rms_norm(r[:, b * LANES:(b + 1) * LANES], qg, lo)) * q_scale
        qa_ref[:, b * LANES:(b + 1) * LANES] = blk.astype(BF16)
    assert GQA_KV_W == LANES
    r = _nn(hb, w_ref[:, OFF_KA:OFF_VA + GQA_KV_W])
    kn = _head_rms_norm(r[:, 0:LANES], kg_ref[...], lo)
    vn = r[:, LANES:2 * LANES]
    if not latent:
        seq = ck_ref.shape[4]
        for bb in range(tm // seq):
            for src, dst in ((kn, ck_ref), (vn, cv_ref)):
                t = src[bb * seq:(bb + 1) * seq, :].T
                for kv in range(GQA_KV_HEADS):
                    dst[bb, 0, kv] = t[kv * HEAD_DIM:(kv + 1) * HEAD_DIM]
    kn = rope(kn)
    for src, dst in ((kn, kd_ref), (vn, vd_ref)):
        sw = pltpu.roll(src, HEAD_DIM, 1)
        dst[:, 0:LANES] = jnp.where(lo, src, sw).astype(BF16)
        dst[:, LANES:2 * LANES] = jnp.where(lo, sw, src).astype(BF16)
    r = _nn(hb, w_ref[:, OFF_QR:OFF_QR + RET_QK_W])
    for b in range(RET_QK_W // LANES):
        qr_ref[b] = rope(r[:, b * LANES:(b + 1) * LANES])
    r = _nn(hb, w_ref[:, OFF_KR:OFF_KR + RET_QK_W]) * dk_scale
    for b in range(RET_QK_W // LANES):
        blk = rope(r[:, b * LANES:(b + 1) * LANES])
        kr_ref[b] = blk
        kt_ref[b] = blk.T
    pair_w = 2 * RET_DV
    per_chunk = PROJ_CHUNK // pair_w
    for c in range(RET_V_W // PROJ_CHUNK):
        r = _nn(hb, w_ref[:, OFF_GR + c * PROJ_CHUNK:OFF_GR + (c + 1) * PROJ_CHUNK])
        r = r * _sigmoid(r)
        for pp in range(per_chunk):
            gr_ref[c * per_chunk + pp] = r[:, pp * pair_w:(pp + 1) * pair_w]
    for c in range(RET_V_W // PROJ_CHUNK):
        r = _nn(hb, w_ref[:, OFF_VR + c * PROJ_CHUNK:OFF_VR + (c + 1) * PROJ_CHUNK]).astype(BF16)
        for pp in range(per_chunk):
            vr_ref[c * per_chunk + pp] = r[:, pp * pair_w:(pp + 1) * pair_w]


def _in_proj_even(x2d, mods, layer, norm_g, w_bf, q_gain2, k_gain2, rope_tabs, *, n_batch, seq_len, latent):
    n = x2d.shape[0]
    tm = PROJ_ROWS
    pairs = RET_HEADS // 2
    assert w_bf.shape == (D_MODEL, EVEN_IN_W)
    bidx = _batch_index_fn(latent, tm, seq_len)
    row = lambda i: (i, 0)
    const = lambda i: (0, 0)
    in_specs = [
        pl.BlockSpec((tm, D_MODEL), row),
        _mod_spec(layer, 0, bidx),
        _mod_spec(layer, 1, bidx),
        pl.BlockSpec((1, D_MODEL), const),
        pl.BlockSpec(w_bf.shape, const),
        pl.BlockSpec((1, LANES), const),
        pl.BlockSpec((1, LANES), const),
    ]
    args = [x2d, mods, mods, norm_g, w_bf, q_gain2, k_gain2]
    if latent:
        tiles_per_seq = seq_len // tm
        in_specs += [pl.BlockSpec((tm, LANES), lambda i: (i % tiles_per_seq, 0))] * 2
        args += list(rope_tabs)
    out_shape = [
        jax.ShapeDtypeStruct((pairs, n, LANES), F32),
        jax.ShapeDtypeStruct((pairs, n, LANES), F32),
        jax.ShapeDtypeStruct((pairs, LANES, n), F32),
        jax.ShapeDtypeStruct((pairs, n, 2 * RET_DV), BF16),
        jax.ShapeDtypeStruct((pairs, n, 2 * RET_DV), F32),
        jax.ShapeDtypeStruct((n, GQA_Q_W), BF16),
        jax.ShapeDtypeStruct((n, 2 * GQA_KV_W), BF16),
        jax.ShapeDtypeStruct((n, 2 * GQA_KV_W), BF16),
    ]
    out_specs = [pl.BlockSpec((pairs, tm, s.shape[2]), lambda i: (0, i, 0)) if len(s.shape) == 3
                 else pl.BlockSpec((tm, s.shape[1]), row) for s in out_shape]
    out_specs[2] = pl.BlockSpec((pairs, LANES, tm), lambda i: (0, 0, i))
    if not latent:
        cache = jax.ShapeDtypeStruct((n_batch, 1, GQA_KV_HEADS, HEAD_DIM, seq_len), F32)
        out_shape += [cache, cache]
        out_specs += [pl.BlockSpec((tm // seq_len, 1, GQA_KV_HEADS, HEAD_DIM, seq_len),
                                   lambda i: (i, 0, 0, 0, 0))] * 2
    return pl.pallas_call(
        functools.partial(_in_even_kernel, latent=latent, tm=tm),
        out_shape=out_shape,
        grid=(n // tm,),
        in_specs=in_specs,
        out_specs=out_specs,
        compiler_params=_params(1),
        name="in_proj_even_latent" if latent else "in_proj_even_ctx",
    )(*args)


def _retention_kernel(*refs, n, nb, has_state, write_state):
    lg_ref, q_ref, k_ref, kt_ref, v_ref, gr_ref, gn_ref = refs[:7]
    refs = refs[7:]
    if has_state:
        s0f_ref, s0b_ref = refs[:2]
        refs = refs[2:]
    o_ref = refs[0]
    if write_state:
        sf_ref, sb_ref = refs[1:3]
    c = RET_CHUNK
    assert c == LANES and 2 * RET_DK == LANES
    nc = n // c
    p = pl.program_id(0)
    lgf = [lg_ref[0, 2 * p + hh] for hh in range(2)]
    lgb = [lg_ref[1, 2 * p + hh] for hh in range(2)]
    row = lax.broadcasted_iota(jnp.int32, (c, c), 0)
    col = lax.broadcasted_iota(jnp.int32, (c, c), 1)
    diff = (row - col).astype(F32)
    pos = row.astype(F32)
    lane_lo = col < RET_DK

    def both_scans(f, b):
        return (jnp.where(diff >= 0, jnp.exp(f * jnp.maximum(diff, 0.0)), 0.0)
                + jnp.where(diff <= 0, jnp.exp(b * jnp.maximum(-diff, 0.0)), 0.0))

    decay2 = jnp.concatenate([both_scans(lgf[0], lgb[0]), both_scans(lgf[1], lgb[1])], axis=1)
    lgf_lane = jnp.where(lane_lo, lgf[0], lgf[1])
    lgb_lane = jnp.where(lane_lo, lgb[0], lgb[1])
    qd_f = jnp.exp(lgf_lane * (pos + 1.0))
    qd_b = jnp.exp(lgb_lane * (c - pos))
    tok = col.astype(F32)
    lgf_row = jnp.where(row < RET_DK, lgf[0], lgf[1])
    lgb_row = jnp.where(row < RET_DK, lgb[0], lgb[1])
    kd_f_t = jnp.exp(lgf_row * (c - 1.0 - tok))
    kd_b_t = jnp.exp(lgb_row * tok)
    srow = lax.broadcasted_iota(jnp.int32, (c, 2 * RET_DV), 0)
    scol = lax.broadcasted_iota(jnp.int32, (c, 2 * RET_DV), 1)
    row_a = srow < RET_DK
    col_a = scol < RET_DV
    own = row_a == col_a
    cd_f = jnp.exp(jnp.where(row_a, lgf[0], lgf[1]) * float(c))
    cd_b = jnp.exp(jnp.where(row_a, lgb[0], lgb[1]) * float(c))
    zeros_half = jnp.zeros((RET_DK, RET_DV), F32)
    gn = gn_ref[...]

    def place(s0_ref, bi):
        top = jnp.concatenate([s0_ref[bi, 0, 0], zeros_half], axis=1)
        bot = jnp.concatenate([zeros_half, s0_ref[bi, 0, 1]], axis=1)
        return jnp.concatenate([top, bot], axis=0)

    seqs = range(nb)
    rows = {bi: [slice(bi * n + i * c, bi * n + (i + 1) * c) for i in range(nc)] for bi in seqs}
    kv_f = {(bi, i): _nn((kt_ref[:, rows[bi][i]] * kd_f_t).astype(BF16), v_ref[rows[bi][i], :])
            for bi in seqs for i in range(nc)}
    kv_b = {(bi, i): _nn((kt_ref[:, rows[bi][i]] * kd_b_t).astype(BF16), v_ref[rows[bi][i], :])
            for bi in seqs for i in range(nc)}
    before_f, before_b = {}, {}
    for bi in seqs:
        if has_state:
            s_f = place(s0f_ref, bi)
            s_b = place(s0b_ref, bi)
        else:
            s_f = jnp.zeros((c, 2 * RET_DV), F32)
            s_b = jnp.zeros((c, 2 * RET_DV), F32)
        for i in range(nc):
            before_f[(bi, i)] = s_f
            s_f = s_f * cd_f + jnp.where(own, kv_f[(bi, i)], 0.0)
        for i in reversed(range(nc)):
            before_b[(bi, i)] = s_b
            s_b = s_b * cd_b + jnp.where(own, kv_b[(bi, i)], 0.0)
        if write_state:
            for hh in range(2):
                blk = (slice(hh * RET_DK, (hh + 1) * RET_DK), slice(hh * RET_DV, (hh + 1) * RET_DV))
                sf_ref[bi, 0, hh] = s_f[blk]
                sb_ref[bi, 0, hh] = s_b[blk]
    for bi in seqs:
        for i in range(nc):
            r = rows[bi][i]
            qc = q_ref[r, :]
            kc = k_ref[r, :]
            vc = v_ref[r, :]
            k_cat = jnp.concatenate([jnp.where(lane_lo, kc, 0.0), jnp.where(lane_lo, 0.0, kc)], axis=0)
            v_blk = jnp.concatenate([jnp.where(col_a, vc, jnp.zeros_like(vc)),
                                     jnp.where(col_a, jnp.zeros_like(vc), vc)], axis=0)
            scores = _nt(qc.astype(BF16), k_cat.astype(BF16)) * decay2
            q_cat = jnp.concatenate([(qc * qd_f).astype(BF16), (qc * qd_b).astype(BF16)], axis=1)
            s_cat = jnp.concatenate([before_f[(bi, i)], before_b[(bi, i)]], axis=0).astype(BF16)
            o = _nn(scores.astype(BF16), v_blk) + _nn(q_cat, s_cat)
            for hh in range(2):
                vcols = slice(hh * RET_DV, (hh + 1) * RET_DV)
                oh = o[:, vcols]
                mu = jnp.mean(oh, axis=-1, keepdims=True)
                d = oh - mu
                var = jnp.mean(d * d, axis=-1, keepdims=True)
                y = d * lax.rsqrt(var + GN_EPS) * gn[:, vcols] * gr_ref[r, vcols]
                o_ref[r, vcols] = y.astype(BF16)


def _retention(log_g, qr, kr, kt, vr, gr, gn, state_f, state_b, *, n_batch, seq_len, write_state):
    pairs, n, _ = qr.shape
    has_state = state_f is not None
    nb = max(1, RET_ROWS // seq_len)
    rows = nb * seq_len
    tok = lambda p, g: (g, p)
    slab = lambda p, g: (p, g, 0)
    in_specs = [
        pl.BlockSpec(memory_space=pltpu.SMEM),
        pl.BlockSpec((None, rows, LANES), slab),
        pl.BlockSpec((None, rows, LANES), slab),
        pl.BlockSpec((None, LANES, rows), lambda p, g: (p, 0, g)),
        pl.BlockSpec((None, rows, 2 * RET_DV), slab),
        pl.BlockSpec((None, rows, 2 * RET_DV), slab),
        pl.BlockSpec((1, 2 * RET_DV), lambda p, g: (0, p)),
    ]
    args = [log_g, qr, kr, kt, vr, gr, gn]
    state_spec = pl.BlockSpec((nb, 1, 2, RET_DK, RET_DV), lambda p, g: (g, 0, p, 0, 0))
    if has_state:
        in_specs += [state_spec, state_spec]
        args += [state_f, state_b]
    out_shape = [jax.ShapeDtypeStruct((n, RET_HEADS * RET_DV), BF16)]
    out_specs = [pl.BlockSpec((rows, 2 * RET_DV), tok)]
    if write_state:
        st = jax.ShapeDtypeStruct((n_batch, 1, RET_HEADS, RET_DK, RET_DV), F32)
        out_shape += [st, st]
        out_specs += [state_spec, state_spec]
    return pl.pallas_call(
        functools.partial(_retention_kernel, n=seq_len, nb=nb, has_state=has_state, write_state=write_state),
        out_shape=out_shape,
        grid=(pairs, n_batch // nb),
        in_specs=in_specs,
        out_specs=out_specs,
        compiler_params=_params(2),
        name="retention_latent" if has_state else "retention_ctx",
    )(*args)


def _gqa_kernel(*refs, n_src, tq, n_seq):
    q_ref = refs[0]
    k_refs = refs[1:1 + 2 * n_src:2]
    v_refs = refs[2:2 + 2 * n_src:2]
    o_ref = refs[1 + 2 * n_src]
    units = [(b, g, half) for b in range(n_seq) for g in range(GQA_KV_HEADS) for half in range(2)]
    outs = {}

    def kv_rows(ref, b):
        n_keys = ref.shape[0] // n_seq
        return slice(b * n_keys, (b + 1) * n_keys)

    def scores_of(unit):
        b, g, half = unit
        rows, base = slice(b * tq, (b + 1) * tq), g * 2 * LANES
        q = jnp.concatenate([q_ref[rows, base:base + LANES], q_ref[rows, base + LANES:base + 2 * LANES]], axis=0)
        return [_nt(q, _masked_half(k_ref[kv_rows(k_ref, b), g * LANES:(g + 1) * LANES], half)) for k_ref in k_refs]

    def finish(unit, scores):
        b, g, half = unit
        outs[half] = _softmax_apply(scores, [v_ref[kv_rows(v_ref, b), g * LANES:(g + 1) * LANES] for v_ref in v_refs])
        if half == 1:
            rows, base = slice(b * tq, (b + 1) * tq), g * 2 * LANES
            o = jnp.where(_lane_lo(2 * tq), outs[0], outs[1]).astype(BF16)
            o_ref[rows, base:base + LANES] = o[:tq]
            o_ref[rows, base + LANES:base + 2 * LANES] = o[tq:]

    _run_pipelined(units, scores_of, finish)


def _gqa_attention(qa, kd, vd, ctx_kd, ctx_vd, *, n_batch, seq_len, tq):
    n = qa.shape[0]
    tiles = seq_len // tq
    n_src = 1 if ctx_kd is None else 2
    n_seq = GQA_CTX_SEQS if (tiles == 1 and n_src == 1) else 1
    qmap = lambda b, t: (b * tiles + t, 0)
    kmap = lambda b, t: (b, 0)
    in_specs = [pl.BlockSpec((n_seq * tq, GQA_Q_W), qmap),
                pl.BlockSpec((n_seq * seq_len, 2 * LANES), kmap),
                pl.BlockSpec((n_seq * seq_len, 2 * LANES), kmap)]
    args = [qa, kd, vd]
    if n_src == 2:
        past = ctx_kd.shape[1]
        cmap = lambda b, t: (b, 0, 0)
        in_specs += [pl.BlockSpec((None, past, 2 * LANES), cmap)] * 2
        args += [ctx_kd, ctx_vd]
    return pl.pallas_call(
        functools.partial(_gqa_kernel, n_src=n_src, tq=tq, n_seq=n_seq),
        out_shape=jax.ShapeDtypeStruct((n, GQA_Q_W), BF16),
        grid=(n_batch // n_seq, tiles),
        in_specs=in_specs,
        out_specs=pl.BlockSpec((n_seq * tq, GQA_Q_W), qmap),
        compiler_params=_params(2),
        name="gqa_latent" if n_src == 2 else "gqa_ctx",
    )(*args)


def _zero_rows(arr, rows):
    pieces, cur = [], 0
    sub = lax.broadcasted_iota(jnp.int32, (8, arr.shape[1]), 0)
    for r in sorted(rows):
        g0 = (r // 8) * 8
        if g0 > cur:
            pieces.append(arr[cur:g0])
        pieces.append(jnp.where(sub == r - g0, 0.0, arr[g0:g0 + 8]))
        cur = g0 + 8
    if cur < arr.shape[0]:
        pieces.append(arr[cur:])
    return jnp.concatenate(pieces, axis=0)


def _mix_ffn_kernel(*refs, tm, seq_len, final, n_mix):
    x_ref, gate_mix_ref, g_ref, scale_ref, shift_ref, gate_ref = refs[:6]
    m_refs = refs[6:6 + n_mix]
    w_refs = refs[6 + n_mix:6 + 2 * n_mix]
    wup_ref, cw_ref, cb_ref, wd_ref = refs[6 + 2 * n_mix:10 + 2 * n_mix]
    refs = refs[10 + 2 * n_mix:]
    if final:
        gfin_ref = refs[0]
        refs = refs[1:]
    o_ref, act_ref, hb_ref = refs
    halves = [slice(rb * (tm // 2), (rb + 1) * (tm // 2)) for rb in range(2)]
    for rows in halves:
        acc = None
        for m_ref, w_ref in zip(m_refs, w_refs):
            t = _nn(m_ref[rows, :], w_ref[...])
            acc = t if acc is None else acc + t
        y = x_ref[rows, :] + gate_mix_ref[...] * acc
        o_ref[rows, :] = y
        hb_ref[rows, :] = _modulated_norm(y, g_ref[...], scale_ref[...], shift_ref[...]).astype(BF16)
    seq_starts = list(range(0, tm, seq_len))
    seq_ends = [s + seq_len - 1 for s in seq_starts]
    for j in range(N_FF_CHUNKS):
        parts = []
        for off in (0, D_FF):
            cols = slice(off + j * FF_CHUNK, off + (j + 1) * FF_CHUNK)
            u = _nn(hb_ref[...], wup_ref[:, cols])
            cw = cw_ref[:, cols]
            prev = _zero_rows(pltpu.roll(u, 1, 0), seq_starts)
            nxt = _zero_rows(pltpu.roll(u, tm - 1, 0), seq_ends)
            parts.append(prev * cw[0:1] + u * cw[1:2] + nxt * cw[2:3] + cb_ref[:, cols])
        a, g = parts
        act_ref[:, j * FF_CHUNK:(j + 1) * FF_CHUNK] = (a * _sigmoid(a) * g).astype(BF16)
    for rows in halves:
        y = o_ref[rows, :] + gate_ref[...] * _nn(act_ref[rows, :], wd_ref[...])
        if final:
            ms = jnp.mean(y * y, axis=-1, keepdims=True)
            y = y * lax.rsqrt(ms + EPS) * gfin_ref[...]
        o_ref[rows, :] = y


def _mix_ffn(x2d, mods, layer, mixes, weights, ffn_norm_g, wup_c, cw_c, cb_c, wd_c, final_g, *, seq_len, latent):
    n = x2d.shape[0]
    tm = FFN_ROWS
    bidx = _batch_index_fn(latent, tm, seq_len)
    row = lambda i: (i, 0)
    const2 = lambda i: (0, 0)
    once = pl.Buffered(1)
    resident = lambda a: pl.BlockSpec((None,) + a.shape[1:], lambda i: (layer,) + (0,) * (a.ndim - 1),
                                      pipeline_mode=once)
    in_specs = [pl.BlockSpec((tm, D_MODEL), row), _mod_spec(layer, 2, bidx), pl.BlockSpec((1, D_MODEL), const2),
                _mod_spec(layer, 4, bidx), _mod_spec(layer, 3, bidx), _mod_spec(layer, 5, bidx)]
    in_specs += [pl.BlockSpec((tm, m.shape[1]), row) for m in mixes]
    in_specs += [pl.BlockSpec(w.shape, const2, pipeline_mode=once) for w in weights]
    in_specs += [resident(wup_c), resident(cw_c), resident(cb_c), resident(wd_c)]
    args = [x2d, mods, ffn_norm_g, mods, mods, mods, *mixes, *weights, wup_c, cw_c, cb_c, wd_c]
    final = final_g is not None
    if final:
        in_specs.append(pl.BlockSpec((1, D_MODEL), const2))
        args.append(final_g)
    return pl.pallas_call(
        functools.partial(_mix_ffn_kernel, tm=tm, seq_len=seq_len, final=final, n_mix=len(mixes)),
        out_shape=jax.ShapeDtypeStruct((n, D_MODEL), F32),
        grid=(n // tm,),
        in_specs=in_specs,
        out_specs=pl.BlockSpec((tm, D_MODEL), row),
        scratch_shapes=[pltpu.VMEM((tm, D_FF), BF16), pltpu.VMEM((tm, D_MODEL), BF16)],
        compiler_params=_params(1),
        name="mix_ffn_latent" if latent else "mix_ffn_ctx",
    )(*args)


def _in_odd_kernel(*refs, write_cache, tm):
    x_ref, shift_ref, scale_ref, g_ref, w_ref, q_ref, k_ref, v_ref = refs[:8]
    hb = _modulated_norm(x_ref[...], g_ref[...], scale_ref[...], shift_ref[...]).astype(BF16)
    q_scale = HEAD_DIM ** -0.5 * LOG2E
    chunks = NA_W // PROJ_CHUNK
    for which, dst in ((1, k_ref), (2, v_ref)):
        for c in range(chunks):
            r = _nn(hb, w_ref[:, which * NA_W + c * PROJ_CHUNK:which * NA_W + (c + 1) * PROJ_CHUNK])
            dst[:, c * PROJ_CHUNK:(c + 1) * PROJ_CHUNK] = r.astype(BF16)
            if write_cache:
                cache_ref = refs[8 + which - 1]
                seq = cache_ref.shape[4]
                for bb in range(tm // seq):
                    for blk in range(PROJ_CHUNK // LANES):
                        t = r[bb * seq:(bb + 1) * seq, blk * LANES:(blk + 1) * LANES].T
                        for hh in range(2):
                            head = c * (PROJ_CHUNK // HEAD_DIM) + 2 * blk + hh
                            cache_ref[bb, 0, head] = t[hh * HEAD_DIM:(hh + 1) * HEAD_DIM]
    for c in range(chunks):
        cols = slice(c * PROJ_CHUNK, (c + 1) * PROJ_CHUNK)
        q_ref[:, cols] = (_nn(hb, w_ref[:, cols]) * q_scale).astype(BF16)


def _in_proj_odd(x2d, mods, layer, norm_g, w_bf, *, n_batch, seq_len, latent):
    n = x2d.shape[0]
    tm = PROJ_ROWS
    width = NA_W
    bidx = _batch_index_fn(latent, tm, seq_len)
    row = lambda i: (i, 0)
    const = lambda i: (0, 0)
    in_specs = [pl.BlockSpec((tm, D_MODEL), row), _mod_spec(layer, 0, bidx), _mod_spec(layer, 1, bidx),
                pl.BlockSpec((1, D_MODEL), const), pl.BlockSpec(w_bf.shape, const)]
    out_shape = [jax.ShapeDtypeStruct((n, width), BF16)] * 3
    out_specs = [pl.BlockSpec((tm, width), row)] * 3
    write_cache = not latent
    if write_cache:
        cache = jax.ShapeDtypeStruct((n_batch, 1, NA_HEADS, HEAD_DIM, seq_len), F32)
        out_shape += [cache, cache]
        out_specs += [pl.BlockSpec((tm // seq_len, 1, NA_HEADS, HEAD_DIM, seq_len),
                                   lambda i: (i, 0, 0, 0, 0))] * 2
    return pl.pallas_call(
        functools.partial(_in_odd_kernel, write_cache=write_cache, tm=tm),
        out_shape=out_shape,
        grid=(n // tm,),
        in_specs=in_specs,
        out_specs=out_specs,
        compiler_params=_params(1),
        name="in_proj_odd_latent" if latent else "in_proj_odd_ctx",
    )(x2d, mods, mods, norm_g, w_bf)


def _dense_pairs_kernel(q_ref, k_ref, v_ref, o_ref, *, seq_len):
    n_seq = q_ref.shape[0] // seq_len
    units = [(b, p, half) for b in range(n_seq) for p in range(NA_HEADS // 2) for half in range(2)]
    outs = {}

    def scores_of(unit):
        b, p, half = unit
        rows, cols = slice(b * seq_len, (b + 1) * seq_len), slice(p * LANES, (p + 1) * LANES)
        return [_nt(q_ref[rows, cols], _masked_half(k_ref[rows, cols], half))]

    def finish(unit, scores):
        b, p, half = unit
        rows, cols = slice(b * seq_len, (b + 1) * seq_len), slice(p * LANES, (p + 1) * LANES)
        outs[half] = _softmax_apply(scores, [v_ref[rows, cols]])
        if half == 1:
            o_ref[rows, cols] = jnp.where(_lane_lo(seq_len), outs[0], outs[1]).astype(BF16)

    _run_pipelined(units, scores_of, finish)


def _dense_attention_ctx(q, k, v, *, n_batch, seq_len):
    width = NA_W
    n_seq = DENSE_CTX_SEQS
    spec = pl.BlockSpec((n_seq * seq_len, width), lambda b: (b, 0))
    return pl.pallas_call(
        functools.partial(_dense_pairs_kernel, seq_len=seq_len),
        out_shape=jax.ShapeDtypeStruct(q.shape, BF16),
        grid=(n_batch // n_seq,),
        in_specs=[spec, spec, spec],
        out_specs=spec,
        compiler_params=_params(1),
        name="dense_attention_ctx",
    )(q, k, v)


NA_Q_ROWS = 8
NA_KEY_ROWS = 12


def _na_window_start(r, n_rows):
    return min(max(r - NA_KH // 2, 0), n_rows - NA_KH)


def _na_bias_tile(bias_ref, half, tile, n_rows):
    r0 = tile * NA_Q_ROWS
    kr0 = min(max(r0 - NA_KH // 2, 0), n_rows - NA_KEY_ROWS)
    lo = _lane_lo(GRID_W)
    neg_block = jnp.full((GRID_W, LANES), NEG, F32)
    left_off = jnp.where(lo, NEG, 0.0)
    right_off = jnp.where(lo, 0.0, NEG)
    rows = []
    for rq in range(NA_Q_ROWS):
        r = r0 + rq
        rs = _na_window_start(r, n_rows)
        blocks = []
        for kk in range(NA_KEY_ROWS // 2):
            ka = kr0 + 2 * kk
            va = rs <= ka < rs + NA_KH
            vb = rs <= ka + 1 < rs + NA_KH
            if not (va or vb):
                blocks.append(neg_block)
                continue
            blk = bias_ref[half, ka - r + NA_KH]
            if not va:
                blk = blk + left_off
            if not vb:
                blk = blk + right_off
            blocks.append(blk)
        rows.append(jnp.concatenate(blocks, axis=1))
    return jnp.concatenate(rows, axis=0), kr0


def _na_kernel(q_ref, k_ref, v_ref, ck_ref, cv_ref, bias_ref, o_ref, *, n_rows):
    tq = NA_Q_ROWS * GRID_W
    span = NA_KEY_ROWS * GRID_W
    units = [(tile, half) for tile in range(n_rows // NA_Q_ROWS) for half in range(2)]
    outs = {}

    def window(tile):
        kr0 = min(max(tile * NA_Q_ROWS - NA_KH // 2, 0), n_rows - NA_KEY_ROWS)
        return slice(kr0 * GRID_W, kr0 * GRID_W + span)

    def scores_of(unit):
        tile, half = unit
        q = q_ref[tile * tq:(tile + 1) * tq, :]
        bias, _ = _na_bias_tile(bias_ref, half, tile, n_rows)
        ck_t = jnp.concatenate([ck_ref[0], ck_ref[1]], axis=0)
        own = (lax.broadcasted_iota(jnp.int32, ck_t.shape, 0) < HEAD_DIM) == (half == 0)
        return [_nt(q, _masked_half(k_ref[window(tile), :], half)) + bias,
                _nn(q, jnp.where(own, ck_t, 0.0).astype(BF16))]

    def finish(unit, scores):
        tile, half = unit
        cv_t = jnp.concatenate([cv_ref[0], cv_ref[1]], axis=0).astype(BF16)
        outs[half] = _softmax_apply(scores, [v_ref[window(tile), :], ("t", cv_t)])
        if half == 1:
            o_ref[tile * tq:(tile + 1) * tq, :] = jnp.where(_lane_lo(tq), outs[0], outs[1]).astype(BF16)

    _run_pipelined(units, scores_of, finish)


def _na_bias_table(rpb):
    cidx = np.arange(GRID_W)
    cs = np.clip(cidx - NA_KW // 2, 0, GRID_W - NA_KW)
    kc = np.arange(GRID_W)
    inside = (kc[None, :] >= cs[:, None]) & (kc[None, :] < cs[:, None] + NA_KW)
    rel = kc[None, :] - cidx[:, None] + NA_KW - 1
    onehot = (rel[None] == np.arange(2 * NA_KW - 1)[:, None, None]) & inside[None]
    m = jnp.einsum("hdj,jck->hdck", rpb * LOG2E, jnp.asarray(onehot, F32), precision=lax.Precision.HIGHEST)
    m = jnp.where(jnp.asarray(inside)[None, None], m, NEG)
    neg = jnp.full((rpb.shape[0], 1, GRID_W, GRID_W), NEG, F32)
    left = jnp.concatenate([neg, m], axis=1)
    right = jnp.concatenate([m, neg], axis=1)
    return jnp.concatenate([left, right], axis=-1)


def _na_attention(q, k, v, ctx_k, ctx_v, bias_tab, *, n_batch, seq_len):
    pairs = NA_HEADS // 2
    tok = lambda p, b: (b, p)
    ctx = lambda p, b: (b, p, 0, 0)
    past = ctx_k.shape[-1]
    return pl.pallas_call(
        functools.partial(_na_kernel, n_rows=seq_len // GRID_W),
        out_shape=jax.ShapeDtypeStruct(q.shape, BF16),
        grid=(pairs, n_batch),
        in_specs=[pl.BlockSpec((seq_len, LANES), tok)] * 3
        + [pl.BlockSpec((None, 2, HEAD_DIM, past), ctx)] * 2
        + [pl.BlockSpec((2, 2 * NA_KH, GRID_W, LANES), lambda p, b: (p, 0, 0, 0))],
        out_specs=pl.BlockSpec((seq_len, LANES), tok),
        compiler_params=_params(2),
        name="neighbourhood_attention",
    )(q, k, v, ctx_k, ctx_v, bias_tab)


def _rope_tables(n):
    t = np.arange(n)
    row = (t // GRID_W).astype(np.float64)
    col = (t % GRID_W).astype(np.float64)
    half = HEAD_DIM // 2
    inv = ROPE_BASE ** (-np.arange(0, half, 2, dtype=np.float64) / half)
    ang_r = row[:, None] * inv
    ang_c = col[:, None] * inv
    cos_h = np.concatenate([np.cos(ang_r)] * 2 + [np.cos(ang_c)] * 2, axis=-1)
    sin_h = np.concatenate([-np.sin(ang_r), np.sin(ang_r), -np.sin(ang_c), np.sin(ang_c)], axis=-1)
    return (jnp.asarray(np.concatenate([cos_h, cos_h], axis=-1), F32),
            jnp.asarray(np.concatenate([sin_h, sin_h], axis=-1), F32))


def _ffn_weights(w_up, conv_w, conv_b, w_down):
    return w_up.astype(BF16), conv_w, conv_b[:, None, :], w_down.astype(BF16)


def _token_major_dup(cache):
    b, kv, t, d = cache.shape
    c = jnp.transpose(cache, (0, 2, 1, 3))[:, :, :, None, :]
    return jnp.broadcast_to(c, (b, t, kv, 2, d)).reshape(b, t, kv * 2 * d).astype(BF16)


def _head_transposed(cache):
    return jnp.swapaxes(cache, -1, -2)


def kernel(x_prompt, x_sample, state_ret_fwd, state_ret_bwd, cache_gqa_k, cache_gqa_v, cache_na_k, cache_na_v,
           c, c_ctx, ada_w, ada_b, norm_mix, norm_ffn, norm_final, even_w_in, even_w_out, ret_decay_fwd,
           ret_decay_bwd, ret_gn, gqa_q_norm, gqa_k_norm, odd_w_in, odd_w_out, na_rpb, ffn_w_up, ffn_conv_w,
           ffn_conv_b, ffn_w_down):
    nb_c, len_c, _ = x_prompt.shape
    nb_s, len_s, _ = x_sample.shape
    depth = ada_w.shape[0]
    streams = {
        False: dict(n_batch=nb_c, seq_len=len_c),
        True: dict(n_batch=nb_s, seq_len=len_s),
    }
    xs = {False: x_prompt.reshape(nb_c * len_c, D_MODEL), True: x_sample.reshape(nb_s * len_s, D_MODEL)}

    rows = 8 * (-(-(1 + nb_s) // 8))
    cond = jnp.zeros((rows, D_MODEL), F32).at[0].set(c_ctx).at[1:1 + nb_s].set(c)
    mods = _ada_params(cond, ada_w, ada_b).reshape(depth, rows, 6, 1, D_MODEL)
    rope_tabs = _rope_tables(len_s)
    ffn_w = _ffn_weights(ffn_w_up, ffn_conv_w, ffn_conv_b, ffn_w_down)
    outs = {}
    mixed = {}

    for l in range(depth):
        g_mix = norm_mix[l][None, :]
        g_ffn = norm_ffn[l][None, :]
        if l % 2 == 0:
            e = l // 2
            w_in = even_w_in[e].astype(BF16)
            w_out = even_w_out[e].astype(BF16)
            w_out_parts = [w_out[:RET_V_W], w_out[RET_V_W:]]
            log_g = jnp.stack([jax.nn.log_sigmoid(ret_decay_fwd[e].astype(F32)),
                               jax.nn.log_sigmoid(ret_decay_bwd[e].astype(F32))])
            gn = ret_gn[e][None, :]
            qg2 = jnp.tile(gqa_q_norm[e], 2)[None, :]
            kg2 = jnp.tile(gqa_k_norm[e], 2)[None, :]
            for latent in (False, True):
                st = streams[latent]
                res = _in_proj_even(xs[latent], mods, l, g_mix, w_in, qg2, kg2, rope_tabs, latent=latent, **st)
                qr, kr, kt, vr, gr, qa, kd, vd = res[:8]
                if latent:
                    ret = _retention(log_g, qr, kr, kt, vr, gr, gn, state_ret_fwd[:, e:e + 1],
                                     state_ret_bwd[:, e:e + 1], write_state=False, **st)[0]
                    att = _gqa_attention(qa, kd, vd, _token_major_dup(cache_gqa_k[:, e]),
                                         _token_major_dup(cache_gqa_v[:, e]), tq=GQA_Q_ROWS, **st)
                else:
                    outs.setdefault("gk", []).append(res[8])
                    outs.setdefault("gv", []).append(res[9])
                    ret, s_f, s_b = _retention(log_g, qr, kr, kt, vr, gr, gn, None, None, write_state=True, **st)
                    outs.setdefault("sf", []).append(s_f)
                    outs.setdefault("sb", []).append(s_b)
                    att = _gqa_attention(qa, kd, vd, None, None, tq=st["seq_len"], **st)
                mixed[latent] = ([ret, att], w_out_parts)
        else:
            o = l // 2
            w_in = odd_w_in[o].astype(BF16)
            w_out = odd_w_out[o].astype(BF16)
            for latent in (False, True):
                st = streams[latent]
                res = _in_proj_odd(xs[latent], mods, l, g_mix, w_in, latent=latent, **st)
                q, k, v = res[:3]
                if latent:
                    att = _na_attention(q, k, v, _head_transposed(cache_na_k[:, o]), _head_transposed(cache_na_v[:, o]),
                                        _na_bias_table(na_rpb[o]), **st)
                else:
                    outs.setdefault("nk", []).append(res[3])
                    outs.setdefault("nv", []).append(res[4])
                    att = _dense_attention_ctx(q, k, v, **st)
                mixed[latent] = ([att], [w_out])
        final_g = norm_final[None, :] if l == depth - 1 else None
        for latent in (False, True):
            xs[latent] = _mix_ffn(xs[latent], mods, l, *mixed[latent], g_ffn, *ffn_w, final_g,
                                   seq_len=streams[latent]["seq_len"], latent=latent)

    tr = lambda a: jnp.swapaxes(a, -1, -2)
    cat = lambda name: outs[name][0] if len(outs[name]) == 1 else jnp.concatenate(outs[name], axis=1)
    return (xs[False].reshape(nb_c, len_c, D_MODEL), xs[True].reshape(nb_s, len_s, D_MODEL),
            cat("sf"), cat("sb"), tr(cat("gk")), tr(cat("gv")), tr(cat("nk")), tr(cat("nv")))
```
